```python
import jax, jax.numpy as jnp
from jax import lax
import numpy as np

D_MODEL = 4096
BATCH = 1
SEQ = 8192
DEPTH = 1

N_MEM = 256
D_MIX = D_MODEL
HEAD_DIM = 128
MLA_HEADS = D_MIX // 2 // HEAD_DIM
MLA_NOPE = 128
MLA_ROPE = 64
MLA_V = HEAD_DIM
MLA_Q_RANK = D_MODEL // 4
MLA_KV_RANK = 512
GDN_HEADS = (D_MIX - MLA_HEADS * MLA_V) // HEAD_DIM
GDN_DK = HEAD_DIM
GDN_DV = HEAD_DIM
GDN_CONV = 4
GDN_CHUNK = 64
Q_BLOCK = 128
XATTN_HEADS = 4
XATTN_DIM = D_MODEL // XATTN_HEADS
D_FF = 4 * D_MODEL
ROPE_BASE = 10000.0
NORM_EPS = 1e-6
L2_EPS = 1e-6
GDN_QKV = GDN_HEADS * (2 * GDN_DK + GDN_DV)
IN_SIZES = (MLA_Q_RANK, MLA_KV_RANK, MLA_ROPE, GDN_QKV, GDN_HEADS, GDN_HEADS, GDN_HEADS * GDN_DV)
D_IN = int(sum(IN_SIZES))
IN_SPLITS = tuple(int(v) for v in np.cumsum(IN_SIZES)[:-1])

kernel_name = 'hybrid_mla_gdn_parallel_heads_block'


def _rms_norm(x, w):
    xf = x.astype(jnp.float32)
    y = xf * lax.rsqrt(jnp.mean(xf * xf, axis=-1, keepdims=True) + NORM_EPS) * w.astype(jnp.float32)
    return y.astype(x.dtype)


def _l2_normalize(x):
    xf = x.astype(jnp.float32)
    return xf * lax.rsqrt(jnp.sum(xf * xf, axis=-1, keepdims=True) + L2_EPS)


def _rope(x, positions):
    half = x.shape[-1] // 2
    inv_freq = ROPE_BASE ** (-jnp.arange(half, dtype=jnp.float32) / half)
    ang = positions.astype(jnp.float32)[..., None] * inv_freq
    cos = jnp.cos(ang)[:, :, None, :]
    sin = jnp.sin(ang)[:, :, None, :]
    x1 = x[..., :half].astype(jnp.float32)
    x2 = x[..., half:].astype(jnp.float32)
    out = jnp.concatenate([x1 * cos - x2 * sin, x2 * cos + x1 * sin], axis=-1)
    return out.astype(x.dtype)


def _causal_block_attention(q, k, v, scale):
    B, S, H, Dq = q.shape
    Dv = v.shape[-1]
    nb = S // Q_BLOCK
    q_blocks = jnp.moveaxis(q.reshape(B, nb, Q_BLOCK, H, Dq), 1, 0)
    starts = jnp.arange(nb, dtype=jnp.int32) * Q_BLOCK
    k_pos = jnp.arange(S, dtype=jnp.int32)

    def one_block(args):
        q_blk, start = args
        s = jnp.einsum('bqhd,bkhd->bhqk', q_blk, k, preferred_element_type=jnp.float32) * scale
        q_pos = start + jnp.arange(Q_BLOCK, dtype=jnp.int32)
        causal = k_pos[None, :] <= q_pos[:, None]
        s = jnp.where(causal[None, None], s, -jnp.inf)
        p = jax.nn.softmax(s, axis=-1).astype(v.dtype)
        return jnp.einsum('bhqk,bkhd->bqhd', p, v)

    out = lax.map(one_block, (q_blocks, starts))
    return jnp.moveaxis(out, 0, 1).reshape(B, S, H, Dv)


def _causal_depthwise_conv(x, w):
    K, C = w.shape
    return lax.conv_general_dilated(x, w[:, None, :].astype(x.dtype), window_strides=(1,),
                                    padding=[(K - 1, 0)], dimension_numbers=('NWC', 'WIO', 'NWC'),
                                    feature_group_count=C)


def _gated_delta_rule_chunked(q, k, v, g, beta):
    B, S, H, Dk = k.shape
    Dv = v.shape[-1]
    N, C = S // GDN_CHUNK, GDN_CHUNK
    f32 = jnp.float32

    def chunks(t):
        t = jnp.swapaxes(t.astype(f32), 1, 2)
        return t.reshape((B, H, N, C) + t.shape[3:])

    qc = chunks(q) * (Dk ** -0.5)
    kc, vc, gc, bc = chunks(k), chunks(v), chunks(g), chunks(beta)
    gc = jnp.cumsum(gc, axis=-1)
    tril = jnp.tril(jnp.ones((C, C), dtype=bool))
    strict = jnp.tril(jnp.ones((C, C), dtype=bool), k=-1)
    decay = jnp.exp(jnp.where(tril, gc[..., :, None] - gc[..., None, :], -jnp.inf))
    k_beta = kc * bc[..., None]
    v_beta = vc * bc[..., None]
    a_mat = jnp.where(strict, jnp.einsum('bhncd,bhnjd->bhncj', k_beta, kc) * decay, 0.0)
    lower = a_mat + jnp.eye(C, dtype=f32)
    u = lax.linalg.triangular_solve(lower, v_beta, left_side=True, lower=True, unit_diagonal=True)
    w = lax.linalg.triangular_solve(lower, k_beta * jnp.exp(gc)[..., None], left_side=True, lower=True,
                                    unit_diagonal=True)
    qk = jnp.einsum('bhncd,bhnjd->bhncj', qc, kc) * decay
    g_last = gc[..., -1]
    q_dec = qc * jnp.exp(gc)[..., None]
    k_tail = kc * jnp.exp(g_last[..., None] - gc)[..., None]
    chunk_decay = jnp.exp(g_last)

    def step(state, xs):
        u_n, w_n, qk_n, qd_n, kt_n, cd_n = xs
        v_new = u_n - jnp.einsum('bhcd,bhde->bhce', w_n, state)
        o_n = jnp.einsum('bhcd,bhde->bhce', qd_n, state) + jnp.einsum('bhcj,bhje->bhce', qk_n, v_new)
        state = state * cd_n[..., None, None] + jnp.einsum('bhcd,bhce->bhde', kt_n, v_new)
        return state, o_n

    xs = tuple(jnp.moveaxis(t, 2, 0) for t in (u, w, qk, q_dec, k_tail, chunk_decay))
    state0 = jnp.zeros((B, H, Dk, Dv), f32)
    _, o = lax.scan(step, state0, xs)
    o = jnp.transpose(o, (1, 0, 3, 2, 4)).reshape(B, S, H, Dv)
    return o.astype(v.dtype)


def _hybrid_mixer(xn, positions, w_in, q_norm_w, w_q_b, kv_norm_w, w_kv_b,
                  conv_w, a_log, dt_bias, gdn_norm_w, w_out):
    B, S, _ = xn.shape
    proj = xn @ w_in
    c_q, c_kv, k_pe, qkv, a_logit, b_logit, gate = jnp.split(proj, IN_SPLITS, axis=-1)

    q = (_rms_norm(c_q, q_norm_w) @ w_q_b).reshape(B, S, MLA_HEADS, MLA_NOPE + MLA_ROPE)
    kv = (_rms_norm(c_kv, kv_norm_w) @ w_kv_b).reshape(B, S, MLA_HEADS, MLA_NOPE + MLA_V)
    q_nope, q_pe = q[..., :MLA_NOPE], _rope(q[..., MLA_NOPE:], positions)
    k_nope, v = kv[..., :MLA_NOPE], kv[..., MLA_NOPE:]
    k_pe = _rope(k_pe[:, :, None, :], positions)
    q_full = jnp.concatenate([q_nope, q_pe], axis=-1)
    k_full = jnp.concatenate([k_nope, jnp.broadcast_to(k_pe, (B, S, MLA_HEADS, MLA_ROPE))], axis=-1)
    mla_out = _causal_block_attention(q_full, k_full, v, (MLA_NOPE + MLA_ROPE) ** -0.5)

    qkv = jax.nn.silu(_causal_depthwise_conv(qkv, conv_w))
    gq, gk, gv = jnp.split(qkv, (GDN_HEADS * GDN_DK, 2 * GDN_HEADS * GDN_DK), axis=-1)
    gq = _l2_normalize(gq.reshape(B, S, GDN_HEADS, GDN_DK))
    gk = _l2_normalize(gk.reshape(B, S, GDN_HEADS, GDN_DK))
    gv = gv.reshape(B, S, GDN_HEADS, GDN_DV)
    log_decay = -jnp.exp(a_log.astype(jnp.float32)) * jax.nn.softplus(
        a_logit.astype(jnp.float32) + dt_bias.astype(jnp.float32))
    beta = jax.nn.sigmoid(b_logit.astype(jnp.float32))
    gdn_out = _gated_delta_rule_chunked(gq, gk, gv, log_decay, beta)
    gdn_out = _rms_norm(gdn_out, gdn_norm_w) * jax.nn.silu(gate.reshape(B, S, GDN_HEADS, GDN_DV))

    mix = jnp.concatenate([mla_out.reshape(B, S, MLA_HEADS * MLA_V),
                           gdn_out.reshape(B, S, GDN_HEADS * GDN_DV)], axis=-1)
    return mix @ w_out


def _memory_cross_attention(hn, memn, wq, wk, wv, wo):
    B, S, _ = hn.shape
    M = memn.shape[1]
    q = (hn @ wq).reshape(B, S, XATTN_HEADS, XATTN_DIM)
    k = (memn @ wk).reshape(B, M, XATTN_HEADS, XATTN_DIM)
    v = (memn @ wv).reshape(B, M, XATTN_HEADS, XATTN_DIM)
    s = jnp.einsum('bqhd,bkhd->bhqk', q, k, preferred_element_type=jnp.float32) * (XATTN_DIM ** -0.5)
    p = jax.nn.softmax(s, axis=-1).astype(v.dtype)
    o = jnp.einsum('bhqk,bkhd->bqhd', p, v).reshape(B, S, XATTN_HEADS * XATTN_DIM)
    return o @ wo


def _squared_relu_mlp(hn, w_up, w_down):
    return jnp.square(jax.nn.relu(hn @ w_up)) @ w_down


def setup_inputs(seed: int = 0) -> dict:
    key = jax.random.key(seed)
    ks = jax.random.split(key, 26)
    f32 = jnp.float32

    def lin(k, fan_in, fan_out):
        return jax.random.normal(k, (DEPTH, fan_in, fan_out), f32) * fan_in ** -0.5

    def gain(k, n):
        return 1.0 + 0.02 * jax.random.normal(k, (DEPTH, n), f32)

    x = jax.random.normal(ks[0], (BATCH, SEQ, D_MODEL), f32)
    mem = jax.random.normal(ks[1], (BATCH, N_MEM, D_MODEL), f32)
    start = jax.random.randint(ks[2], (BATCH, 1), 0, 4096, dtype=jnp.int32)
    positions = start + jnp.arange(SEQ, dtype=jnp.int32)[None, :]
    dt = jnp.exp(jax.random.uniform(ks[11], (DEPTH, GDN_HEADS), f32, minval=np.log(1e-3), maxval=np.log(1e-1)))
    return {
        'x': x,
        'mem': mem,
        'positions': positions,
        'attn_norm_w': gain(ks[3], D_MODEL),
        'w_in': lin(ks[4], D_MODEL, D_IN),
        'mla_q_norm_w': gain(ks[5], MLA_Q_RANK),
        'mla_w_q_b': lin(ks[6], MLA_Q_RANK, MLA_HEADS * (MLA_NOPE + MLA_ROPE)),
        'mla_kv_norm_w': gain(ks[7], MLA_KV_RANK),
        'mla_w_kv_b': lin(ks[8], MLA_KV_RANK, MLA_HEADS * (MLA_NOPE + MLA_V)),
        'gdn_conv_w': jax.random.normal(ks[9], (DEPTH, GDN_CONV, GDN_QKV), f32) * GDN_CONV ** -0.5,
        'gdn_a_log': jnp.log(jax.random.uniform(ks[10], (DEPTH, GDN_HEADS), f32, minval=1.0, maxval=16.0)),
        'gdn_dt_bias': dt + jnp.log(-jnp.expm1(-dt)),
        'gdn_norm_w': gain(ks[12], GDN_DV),
        'w_out': lin(ks[13], D_MIX, D_MODEL),
        'xattn_norm_w': gain(ks[14], D_MODEL),
        'mem_norm_w': gain(ks[15], D_MODEL),
        'xattn_wq': lin(ks[16], D_MODEL, XATTN_HEADS * XATTN_DIM),
        'xattn_wk': lin(ks[17], D_MODEL, XATTN_HEADS * XATTN_DIM),
        'xattn_wv': lin(ks[18], D_MODEL, XATTN_HEADS * XATTN_DIM),
        'xattn_wo': lin(ks[19], XATTN_HEADS * XATTN_DIM, D_MODEL),
        'mlp_norm_w': gain(ks[20], D_MODEL),
        'mlp_w_up': lin(ks[21], D_MODEL, D_FF),
        'mlp_w_down': lin(ks[22], D_FF, D_MODEL),
        'final_norm_w': 1.0 + 0.02 * jax.random.normal(ks[23], (D_MODEL,), f32),
    }


def reference(x, mem, positions, attn_norm_w, w_in, mla_q_norm_w, mla_w_q_b, mla_kv_norm_w, mla_w_kv_b,
              gdn_conv_w, gdn_a_log, gdn_dt_bias, gdn_norm_w, w_out, xattn_norm_w, mem_norm_w,
              xattn_wq, xattn_wk, xattn_wv, xattn_wo, mlp_norm_w, mlp_w_up, mlp_w_down, final_norm_w):
    h = x
    for l in range(DEPTH):
        h = h + _hybrid_mixer(_rms_norm(h, attn_norm_w[l]), positions, w_in[l],
                              mla_q_norm_w[l], mla_w_q_b[l], mla_kv_norm_w[l], mla_w_kv_b[l],
                              gdn_conv_w[l], gdn_a_log[l], gdn_dt_bias[l], gdn_norm_w[l], w_out[l])
        h = h + _memory_cross_attention(_rms_norm(h, xattn_norm_w[l]), _rms_norm(mem, mem_norm_w[l]),
                                        xattn_wq[l], xattn_wk[l], xattn_wv[l], xattn_wo[l])
        h = h + _squared_relu_mlp(_rms_norm(h, mlp_norm_w[l]), mlp_w_up[l], mlp_w_down[l])
    return _rms_norm(h, final_norm_w)
```

```python
import functools
import math

import jax
import jax.numpy as jnp
from jax import lax
from jax.experimental import pallas as pl
from jax.experimental.pallas import tpu as pltpu

F32 = jnp.float32
BF16 = jnp.bfloat16

HEAD_DIM = 128
MLA_NOPE = 128
MLA_ROPE = 64
MLA_Q_RANK = 1024
MLA_KV_RANK = 512
GDN_CONV = 4
XATTN_HEADS = 4
ROPE_BASE = 10000.0
NORM_EPS = 1e-6
L2_EPS = 1e-6
LOG2E = math.log2(math.e)

V7X_VMEM_BYTES = 64 * 1024 * 1024
VMEM_LIMIT = 52 * 1024 * 1024
LANES = 128

GDN_CHUNK = 128


def _cparams(sem):
    return pltpu.CompilerParams(dimension_semantics=sem, vmem_limit_bytes=VMEM_LIMIT)


def _rms_kernel(x_ref, w_ref, o_ref):
    x = x_ref[...].astype(F32)
    r = lax.rsqrt(jnp.mean(x * x, axis=-1, keepdims=True) + NORM_EPS)
    o_ref[...] = (x * r * w_ref[...]).astype(o_ref.dtype)


def _rms_add_kernel(x_ref, y_ref, w_ref, o_ref):
    x = x_ref[...] + y_ref[...]
    r = lax.rsqrt(jnp.mean(x * x, axis=-1, keepdims=True) + NORM_EPS)
    o_ref[...] = (x * r * w_ref[...]).astype(o_ref.dtype)


def rms_norm(x, w, out_dtype, tm, add=None):
    M, D = x.shape
    tm = min(tm, M)
    row = pl.BlockSpec((tm, D), lambda i: (i, 0))
    wspec = pl.BlockSpec((1, D), lambda i: (0, 0))
    if add is None:
        kern, specs, args = _rms_kernel, [row, wspec], (x, w.reshape(1, D))
    else:
        kern, specs, args = _rms_add_kernel, [row, row, wspec], (x, add, w.reshape(1, D))
    return pl.pallas_call(
        kern, grid=(M // tm,), in_specs=specs, out_specs=row,
        out_shape=jax.ShapeDtypeStruct((M, D), out_dtype),
        compiler_params=_cparams(("parallel",)), name="rms_norm")(*args)


def _mm_kernel(*refs, nk, relu2, has_res, out_scale):
    x_ref, w_ref = refs[0], refs[1]
    res_ref = refs[2] if has_res else None
    o_ref = refs[2 + has_res]

    def finish(r):
        if relu2:
            r = jnp.square(jnp.maximum(r, 0.0))
        if out_scale is not None:
            r = r * out_scale
        if has_res:
            r = r + res_ref[...]
        o_ref[...] = r.astype(o_ref.dtype)

    part = jnp.dot(x_ref[...], w_ref[...].astype(BF16), preferred_element_type=F32)
    if nk == 1:
        finish(part)
        return
    acc_ref = refs[3 + has_res]
    k = pl.program_id(2)

    @pl.when(k == 0)
    def _():
        acc_ref[...] = part

    @pl.when(jnp.logical_and(k > 0, k < nk - 1))
    def _():
        acc_ref[...] += part

    @pl.when(k == nk - 1)
    def _():
        finish(acc_ref[...] + part)


def matmul(x, w, *, tm, tn, tk, out_dtype, relu2=False, res=None, out_scale=None, name="matmul"):
    M, K = x.shape
    K2, N = w.shape
    assert K == K2
    tm, tn, tk = min(tm, M), min(tn, N), min(tk, K)
    assert M % tm == 0 and N % tn == 0 and K % tk == 0
    nk = K // tk
    in_specs = [pl.BlockSpec((tm, tk), lambda i, j, k: (i, k)),
                pl.BlockSpec((tk, tn), lambda i, j, k: (k, j))]
    args = [x, w]
    if res is not None:
        in_specs.append(pl.BlockSpec((tm, tn), lambda i, j, k: (i, j)))
        args.append(res)
    scratch = [pltpu.VMEM((tm, tn), F32)] if nk > 1 else []
    kern = functools.partial(_mm_kernel, nk=nk, relu2=relu2, has_res=res is not None, out_scale=out_scale)
    return pl.pallas_call(
        kern, grid=(M // tm, N // tn, nk), in_specs=in_specs,
        out_specs=pl.BlockSpec((tm, tn), lambda i, j, k: (i, j)),
        out_shape=jax.ShapeDtypeStruct((M, N), out_dtype),
        scratch_shapes=scratch,
        compiler_params=_cparams(("parallel", "parallel", "arbitrary")), name=name)(*args)


def _gates_kernel(y_ref, pos_ref, invf_ref, alog_ref, dtb_ref, aux_ref, cos_ref, sin_ref):
    y = y_ref[...]
    ang = pos_ref[...].astype(F32) * invf_ref[...]
    c = jnp.cos(ang)
    s = jnp.sin(ang)
    cos_ref[...] = c
    sin_ref[...] = s
    lane = lax.broadcasted_iota(jnp.int32, y.shape, 1)
    half = MLA_ROPE // 2
    rot = jnp.where(lane < half, -pltpu.roll(y, LANES - half, 1), pltpu.roll(y, half, 1))
    roped = y * c + rot * s
    z = y + dtb_ref[...]
    softplus = jnp.maximum(z, 0.0) + jnp.log1p(jnp.exp(-jnp.abs(z)))
    g = -jnp.exp(alog_ref[...]) * softplus
    beta = jax.nn.sigmoid(y)
    aux_ref[...] = jnp.where(lane < MLA_ROPE, roped, jnp.where(lane < MLA_ROPE + 16, g, beta))


def gates(y, pos_col, invf_row, alog_row, dtb_row, tm):
    M = y.shape[0]
    tm = min(tm, M)
    row = pl.BlockSpec((tm, LANES), lambda i: (i, 0))
    one = pl.BlockSpec((1, LANES), lambda i: (0, 0))
    shp = jax.ShapeDtypeStruct((M, LANES), F32)
    return pl.pallas_call(
        _gates_kernel, grid=(M // tm,),
        in_specs=[row, pl.BlockSpec((tm, 1), lambda i: (i, 0)), one, one, one],
        out_specs=[row, row, row], out_shape=[shp, shp, shp],
        compiler_params=_cparams(("parallel",)), name="gates")(y, pos_col, invf_row, alog_row, dtb_row)


def _mla_q_kernel(cq_ref, nw_ref, wq_ref, cos_ref, sin_ref, o_ref, *, heads, scale):
    x = cq_ref[...].astype(F32)
    r = lax.rsqrt(jnp.mean(x * x, axis=-1, keepdims=True) + NORM_EPS)
    xn = (x * r * nw_ref[...]).astype(BF16)
    q = jnp.dot(xn, wq_ref[...], preferred_element_type=F32) * scale
    half = MLA_ROPE // 2
    n_nope = heads * MLA_NOPE
    n_half = heads * half
    reps = n_half // LANES
    c = jnp.concatenate([cos_ref[...]] * reps, axis=1)
    s = jnp.concatenate([sin_ref[...]] * reps, axis=1)
    p1 = q[:, n_nope:n_nope + n_half]
    p2 = q[:, n_nope + n_half:]
    r1t = (p1 * c - p2 * s).T.astype(BF16)
    r2t = (p2 * c + p1 * s).T.astype(BF16)
    for h in range(heads):
        o_ref[h, 0:MLA_NOPE, :] = q[:, h * MLA_NOPE:(h + 1) * MLA_NOPE].T.astype(BF16)
        o_ref[h, MLA_NOPE:MLA_NOPE + half, :] = r1t[h * half:(h + 1) * half, :]
        o_ref[h, MLA_NOPE + half:MLA_NOPE + 2 * half, :] = r2t[h * half:(h + 1) * half, :]


def mla_q(proj, nw, wq, cos, sin, *, heads, tm, scale):
    S = proj.shape[0]
    R = nw.shape[0]
    dq = MLA_NOPE + MLA_ROPE
    kern = functools.partial(_mla_q_kernel, heads=heads, scale=scale)
    return pl.pallas_call(
        kern, grid=(S // tm,),
        in_specs=[pl.BlockSpec((tm, R), lambda i: (i, 0)),
                  pl.BlockSpec((1, R), lambda i: (0, 0)),
                  pl.BlockSpec(wq.shape, lambda i: (0, 0)),
                  pl.BlockSpec((tm, LANES), lambda i: (i, 0)),
                  pl.BlockSpec((tm, LANES), lambda i: (i, 0))],
        out_specs=pl.BlockSpec((heads, dq, tm), lambda i: (0, 0, i)),
        out_shape=jax.ShapeDtypeStruct((heads, dq, S), BF16),
        compiler_params=_cparams(("parallel",)), name="mla_q")(proj, nw.reshape(1, R), wq, cos, sin)


def _mla_kv_kernel(ckv_ref, nw_ref, wkv_ref, aux_ref, k_ref, vt_ref, *, heads):
    x = ckv_ref[...].astype(F32)
    r = lax.rsqrt(jnp.mean(x * x, axis=-1, keepdims=True) + NORM_EPS)
    xn = (x * r * nw_ref[...]).astype(BF16)
    kv = jnp.dot(xn, wkv_ref[...], preferred_element_type=F32)
    kpe = aux_ref[...][:, 0:MLA_ROPE].astype(BF16)
    n_nope = heads * MLA_NOPE
    for h in range(heads):
        k_ref[h, :, 0:MLA_NOPE] = kv[:, h * MLA_NOPE:(h + 1) * MLA_NOPE].astype(BF16)
        k_ref[h, :, MLA_NOPE:MLA_NOPE + MLA_ROPE] = kpe
        vt_ref[h, 0] = kv[:, n_nope + h * HEAD_DIM:n_nope + (h + 1) * HEAD_DIM].T.astype(BF16)


def mla_kv(proj, col_block, nw, wkv, aux, *, heads, tm):
    S = proj.shape[0]
    R = nw.shape[0]
    dq = MLA_NOPE + MLA_ROPE
    kern = functools.partial(_mla_kv_kernel, heads=heads)
    return pl.pallas_call(
        kern, grid=(S // tm,),
        in_specs=[pl.BlockSpec((tm, R), lambda i: (i, col_block)),
                  pl.BlockSpec((1, R), lambda i: (0, 0)),
                  pl.BlockSpec(wkv.shape, lambda i: (0, 0)),
                  pl.BlockSpec((tm, LANES), lambda i: (i, 0))],
        out_specs=[pl.BlockSpec((heads, tm, dq), lambda i: (0, i, 0)),
                   pl.BlockSpec((heads, 1, HEAD_DIM, tm), lambda i: (0, i, 0, 0))],
        out_shape=[jax.ShapeDtypeStruct((heads, S, dq), BF16),
                   jax.ShapeDtypeStruct((heads, S // tm, HEAD_DIM, tm), BF16)],
        compiler_params=_cparams(("parallel",)), name="mla_kv")(proj, nw.reshape(1, R), wkv, aux)


def _flash_kernel(qt_ref, k_ref, vt_ref, o_ref, acc_ref, m_ref, l_ref, *, t):
    i = pl.program_id(1)
    qt = qt_ref[...]
    m_ref[...] = jnp.full(m_ref.shape, -jnp.inf, F32)
    l_ref[...] = jnp.zeros(l_ref.shape, F32)
    acc_ref[...] = jnp.zeros(acc_ref.shape, F32)

    def tile(j, masked):
        s = jnp.dot(k_ref[j], qt, preferred_element_type=F32)
        if masked:
            kpos = lax.broadcasted_iota(jnp.int32, s.shape, 0)
            qpos = lax.broadcasted_iota(jnp.int32, s.shape, 1)
            s = jnp.where(kpos <= qpos, s, -jnp.inf)
        m_old = m_ref[...]
        m_new = jnp.maximum(m_old, jnp.max(s, axis=0, keepdims=True))
        alpha = jnp.exp2(m_old - m_new)
        p = jnp.exp2(s - m_new)
        l_ref[...] = alpha * l_ref[...] + jnp.sum(p, axis=0, keepdims=True)
        acc_ref[...] = alpha * acc_ref[...] + jnp.dot(vt_ref[j], p.astype(BF16), preferred_element_type=F32)
        m_ref[...] = m_new

    def body(j, carry):
        tile(j, False)
        return carry

    lax.fori_loop(0, i, body, 0)
    tile(i, True)
    o_ref[...] = (acc_ref[...] / l_ref[...]).T.astype(o_ref.dtype)


def flash_attention(qt, k4, vt4, *, t):
    H, dq, S = qt.shape
    nk = S // t
    kern = functools.partial(_flash_kernel, t=t)
    return pl.pallas_call(
        kern, grid=(H, nk),
        in_specs=[pl.BlockSpec((None, dq, t), lambda h, i: (h, 0, i)),
                  pl.BlockSpec((None, nk, t, dq), lambda h, i: (h, 0, 0, 0)),
                  pl.BlockSpec((None, nk, HEAD_DIM, t), lambda h, i: (h, 0, 0, 0))],
        out_specs=pl.BlockSpec((t, HEAD_DIM), lambda h, i: (i, h)),
        out_shape=jax.ShapeDtypeStruct((S, H * HEAD_DIM), BF16),
        scratch_shapes=[pltpu.VMEM((HEAD_DIM, t), F32), pltpu.VMEM((1, t), F32), pltpu.VMEM((1, t), F32)],
        compiler_params=_cparams(("parallel", "arbitrary")), name="mla_flash")(qt, k4, vt4)


def _conv_kernel(x_ref, halo_ref, w_ref, *o_refs, l2norm, out_scale, emit_t, heads_per_block):
    i = pl.program_id(0)
    x = x_ref[...].astype(F32)
    halo = halo_ref[...].astype(F32)[8:16, :]
    halo = jnp.where(i == 0, 0.0, halo)
    xcat = jnp.concatenate([halo, x], axis=0)
    w = w_ref[...]
    acc = x * w[GDN_CONV - 1:GDN_CONV, :]
    for s in range(1, GDN_CONV):
        acc = acc + pltpu.roll(xcat, s, 0)[8:, :] * w[GDN_CONV - 1 - s:GDN_CONV - s, :]
    y = acc * jax.nn.sigmoid(acc)
    if l2norm:
        parts = []
        for h in range(heads_per_block):
            yh = y[:, h * HEAD_DIM:(h + 1) * HEAD_DIM]
            parts.append(yh * (lax.rsqrt(jnp.sum(yh * yh, axis=-1, keepdims=True) + L2_EPS) * out_scale))
        y = jnp.concatenate(parts, axis=1)
    o_refs[0][...] = y
    if emit_t:
        o_refs[1][...] = y.T


def gdn_conv(proj, col_block0, conv_w, wcol_block0, n_cols, *, tm, tc, l2norm, out_scale=1.0, emit_t=False):
    S = proj.shape[0]
    kern = functools.partial(_conv_kernel, l2norm=l2norm, out_scale=out_scale, emit_t=emit_t,
                             heads_per_block=tc // HEAD_DIM)
    out_specs = [pl.BlockSpec((tm, tc), lambda i, j: (i, j))]
    out_shape = [jax.ShapeDtypeStruct((S, n_cols), F32)]
    if emit_t:
        out_specs.append(pl.BlockSpec((tc, tm), lambda i, j: (j, i)))
        out_shape.append(jax.ShapeDtypeStruct((n_cols, S), F32))
    hb = tm // 16
    return pl.pallas_call(
        kern, grid=(S // tm, n_cols // tc),
        in_specs=[pl.BlockSpec((tm, tc), lambda i, j: (i, col_block0 + j)),
                  pl.BlockSpec((16, tc), lambda i, j: (jnp.maximum(i * hb - 1, 0), col_block0 + j)),
                  pl.BlockSpec((GDN_CONV, tc), lambda i, j: (0, wcol_block0 + j))],
        out_specs=out_specs, out_shape=out_shape,
        compiler_params=_cparams(("parallel", "parallel")), name="gdn_conv")(proj, proj, conv_w)


def _split_dot(a, b, passes):
    a_hi = a.astype(BF16)
    b_hi = b.astype(BF16)
    out = jnp.dot(a_hi, b_hi, preferred_element_type=F32)
    if passes == 3:
        a_lo = (a - a_hi.astype(F32)).astype(BF16)
        b_lo = (b - b_hi.astype(F32)).astype(BF16)
        out = out + jnp.dot(a_hi, b_lo, preferred_element_type=F32) + jnp.dot(a_lo, b_hi, preferred_element_type=F32)
    return out


def _cumsum_kernel(g_ref, o_ref, *, chunk):
    r = lax.broadcasted_iota(jnp.int32, (chunk, chunk), 0)
    c = lax.broadcasted_iota(jnp.int32, (chunk, chunk), 1)
    tril = jnp.where(r >= c, 1.0, 0.0).astype(F32)
    g = g_ref[...]
    g_hi = g.astype(BF16).astype(F32)
    g_mid = (g - g_hi).astype(BF16).astype(F32)
    g_lo = g - g_hi - g_mid
    trilb = tril.astype(BF16)
    o_ref[...] = (jnp.dot(trilb, g_hi.astype(BF16), preferred_element_type=F32)
                  + jnp.dot(trilb, g_mid.astype(BF16), preferred_element_type=F32)
                  + jnp.dot(trilb, g_lo.astype(BF16), preferred_element_type=F32))


def chunk_cumsum(g, chunk):
    M = g.shape[0]
    row = pl.BlockSpec((chunk, LANES), lambda i: (i, 0))
    return pl.pallas_call(
        functools.partial(_cumsum_kernel, chunk=chunk), grid=(M // chunk,), in_specs=[row], out_specs=row,
        out_shape=jax.ShapeDtypeStruct((M, LANES), F32),
        compiler_params=_cparams(("parallel",)), name="gdn_cumsum")(g)


def _gdn_kernel(q_ref, k_ref, kt_ref, v_ref, gate_ref, gcc_ref, bc_ref, gcr_ref, nw_ref, o_ref, s_ref,
                *, hb, rows, chunk, solve_passes, state_passes):
    @pl.when(pl.program_id(1) == 0)
    def _():
        s_ref[...] = jnp.zeros(s_ref.shape, F32)

    C = chunk
    ri = lax.broadcasted_iota(jnp.int32, (C, C), 0)
    ci = lax.broadcasted_iota(jnp.int32, (C, C), 1)
    lower = ri >= ci
    strict = ri > ci
    nw = nw_ref[...]
    n_steps = C.bit_length() - 1

    for c in range(rows // C):
        r0 = c * C
        for h in range(hb):
            c0 = h * HEAD_DIM
            q = q_ref[r0:r0 + C, c0:c0 + HEAD_DIM]
            k = k_ref[r0:r0 + C, c0:c0 + HEAD_DIM]
            v = v_ref[r0:r0 + C, c0:c0 + HEAD_DIM]
            kt = kt_ref[c0:c0 + HEAD_DIM, r0:r0 + C]
            gcc = gcc_ref[r0:r0 + C, h:h + 1]
            bc = bc_ref[r0:r0 + C, h:h + 1]
            gcr = gcr_ref[h:h + 1, r0:r0 + C]
            glast = gcr[:, C - 1:C]

            decay = jnp.exp(jnp.where(lower, gcc - gcr, -jnp.inf))
            egc = jnp.exp(gcc)
            kb = k * bc
            a_mat = jnp.where(strict, _split_dot(kb, kt, solve_passes) * decay, 0.0)
            x = jnp.concatenate([v * bc, kb * egc], axis=1)
            p = -a_mat
            for step in range(n_steps):
                x = x + _split_dot(p, x, solve_passes)
                if step + 1 < n_steps:
                    p = _split_dot(p, p, solve_passes)
            u = x[:, :HEAD_DIM]
            w = x[:, HEAD_DIM:]
            qk = jnp.where(lower, _split_dot(q, kt, state_passes) * decay, 0.0)
            state = s_ref[h]
            ws = _split_dot(jnp.concatenate([w, q * egc], axis=0), state, state_passes)
            v_new = u - ws[:C]
            o = ws[C:] + _split_dot(qk, v_new, state_passes)
            kt_tail = kt * jnp.exp(glast - gcr)
            s_ref[h] = state * jnp.exp(glast) + _split_dot(kt_tail, v_new, state_passes)

            o = o * lax.rsqrt(jnp.mean(o * o, axis=-1, keepdims=True) + NORM_EPS) * nw
            gate = gate_ref[r0:r0 + C, c0:c0 + HEAD_DIM].astype(F32)
            o_ref[r0:r0 + C, c0:c0 + HEAD_DIM] = (o * (gate * jax.nn.sigmoid(gate))).astype(o_ref.dtype)


def gdn_delta(q, k, kt, v, proj, gate_block0, gcc, bc, gcr, nw, *, hb, rows, chunk, solve_passes, state_passes):
    S, W = q.shape
    tc = hb * HEAD_DIM
    kern = functools.partial(_gdn_kernel, hb=hb, rows=rows, chunk=chunk,
                             solve_passes=solve_passes, state_passes=state_passes)
    blk = pl.BlockSpec((rows, tc), lambda g, r: (r, g))
    return pl.pallas_call(
        kern, grid=(W // tc, S // rows),
        in_specs=[blk, blk, pl.BlockSpec((tc, rows), lambda g, r: (g, r)), blk,
                  pl.BlockSpec((rows, tc), lambda g, r: (r, gate_block0 + g)),
                  pl.BlockSpec((None, rows, hb), lambda g, r: (g, r, 0)),
                  pl.BlockSpec((None, rows, hb), lambda g, r: (g, r, 0)),
                  pl.BlockSpec((None, hb, rows), lambda g, r: (g, 0, r)),
                  pl.BlockSpec((1, HEAD_DIM), lambda g, r: (0, 0))],
        out_specs=blk,
        out_shape=jax.ShapeDtypeStruct((S, W), BF16),
        scratch_shapes=[pltpu.VMEM((hb, HEAD_DIM, HEAD_DIM), F32)],
        compiler_params=_cparams(("parallel", "arbitrary")), name="gdn_delta")(
            q, k, kt, v, proj, gcc, bc, gcr, nw.reshape(1, HEAD_DIM))


def _xattn_kernel(q_ref, kt_ref, v_ref, o_ref):
    s = jnp.dot(q_ref[...], kt_ref[...], preferred_element_type=F32)
    m = jnp.max(s, axis=-1, keepdims=True)
    p = jnp.exp2(s - m)
    l = jnp.sum(p, axis=-1, keepdims=True)
    o = jnp.dot(p.astype(BF16), v_ref[...], preferred_element_type=F32)
    o_ref[...] = (o / l).astype(o_ref.dtype)


def xattn_core(q, kt, v, *, heads, tq):
    S, W = q.shape
    M = v.shape[0]
    d = W // heads
    return pl.pallas_call(
        _xattn_kernel, grid=(heads, S // tq),
        in_specs=[pl.BlockSpec((tq, d), lambda h, i: (i, h)),
                  pl.BlockSpec((d, M), lambda h, i: (h, 0)),
                  pl.BlockSpec((M, d), lambda h, i: (0, h))],
        out_specs=pl.BlockSpec((tq, d), lambda h, i: (i, h)),
        out_shape=jax.ShapeDtypeStruct((S, W), BF16),
        compiler_params=_cparams(("parallel", "parallel")), name="xattn_core")(q, kt, v)


def _block(x, mem, positions, attn_norm_w, w_in, mla_q_norm_w, mla_w_q_b, mla_kv_norm_w, mla_w_kv_b,
           gdn_conv_w, gdn_a_log, gdn_dt_bias, gdn_norm_w, w_out, xattn_norm_w, mem_norm_w,
           xattn_wq, xattn_wk, xattn_wv, xattn_wo, mlp_norm_w, mlp_w_up, mlp_w_down, final_norm_w):
    S, D = x.shape
    n_mix = w_out.shape[0]
    mla_heads = n_mix // 2 // HEAD_DIM
    gdn_heads = gdn_conv_w.shape[1] // (3 * HEAD_DIM)
    gw = gdn_heads * HEAD_DIM
    dq = MLA_NOPE + MLA_ROPE
    half = MLA_ROPE // 2

    o_kv = MLA_Q_RANK
    o_pe = o_kv + MLA_KV_RANK
    o_qkv = o_pe + MLA_ROPE
    o_a = o_qkv + 3 * gw
    o_b = o_a + gdn_heads
    o_gate = o_b + gdn_heads
    w_main = jnp.concatenate([w_in[:, :o_pe], w_in[:, o_qkv:o_a], w_in[:, o_gate:]], axis=1).astype(BF16)
    w_small = jnp.concatenate([w_in[:, o_pe:o_qkv], w_in[:, o_a:o_gate],
                               jnp.zeros((D, LANES - MLA_ROPE - 2 * gdn_heads), F32)], axis=1)
    c_qkv = o_pe
    c_gate = o_pe + 3 * gw

    wq3 = mla_w_q_b.reshape(MLA_Q_RANK, mla_heads, dq)
    wq_perm = jnp.concatenate([wq3[:, :, :MLA_NOPE].reshape(MLA_Q_RANK, -1),
                               wq3[:, :, MLA_NOPE:MLA_NOPE + half].reshape(MLA_Q_RANK, -1),
                               wq3[:, :, MLA_NOPE + half:].reshape(MLA_Q_RANK, -1)], axis=1).astype(BF16)
    wkv3 = mla_w_kv_b.reshape(MLA_KV_RANK, mla_heads, MLA_NOPE + HEAD_DIM)
    wkv_perm = jnp.concatenate([wkv3[:, :, :MLA_NOPE].reshape(MLA_KV_RANK, -1),
                                wkv3[:, :, MLA_NOPE:].reshape(MLA_KV_RANK, -1)], axis=1).astype(BF16)

    inv_freq = ROPE_BASE ** (-jnp.arange(half, dtype=F32) / half)
    invf_row = jnp.tile(inv_freq, LANES // half).reshape(1, LANES)
    pad_l = jnp.zeros((MLA_ROPE,), F32)
    pad_r = jnp.zeros((LANES - MLA_ROPE - gdn_heads,), F32)
    alog_row = jnp.concatenate([pad_l, gdn_a_log.astype(F32), pad_r]).reshape(1, LANES)
    dtb_row = jnp.concatenate([pad_l, gdn_dt_bias.astype(F32), pad_r]).reshape(1, LANES)

    xn = rms_norm(x, attn_norm_w, BF16, 256)
    proj = matmul(xn, w_main, tm=1024, tn=512, tk=D, out_dtype=BF16, name="proj_main")
    small = matmul(xn, w_small, tm=1024, tn=LANES, tk=D, out_dtype=F32, name="proj_small")
    aux, cos, sin = gates(small, positions.reshape(S, 1), invf_row, alog_row, dtb_row, 1024)

    t_att = 512
    qt = mla_q(proj, mla_q_norm_w, wq_perm, cos, sin, heads=mla_heads, tm=256, scale=dq ** -0.5 * LOG2E)
    k_full, vt4 = mla_kv(proj, o_kv // MLA_KV_RANK, mla_kv_norm_w, wkv_perm, aux, heads=mla_heads, tm=t_att)
    k4 = k_full.reshape(mla_heads, S // t_att, t_att, dq)
    mla_out = flash_attention(qt, k4, vt4, t=t_att)

    tc = 512
    gq = gdn_conv(proj, c_qkv // tc, gdn_conv_w, 0, gw, tm=512, tc=tc, l2norm=True, out_scale=HEAD_DIM ** -0.5)[0]
    gk, gkt = gdn_conv(proj, (c_qkv + gw) // tc, gdn_conv_w, gw // tc, gw, tm=512, tc=tc, l2norm=True, emit_t=True)
    gv = gdn_conv(proj, (c_qkv + 2 * gw) // tc, gdn_conv_w, 2 * gw // tc, gw, tm=512, tc=tc, l2norm=False)[0]
    hb = 2
    gc_full = chunk_cumsum(aux, GDN_CHUNK)
    gc = gc_full[:, MLA_ROPE:MLA_ROPE + gdn_heads]
    beta = aux[:, MLA_ROPE + gdn_heads:MLA_ROPE + 2 * gdn_heads]
    gcc = gc.reshape(S, gdn_heads // hb, hb).transpose(1, 0, 2)
    bc = beta.reshape(S, gdn_heads // hb, hb).transpose(1, 0, 2)
    gcr = gc.T.reshape(gdn_heads // hb, hb, S)
    gdn_out = gdn_delta(gq, gk, gkt, gv, proj, c_gate // (hb * HEAD_DIM), gcc, bc, gcr, gdn_norm_w,
                        hb=hb, rows=512, chunk=GDN_CHUNK, solve_passes=3, state_passes=3)

    mix = jnp.concatenate([mla_out, gdn_out], axis=1)
    h1 = matmul(mix, w_out, tm=1024, tn=1024, tk=1024, out_dtype=F32, res=x, name="w_out")

    hn = rms_norm(h1, xattn_norm_w, BF16, 256)
    memn = rms_norm(mem, mem_norm_w, BF16, 256)
    xd = D // XATTN_HEADS
    xq = matmul(hn, xattn_wq, tm=1024, tn=1024, tk=1024, out_dtype=BF16, out_scale=xd ** -0.5 * LOG2E, name="xattn_q")
    xk = matmul(memn, xattn_wk, tm=256, tn=1024, tk=1024, out_dtype=BF16, name="xattn_k")
    xv = matmul(memn, xattn_wv, tm=256, tn=1024, tk=1024, out_dtype=BF16, name="xattn_v")
    xo = xattn_core(xq, xk.T, xv, heads=XATTN_HEADS, tq=512)
    h2 = matmul(xo, xattn_wo, tm=1024, tn=1024, tk=1024, out_dtype=F32, res=h1, name="xattn_o")

    hn2 = rms_norm(h2, mlp_norm_w, BF16, 256)
    up = matmul(hn2, mlp_w_up, tm=1024, tn=512, tk=D, out_dtype=BF16, relu2=True, name="mlp_up")
    down = matmul(up, mlp_w_down, tm=2048, tn=1024, tk=512, out_dtype=F32, name="mlp_down")
    return rms_norm(h2, final_norm_w, F32, 256, add=down)


def kernel(x, mem, positions, attn_norm_w, w_in, mla_q_norm_w, mla_w_q_b, mla_kv_norm_w, mla_w_kv_b, gdn_conv_w, gdn_a_log, gdn_dt_bias, gdn_norm_w, w_out, xattn_norm_w, mem_norm_w, xattn_wq, xattn_wk, xattn_wv, xattn_wo, mlp_norm_w, mlp_w_up, mlp_w_down, final_norm_w):
    assert x.shape[0] == 1 and attn_norm_w.shape[0] == 1, "single sequence, single layer"
    out = _block(x[0], mem[0], positions[0], attn_norm_w[0], w_in[0], mla_q_norm_w[0], mla_w_q_b[0],
                 mla_kv_norm_w[0], mla_w_kv_b[0], gdn_conv_w[0], gdn_a_log[0], gdn_dt_bias[0], gdn_norm_w[0],
                 w_out[0], xattn_norm_w[0], mem_norm_w[0], xattn_wq[0], xattn_wk[0], xattn_wv[0], xattn_wo[0],
                 mlp_norm_w[0], mlp_w_up[0], mlp_w_down[0], final_norm_w)
    return out[None]
```

```python
import functools
import math

import jax
import jax.numpy as jnp
from jax import lax
from jax.experimental import pallas as pl
from jax.experimental.pallas import tpu as pltpu

F32 = jnp.float32
BF16 = jnp.bfloat16

HEAD_DIM = 128
MLA_NOPE = 128
MLA_ROPE = 64
MLA_Q_RANK = 1024
MLA_KV_RANK = 512
GDN_CONV = 4
XATTN_HEADS = 4
ROPE_BASE = 10000.0
NORM_EPS = 1e-6
L2_EPS = 1e-6
LOG2E = math.log2(math.e)

V7X_VMEM_BYTES = 64 * 1024 * 1024
VMEM_LIMIT = 52 * 1024 * 1024
LANES = 128

GDN_CHUNK = 128
GDN_SOLVE_PASSES = 1
GDN_STATE_PASSES = 1
GDN_REFINE = False


def _cparams(sem):
    return pltpu.CompilerParams(dimension_semantics=sem, vmem_limit_bytes=VMEM_LIMIT)


def _rms_kernel(x_ref, w_ref, o_ref):
    x = x_ref[...].astype(F32)
    r = lax.rsqrt(jnp.mean(x * x, axis=-1, keepdims=True) + NORM_EPS)
    o_ref[...] = (x * r * w_ref[...]).astype(o_ref.dtype)


def _rms_add_kernel(x_ref, y_ref, w_ref, o_ref):
    x = x_ref[...] + y_ref[...]
    r = lax.rsqrt(jnp.mean(x * x, axis=-1, keepdims=True) + NORM_EPS)
    o_ref[...] = (x * r * w_ref[...]).astype(o_ref.dtype)


def rms_norm(x, w, out_dtype, tm, add=None):
    M, D = x.shape
    tm = min(tm, M)
    row = pl.BlockSpec((tm, D), lambda i: (i, 0))
    wspec = pl.BlockSpec((1, D), lambda i: (0, 0))
    if add is None:
        kern, specs, args = _rms_kernel, [row, wspec], (x, w.reshape(1, D))
    else:
        kern, specs, args = _rms_add_kernel, [row, row, wspec], (x, add, w.reshape(1, D))
    return pl.pallas_call(
        kern, grid=(M // tm,), in_specs=specs, out_specs=row,
        out_shape=jax.ShapeDtypeStruct((M, D), out_dtype),
        compiler_params=_cparams(("parallel",)), name="rms_norm")(*args)


def _mm_kernel(*refs, nk, relu2, has_res, out_scale):
    x_ref, w_ref = refs[0], refs[1]
    res_ref = refs[2] if has_res else None
    o_ref = refs[2 + has_res]

    def finish(r):
        if relu2:
            r = jnp.square(jnp.maximum(r, 0.0))
        if out_scale is not None:
            r = r * out_scale
        if has_res:
            r = r + res_ref[...]
        o_ref[...] = r.astype(o_ref.dtype)

    part = jnp.dot(x_ref[...], w_ref[...].astype(BF16), preferred_element_type=F32)
    if nk == 1:
        finish(part)
        return
    acc_ref = refs[3 + has_res]
    k = pl.program_id(2)

    @pl.when(k == 0)
    def _():
        acc_ref[...] = part

    @pl.when(jnp.logical_and(k > 0, k < nk - 1))
    def _():
        acc_ref[...] += part

    @pl.when(k == nk - 1)
    def _():
        finish(acc_ref[...] + part)


def matmul(x, w, *, tm, tn, tk, out_dtype, relu2=False, res=None, out_scale=None, name="matmul"):
    M, K = x.shape
    K2, N = w.shape
    assert K == K2
    tm, tn, tk = min(tm, M), min(tn, N), min(tk, K)
    assert M % tm == 0 and N % tn == 0 and K % tk == 0
    nk = K // tk
    in_specs = [pl.BlockSpec((tm, tk), lambda i, j, k: (i, k)),
                pl.BlockSpec((tk, tn), lambda i, j, k: (k, j))]
    args = [x, w]
    if res is not None:
        in_specs.append(pl.BlockSpec((tm, tn), lambda i, j, k: (i, j)))
        args.append(res)
    scratch = [pltpu.VMEM((tm, tn), F32)] if nk > 1 else []
    kern = functools.partial(_mm_kernel, nk=nk, relu2=relu2, has_res=res is not None, out_scale=out_scale)
    return pl.pallas_call(
        kern, grid=(M // tm, N // tn, nk), in_specs=in_specs,
        out_specs=pl.BlockSpec((tm, tn), lambda i, j, k: (i, j)),
        out_shape=jax.ShapeDtypeStruct((M, N), out_dtype),
        scratch_shapes=scratch,
        compiler_params=_cparams(("parallel", "parallel", "arbitrary")), name=name)(*args)


def _mm_kacc_kernel(x_ref, w_ref, o_ref, *, row_chunk):
    k = pl.program_id(2)
    wb = w_ref[...].astype(BF16)
    tm = x_ref.shape[0]

    @pl.when(k == 0)
    def _():
        for r in range(0, tm, row_chunk):
            o_ref[r:r + row_chunk, :] = jnp.dot(x_ref[r:r + row_chunk, :], wb, preferred_element_type=F32)

    @pl.when(k > 0)
    def _():
        for r in range(0, tm, row_chunk):
            o_ref[r:r + row_chunk, :] += jnp.dot(x_ref[r:r + row_chunk, :], wb, preferred_element_type=F32)


def matmul_kacc(x, w, *, tm, tn, tk, row_chunk, name):
    M, K = x.shape
    N = w.shape[1]
    tm, tn, tk = min(tm, M), min(tn, N), min(tk, K)
    assert M % tm == 0 and N % tn == 0 and K % tk == 0 and tm % row_chunk == 0
    return pl.pallas_call(
        functools.partial(_mm_kacc_kernel, row_chunk=row_chunk), grid=(M // tm, N // tn, K // tk),
        in_specs=[pl.BlockSpec((tm, tk), lambda i, j, k: (i, k)),
                  pl.BlockSpec((tk, tn), lambda i, j, k: (k, j))],
        out_specs=pl.BlockSpec((tm, tn), lambda i, j, k: (i, j)),
        out_shape=jax.ShapeDtypeStruct((M, N), F32),
        compiler_params=_cparams(("parallel", "parallel", "arbitrary")), name=name)(x, w)


def _gates_kernel(y_ref, pos_ref, invf_ref, alog_ref, dtb_ref, aux_ref, cos_ref, sin_ref):
    y = y_ref[...]
    ang = pos_ref[...].astype(F32) * invf_ref[...]
    c = jnp.cos(ang)
    s = jnp.sin(ang)
    cos_ref[...] = c
    sin_ref[...] = s
    lane = lax.broadcasted_iota(jnp.int32, y.shape, 1)
    half = MLA_ROPE // 2
    rot = jnp.where(lane < half, -pltpu.roll(y, LANES - half, 1), pltpu.roll(y, half, 1))
    roped = y * c + rot * s
    z = y + dtb_ref[...]
    softplus = jnp.maximum(z, 0.0) + jnp.log1p(jnp.exp(-jnp.abs(z)))
    g = -jnp.exp(alog_ref[...]) * softplus
    beta = jax.nn.sigmoid(y)
    aux_ref[...] = jnp.where(lane < MLA_ROPE, roped, jnp.where(lane < MLA_ROPE + 16, g, beta))


def gates(y, pos_col, invf_row, alog_row, dtb_row, tm):
    M = y.shape[0]
    tm = min(tm, M)
    row = pl.BlockSpec((tm, LANES), lambda i: (i, 0))
    one = pl.BlockSpec((1, LANES), lambda i: (0, 0))
    shp = jax.ShapeDtypeStruct((M, LANES), F32)
    return pl.pallas_call(
        _gates_kernel, grid=(M // tm,),
        in_specs=[row, pl.BlockSpec((tm, 1), lambda i: (i, 0)), one, one, one],
        out_specs=[row, row, row], out_shape=[shp, shp, shp],
        compiler_params=_cparams(("parallel",)), name="gates")(y, pos_col, invf_row, alog_row, dtb_row)


def _mla_q_kernel(cq_ref, nw_ref, wq_ref, cos_ref, sin_ref, o_ref, *, heads, scale):
    x = cq_ref[...].astype(F32)
    r = lax.rsqrt(jnp.mean(x * x, axis=-1, keepdims=True) + NORM_EPS)
    xn = (x * r * nw_ref[...]).astype(BF16)
    q = jnp.dot(xn, wq_ref[...], preferred_element_type=F32) * scale
    half = MLA_ROPE // 2
    n_nope = heads * MLA_NOPE
    n_half = heads * half
    reps = n_half // LANES
    c = jnp.concatenate([cos_ref[...]] * reps, axis=1)
    s = jnp.concatenate([sin_ref[...]] * reps, axis=1)
    p1 = q[:, n_nope:n_nope + n_half]
    p2 = q[:, n_nope + n_half:]
    r1t = (p1 * c - p2 * s).T.astype(BF16)
    r2t = (p2 * c + p1 * s).T.astype(BF16)
    for h in range(heads):
        o_ref[h, 0:MLA_NOPE, :] = q[:, h * MLA_NOPE:(h + 1) * MLA_NOPE].T.astype(BF16)
        o_ref[h, MLA_NOPE:MLA_NOPE + half, :] = r1t[h * half:(h + 1) * half, :]
        o_ref[h, MLA_NOPE + half:MLA_NOPE + 2 * half, :] = r2t[h * half:(h + 1) * half, :]


def mla_q(proj, nw, wq, cos, sin, *, heads, tm, scale):
    S = proj.shape[0]
    R = nw.shape[0]
    dq = MLA_NOPE + MLA_ROPE
    kern = functools.partial(_mla_q_kernel, heads=heads, scale=scale)
    return pl.pallas_call(
        kern, grid=(S // tm,),
        in_specs=[pl.BlockSpec((tm, R), lambda i: (i, 0)),
                  pl.BlockSpec((1, R), lambda i: (0, 0)),
                  pl.BlockSpec(wq.shape, lambda i: (0, 0)),
                  pl.BlockSpec((tm, LANES), lambda i: (i, 0)),
                  pl.BlockSpec((tm, LANES), lambda i: (i, 0))],
        out_specs=pl.BlockSpec((heads, dq, tm), lambda i: (0, 0, i)),
        out_shape=jax.ShapeDtypeStruct((heads, dq, S), BF16),
        compiler_params=_cparams(("parallel",)), name="mla_q")(proj, nw.reshape(1, R), wq, cos, sin)


def _mla_kv_kernel(ckv_ref, nw_ref, wkv_ref, aux_ref, k_ref, vt_ref, *, heads):
    x = ckv_ref[...].astype(F32)
    r = lax.rsqrt(jnp.mean(x * x, axis=-1, keepdims=True) + NORM_EPS)
    xn = (x * r * nw_ref[...]).astype(BF16)
    kv = jnp.dot(xn, wkv_ref[...], preferred_element_type=F32)
    kpe = aux_ref[...][:, 0:MLA_ROPE].astype(BF16)
    n_nope = heads * MLA_NOPE
    for h in range(heads):
        k_ref[h, :, 0:MLA_NOPE] = kv[:, h * MLA_NOPE:(h + 1) * MLA_NOPE].astype(BF16)
        k_ref[h, :, MLA_NOPE:MLA_NOPE + MLA_ROPE] = kpe
        vt_ref[h, 0] = kv[:, n_nope + h * HEAD_DIM:n_nope + (h + 1) * HEAD_DIM].T.astype(BF16)


def mla_kv(proj, col_block, nw, wkv, aux, *, heads, tm):
    S = proj.shape[0]
    R = nw.shape[0]
    dq = MLA_NOPE + MLA_ROPE
    kern = functools.partial(_mla_kv_kernel, heads=heads)
    return pl.pallas_call(
        kern, grid=(S // tm,),
        in_specs=[pl.BlockSpec((tm, R), lambda i: (i, col_block)),
                  pl.BlockSpec((1, R), lambda i: (0, 0)),
                  pl.BlockSpec(wkv.shape, lambda i: (0, 0)),
                  pl.BlockSpec((tm, LANES), lambda i: (i, 0))],
        out_specs=[pl.BlockSpec((heads, tm, dq), lambda i: (0, i, 0)),
                   pl.BlockSpec((heads, 1, HEAD_DIM, tm), lambda i: (0, i, 0, 0))],
        out_shape=[jax.ShapeDtypeStruct((heads, S, dq), BF16),
                   jax.ShapeDtypeStruct((heads, S // tm, HEAD_DIM, tm), BF16)],
        compiler_params=_cparams(("parallel",)), name="mla_kv")(proj, nw.reshape(1, R), wkv, aux)


def _flash_kernel(qt_ref, k_ref, vt_ref, o_ref, acc_ref, m_ref, l_ref, sa_ref, sb_ref, *, t, hp):
    i = pl.program_id(1)
    m_ref[...] = jnp.full(m_ref.shape, -jnp.inf, F32)
    l_ref[...] = jnp.zeros(l_ref.shape, F32)
    acc_ref[...] = jnp.zeros(acc_ref.shape, F32)
    heads = range(hp)

    def scores(j, s_ref):
        for h in heads:
            s_ref[h] = jnp.dot(k_ref[h, j], qt_ref[h], preferred_element_type=F32)

    def softmax_pv(j, s_ref, masked):
        s = [s_ref[h] for h in heads]
        if masked:
            kpos = lax.broadcasted_iota(jnp.int32, (t, t), 0)
            qpos = lax.broadcasted_iota(jnp.int32, (t, t), 1)
            s = [jnp.where(kpos <= qpos, sh, -jnp.inf) for sh in s]
        m_old = [m_ref[h] for h in heads]
        m_new = [jnp.maximum(m_old[h], jnp.max(s[h], axis=0, keepdims=True)) for h in heads]
        alpha = [jnp.exp2(m_old[h] - m_new[h]) for h in heads]
        p = [jnp.exp2(s[h] - m_new[h]) for h in heads]
        for h in heads:
            l_ref[h] = alpha[h] * l_ref[h] + jnp.sum(p[h], axis=0, keepdims=True)
            m_ref[h] = m_new[h]
        pv = [jnp.dot(vt_ref[h, j], p[h].astype(BF16), preferred_element_type=F32) for h in heads]
        for h in heads:
            acc_ref[h] = alpha[h] * acc_ref[h] + pv[h]

    scores(0, sa_ref)

    def body(jj, carry):
        j = 2 * jj
        scores(j + 1, sb_ref)
        softmax_pv(j, sa_ref, False)
        scores(j + 2, sa_ref)
        softmax_pv(j + 1, sb_ref, False)
        return carry

    lax.fori_loop(0, i // 2, body, 0)

    @pl.when(i % 2 == 0)
    def _():
        softmax_pv(i, sa_ref, True)

    @pl.when(i % 2 == 1)
    def _():
        scores(i, sb_ref)
        softmax_pv(i - 1, sa_ref, False)
        softmax_pv(i, sb_ref, True)

    for h in range(hp):
        o_ref[:, h * HEAD_DIM:(h + 1) * HEAD_DIM] = (acc_ref[h] / l_ref[h]).T.astype(o_ref.dtype)


def flash_attention(qt, k4, vt4, *, t, hp):
    H, dq, S = qt.shape
    nk = S // t
    kern = functools.partial(_flash_kernel, t=t, hp=hp)
    return pl.pallas_call(
        kern, grid=(H // hp, nk),
        in_specs=[pl.BlockSpec((hp, dq, t), lambda g, i: (g, 0, i)),
                  pl.BlockSpec((hp, nk, t, dq), lambda g, i: (g, 0, 0, 0)),
                  pl.BlockSpec((hp, nk, HEAD_DIM, t), lambda g, i: (g, 0, 0, 0))],
        out_specs=pl.BlockSpec((t, hp * HEAD_DIM), lambda g, i: (i, g)),
        out_shape=jax.ShapeDtypeStruct((S, H * HEAD_DIM), BF16),
        scratch_shapes=[pltpu.VMEM((hp, HEAD_DIM, t), F32), pltpu.VMEM((hp, 1, t), F32),
                        pltpu.VMEM((hp, 1, t), F32), pltpu.VMEM((hp, t, t), F32), pltpu.VMEM((hp, t, t), F32)],
        compiler_params=_cparams(("parallel", "arbitrary")), name="mla_flash")(qt, k4, vt4)


def _conv_kernel(x_ref, halo_ref, w_ref, *o_refs, l2norm, out_scale, emit_t, heads_per_block):
    i = pl.program_id(0)
    x = x_ref[...].astype(F32)
    halo = halo_ref[...].astype(F32)[8:16, :]
    halo = jnp.where(i == 0, 0.0, halo)
    xcat = jnp.concatenate([halo, x], axis=0)
    w = w_ref[...]
    acc = x * w[GDN_CONV - 1:GDN_CONV, :]
    for s in range(1, GDN_CONV):
        acc = acc + pltpu.roll(xcat, s, 0)[8:, :] * w[GDN_CONV - 1 - s:GDN_CONV - s, :]
    y = acc * jax.nn.sigmoid(acc)
    if l2norm:
        parts = []
        for h in range(heads_per_block):
            yh = y[:, h * HEAD_DIM:(h + 1) * HEAD_DIM]
            parts.append(yh * (lax.rsqrt(jnp.sum(yh * yh, axis=-1, keepdims=True) + L2_EPS) * out_scale))
        y = jnp.concatenate(parts, axis=1)
    o_refs[0][...] = y
    if emit_t:
        o_refs[1][...] = y.T


def gdn_conv(proj, col_block0, conv_w, wcol_block0, n_cols, *, tm, tc, l2norm, out_scale=1.0, emit_t=False):
    S = proj.shape[0]
    kern = functools.partial(_conv_kernel, l2norm=l2norm, out_scale=out_scale, emit_t=emit_t,
                             heads_per_block=tc // HEAD_DIM)
    out_specs = [pl.BlockSpec((tm, tc), lambda i, j: (i, j))]
    out_shape = [jax.ShapeDtypeStruct((S, n_cols), F32)]
    if emit_t:
        out_specs.append(pl.BlockSpec((tc, tm), lambda i, j: (j, i)))
        out_shape.append(jax.ShapeDtypeStruct((n_cols, S), F32))
    hb = tm // 16
    return pl.pallas_call(
        kern, grid=(S // tm, n_cols // tc),
        in_specs=[pl.BlockSpec((tm, tc), lambda i, j: (i, col_block0 + j)),
                  pl.BlockSpec((16, tc), lambda i, j: (jnp.maximum(i * hb - 1, 0), col_block0 + j)),
                  pl.BlockSpec((GDN_CONV, tc), lambda i, j: (0, wcol_block0 + j))],
        out_specs=out_specs, out_shape=out_shape,
        compiler_params=_cparams(("parallel", "parallel")), name="gdn_conv")(proj, proj, conv_w)


def _split_dot(a, b, passes):
    a_hi = a.astype(BF16)
    b_hi = b.astype(BF16)
    out = jnp.dot(a_hi, b_hi, preferred_element_type=F32)
    if passes == 3:
        a_lo = (a - a_hi.astype(F32)).astype(BF16)
        b_lo = (b - b_hi.astype(F32)).astype(BF16)
        out = out + jnp.dot(a_hi, b_lo, preferred_element_type=F32) + jnp.dot(a_lo, b_hi, preferred_element_type=F32)
    return out


def _cumsum_kernel(g_ref, o_ref, *, chunk):
    r = lax.broadcasted_iota(jnp.int32, (chunk, chunk), 0)
    c = lax.broadcasted_iota(jnp.int32, (chunk, chunk), 1)
    tril = jnp.where(r >= c, 1.0, 0.0).astype(F32)
    g = g_ref[...]
    g_hi = g.astype(BF16).astype(F32)
    g_mid = (g - g_hi).astype(BF16).astype(F32)
    g_lo = g - g_hi - g_mid
    trilb = tril.astype(BF16)
    o_ref[...] = (jnp.dot(trilb, g_hi.astype(BF16), preferred_element_type=F32)
                  + jnp.dot(trilb, g_mid.astype(BF16), preferred_element_type=F32)
                  + jnp.dot(trilb, g_lo.astype(BF16), preferred_element_type=F32))


def chunk_cumsum(g, chunk):
    M = g.shape[0]
    row = pl.BlockSpec((chunk, LANES), lambda i: (i, 0))
    return pl.pallas_call(
        functools.partial(_cumsum_kernel, chunk=chunk), grid=(M // chunk,), in_specs=[row], out_specs=row,
        out_shape=jax.ShapeDtypeStruct((M, LANES), F32),
        compiler_params=_cparams(("parallel",)), name="gdn_cumsum")(g)


def _gdn_kernel(q_ref, k_ref, kt_ref, v_ref, gate_ref, gcc_ref, bc_ref, gcr_ref, nw_ref, o_ref, s_ref,
                *, hb, rows, chunk, solve_passes, state_passes, refine):
    @pl.when(pl.program_id(1) == 0)
    def _():
        s_ref[...] = jnp.zeros(s_ref.shape, F32)

    C = chunk
    ri = lax.broadcasted_iota(jnp.int32, (C, C), 0)
    ci = lax.broadcasted_iota(jnp.int32, (C, C), 1)
    lower = ri >= ci
    strict = ri > ci
    nw = nw_ref[...]
    n_steps = C.bit_length() - 1
    n_chunks = rows // C
    bodies = [(c, h) for c in range(n_chunks) for h in range(hb)]

    def rows_of(c):
        return slice(c * C, (c + 1) * C)

    def cols_of(h):
        return slice(h * HEAD_DIM, (h + 1) * HEAD_DIM)

    xs, amats, rhs, qks, qds, tails, eglasts = {}, {}, {}, {}, {}, {}, {}
    for b in bodies:
        c, h = b
        q = q_ref[rows_of(c), cols_of(h)]
        k = k_ref[rows_of(c), cols_of(h)]
        v = v_ref[rows_of(c), cols_of(h)]
        kt = kt_ref[cols_of(h), rows_of(c)]
        gcc = gcc_ref[rows_of(c), h:h + 1]
        bc = bc_ref[rows_of(c), h:h + 1]
        gcr = gcr_ref[h:h + 1, rows_of(c)]
        glast = gcr[:, C - 1:C]
        decay = jnp.exp(jnp.where(lower, gcc - gcr, -jnp.inf))
        egc = jnp.exp(gcc)
        kb = k * bc
        amats[b] = jnp.where(strict, _split_dot(kb, kt, state_passes) * decay, 0.0)
        rhs[b] = jnp.concatenate([v * bc, kb * egc], axis=1)
        qks[b] = jnp.where(lower, _split_dot(q, kt, state_passes) * decay, 0.0)
        qds[b] = q * egc
        tails[b] = kt * jnp.exp(glast - gcr)
        eglasts[b] = jnp.exp(glast)

    xor = ri ^ ci
    tinv = {b: jnp.where(ri == ci, 1.0, 0.0) - jnp.where(strict & (xor == 1), amats[b], 0.0) for b in bodies}
    for level in range(1, n_steps):
        join = strict & ((xor >> level) == 1)
        mids = {b: _split_dot(jnp.where(join, amats[b], 0.0), tinv[b], solve_passes) for b in bodies}
        for b in bodies:
            tinv[b] = tinv[b] - _split_dot(tinv[b], mids[b], solve_passes)
    for b in bodies:
        xs[b] = _split_dot(tinv[b], rhs[b], solve_passes)
    if refine:
        for b in bodies:
            lx = xs[b] + _split_dot(amats[b], xs[b], 3)
            xs[b] = xs[b] + _split_dot(tinv[b], rhs[b] - lx, 1)

    states = [s_ref[h] for h in range(hb)]
    for c in range(n_chunks):
        ws = [_split_dot(jnp.concatenate([xs[(c, h)][:, HEAD_DIM:], qds[(c, h)]], axis=0), states[h], state_passes)
              for h in range(hb)]
        v_new = [xs[(c, h)][:, :HEAD_DIM] - ws[h][:C] for h in range(hb)]
        outs = [ws[h][C:] + _split_dot(qks[(c, h)], v_new[h], state_passes) for h in range(hb)]
        states = [states[h] * eglasts[(c, h)] + _split_dot(tails[(c, h)], v_new[h], state_passes)
                  for h in range(hb)]
        for h in range(hb):
            o = outs[h]
            o = o * lax.rsqrt(jnp.mean(o * o, axis=-1, keepdims=True) + NORM_EPS) * nw
            gate = gate_ref[rows_of(c), cols_of(h)].astype(F32)
            o_ref[rows_of(c), cols_of(h)] = (o * (gate * jax.nn.sigmoid(gate))).astype(o_ref.dtype)
    for h in range(hb):
        s_ref[h] = states[h]


def gdn_delta(q, k, kt, v, proj, gate_block0, gcc, bc, gcr, nw, *, hb, rows, chunk, solve_passes, state_passes,
              refine):
    S, W = q.shape
    tc = hb * HEAD_DIM
    kern = functools.partial(_gdn_kernel, hb=hb, rows=rows, chunk=chunk,
                             solve_passes=solve_passes, state_passes=state_passes, refine=refine)
    blk = pl.BlockSpec((rows, tc), lambda g, r: (r, g))
    return pl.pallas_call(
        kern, grid=(W // tc, S // rows),
        in_specs=[blk, blk, pl.BlockSpec((tc, rows), lambda g, r: (g, r)), blk,
                  pl.BlockSpec((rows, tc), lambda g, r: (r, gate_block0 + g)),
                  pl.BlockSpec((None, rows, hb), lambda g, r: (g, r, 0)),
                  pl.BlockSpec((None, rows, hb), lambda g, r: (g, r, 0)),
                  pl.BlockSpec((None, hb, rows), lambda g, r: (g, 0, r)),
                  pl.BlockSpec((1, HEAD_DIM), lambda g, r: (0, 0))],
        out_specs=blk,
        out_shape=jax.ShapeDtypeStruct((S, W), BF16),
        scratch_shapes=[pltpu.VMEM((hb, HEAD_DIM, HEAD_DIM), F32)],
        compiler_params=_cparams(("parallel", "arbitrary")), name="gdn_delta")(
            q, k, kt, v, proj, gcc, bc, gcr, nw.reshape(1, HEAD_DIM))


def _xattn_kernel(q_ref, kt_ref, v_ref, o_ref):
    s = jnp.dot(q_ref[...], kt_ref[...], preferred_element_type=F32)
    m = jnp.max(s, axis=-1, keepdims=True)
    p = jnp.exp2(s - m)
    l = jnp.sum(p, axis=-1, keepdims=True)
    o = jnp.dot(p.astype(BF16), v_ref[...], preferred_element_type=F32)
    o_ref[...] = (o / l).astype(o_ref.dtype)


def xattn_core(q, kt, v, *, heads, tq):
    S, W = q.shape
    M = v.shape[0]
    d = W // heads
    return pl.pallas_call(
        _xattn_kernel, grid=(heads, S // tq),
        in_specs=[pl.BlockSpec((tq, d), lambda h, i: (i, h)),
                  pl.BlockSpec((d, M), lambda h, i: (h, 0)),
                  pl.BlockSpec((M, d), lambda h, i: (0, h))],
        out_specs=pl.BlockSpec((tq, d), lambda h, i: (i, h)),
        out_shape=jax.ShapeDtypeStruct((S, W), BF16),
        compiler_params=_cparams(("parallel", "parallel")), name="xattn_core")(q, kt, v)


def _block(x, mem, positions, attn_norm_w, w_in, mla_q_norm_w, mla_w_q_b, mla_kv_norm_w, mla_w_kv_b,
           gdn_conv_w, gdn_a_log, gdn_dt_bias, gdn_norm_w, w_out, xattn_norm_w, mem_norm_w,
           xattn_wq, xattn_wk, xattn_wv, xattn_wo, mlp_norm_w, mlp_w_up, mlp_w_down, final_norm_w):
    S, D = x.shape
    n_mix = w_out.shape[0]
    mla_heads = n_mix // 2 // HEAD_DIM
    gdn_heads = gdn_conv_w.shape[1] // (3 * HEAD_DIM)
    gw = gdn_heads * HEAD_DIM
    dq = MLA_NOPE + MLA_ROPE
    half = MLA_ROPE // 2

    o_kv = MLA_Q_RANK
    o_pe = o_kv + MLA_KV_RANK
    o_qkv = o_pe + MLA_ROPE
    o_a = o_qkv + 3 * gw
    o_b = o_a + gdn_heads
    o_gate = o_b + gdn_heads
    w_main = jnp.concatenate([w_in[:, :o_pe], w_in[:, o_qkv:o_a], w_in[:, o_gate:]], axis=1).astype(BF16)
    w_small = jnp.concatenate([w_in[:, o_pe:o_qkv], w_in[:, o_a:o_gate],
                               jnp.zeros((D, LANES - MLA_ROPE - 2 * gdn_heads), F32)], axis=1)
    c_qkv = o_pe
    c_gate = o_pe + 3 * gw

    wq3 = mla_w_q_b.reshape(MLA_Q_RANK, mla_heads, dq)
    wq_perm = jnp.concatenate([wq3[:, :, :MLA_NOPE].reshape(MLA_Q_RANK, -1),
                               wq3[:, :, MLA_NOPE:MLA_NOPE + half].reshape(MLA_Q_RANK, -1),
                               wq3[:, :, MLA_NOPE + half:].reshape(MLA_Q_RANK, -1)], axis=1).astype(BF16)
    wkv3 = mla_w_kv_b.reshape(MLA_KV_RANK, mla_heads, MLA_NOPE + HEAD_DIM)
    wkv_perm = jnp.concatenate([wkv3[:, :, :MLA_NOPE].reshape(MLA_KV_RANK, -1),
                                wkv3[:, :, MLA_NOPE:].reshape(MLA_KV_RANK, -1)], axis=1).astype(BF16)

    inv_freq = ROPE_BASE ** (-jnp.arange(half, dtype=F32) / half)
    invf_row = jnp.tile(inv_freq, LANES // half).reshape(1, LANES)
    pad_l = jnp.zeros((MLA_ROPE,), F32)
    pad_r = jnp.zeros((LANES - MLA_ROPE - gdn_heads,), F32)
    alog_row = jnp.concatenate([pad_l, gdn_a_log.astype(F32), pad_r]).reshape(1, LANES)
    dtb_row = jnp.concatenate([pad_l, gdn_dt_bias.astype(F32), pad_r]).reshape(1, LANES)

    xn = rms_norm(x, attn_norm_w, BF16, 256)
    proj = matmul(xn, w_main, tm=1024, tn=512, tk=D, out_dtype=BF16, name="proj_main")
    small = matmul(xn, w_small, tm=1024, tn=LANES, tk=D, out_dtype=F32, name="proj_small")
    aux, cos, sin = gates(small, positions.reshape(S, 1), invf_row, alog_row, dtb_row, 1024)

    t_att = 512
    qt = mla_q(proj, mla_q_norm_w, wq_perm, cos, sin, heads=mla_heads, tm=256, scale=dq ** -0.5 * LOG2E)
    k_full, vt4 = mla_kv(proj, o_kv // MLA_KV_RANK, mla_kv_norm_w, wkv_perm, aux, heads=mla_heads, tm=t_att)
    k4 = k_full.reshape(mla_heads, S // t_att, t_att, dq)
    mla_out = flash_attention(qt, k4, vt4, t=t_att, hp=2)

    tc = 512
    gq = gdn_conv(proj, c_qkv // tc, gdn_conv_w, 0, gw, tm=512, tc=tc, l2norm=True, out_scale=HEAD_DIM ** -0.5)[0]
    gk, gkt = gdn_conv(proj, (c_qkv + gw) // tc, gdn_conv_w, gw // tc, gw, tm=512, tc=tc, l2norm=True, emit_t=True)
    gv = gdn_conv(proj, (c_qkv + 2 * gw) // tc, gdn_conv_w, 2 * gw // tc, gw, tm=512, tc=tc, l2norm=False)[0]
    hb = 4
    gc_full = chunk_cumsum(aux, GDN_CHUNK)
    gc = gc_full[:, MLA_ROPE:MLA_ROPE + gdn_heads]
    beta = aux[:, MLA_ROPE + gdn_heads:MLA_ROPE + 2 * gdn_heads]
    gcc = gc.reshape(S, gdn_heads // hb, hb).transpose(1, 0, 2)
    bc = beta.reshape(S, gdn_heads // hb, hb).transpose(1, 0, 2)
    gcr = gc.T.reshape(gdn_heads // hb, hb, S)
    gdn_out = gdn_delta(gq, gk, gkt, gv, proj, c_gate // (hb * HEAD_DIM), gcc, bc, gcr, gdn_norm_w,
                        hb=hb, rows=512, chunk=GDN_CHUNK, solve_passes=GDN_SOLVE_PASSES,
                        state_passes=GDN_STATE_PASSES, refine=GDN_REFINE)

    mix = jnp.concatenate([mla_out, gdn_out], axis=1)
    h1 = matmul(mix, w_out, tm=1024, tn=512, tk=n_mix, out_dtype=F32, res=x, name="w_out")

    hn = rms_norm(h1, xattn_norm_w, BF16, 256)
    memn = rms_norm(mem, mem_norm_w, BF16, 256)
    xd = D // XATTN_HEADS
    xq = matmul(hn, xattn_wq, tm=1024, tn=512, tk=D, out_dtype=BF16, out_scale=xd ** -0.5 * LOG2E, name="xattn_q")
    xk = matmul(memn, xattn_wk, tm=256, tn=1024, tk=1024, out_dtype=BF16, name="xattn_k")
    xv = matmul(memn, xattn_wv, tm=256, tn=1024, tk=1024, out_dtype=BF16, name="xattn_v")
    xo = xattn_core(xq, xk.T, xv, heads=XATTN_HEADS, tq=512)
    h2 = matmul(xo, xattn_wo, tm=1024, tn=512, tk=D, out_dtype=F32, res=h1, name="xattn_o")

    hn2 = rms_norm(h2, mlp_norm_w, BF16, 256)
    up = matmul(hn2, mlp_w_up, tm=1024, tn=512, tk=D, out_dtype=BF16, relu2=True, name="mlp_up")
    down = matmul_kacc(up, mlp_w_down, tm=2048, tn=1024, tk=1024, row_chunk=256, name="mlp_down")
    return rms_norm(h2, final_norm_w, F32, 256, add=down)


def kernel(x, mem, positions, attn_norm_w, w_in, mla_q_norm_w, mla_w_q_b, mla_kv_norm_w, mla_w_kv_b, gdn_conv_w, gdn_a_log, gdn_dt_bias, gdn_norm_w, w_out, xattn_norm_w, mem_norm_w, xattn_wq, xattn_wk, xattn_wv, xattn_wo, mlp_norm_w, mlp_w_up, mlp_w_down, final_norm_w):
    assert x.shape[0] == 1 and attn_norm_w.shape[0] == 1, "single sequence, single layer"
    out = _block(x[0], mem[0], positions[0], attn_norm_w[0], w_in[0], mla_q_norm_w[0], mla_w_q_b[0],
                 mla_kv_norm_w[0], mla_w_kv_b[0], gdn_conv_w[0], gdn_a_log[0], gdn_dt_bias[0], gdn_norm_w[0],
                 w_out[0], xattn_norm_w[0], mem_norm_w[0], xattn_wq[0], xattn_wk[0], xattn_wv[0], xattn_wo[0],
                 mlp_norm_w[0], mlp_w_up[0], mlp_w_down[0], final_norm_w)
    return out[None]
```

```python
import functools
import math

import jax
import jax.numpy as jnp
from jax import lax
from jax.experimental import pallas as pl
from jax.experimental.pallas import tpu as pltpu

F32 = jnp.float32
BF16 = jnp.bfloat16

HEAD_DIM = 128
MLA_NOPE = 128
MLA_ROPE = 64
MLA_Q_RANK = 1024
MLA_KV_RANK = 512
GDN_CONV = 4
XATTN_HEADS = 4
ROPE_BASE = 10000.0
NORM_EPS = 1e-6
L2_EPS = 1e-6
LOG2E = math.log2(math.e)

V7X_VMEM_BYTES = 64 * 1024 * 1024
VMEM_LIMIT = 52 * 1024 * 1024
LANES = 128

GDN_CHUNK = 128
GDN_SOLVE_PASSES = 1
GDN_STATE_PASSES = 1
GDN_REFINE = False


def _cparams(sem):
    return pltpu.CompilerParams(dimension_semantics=sem, vmem_limit_bytes=VMEM_LIMIT)


def _rms_kernel(x_ref, w_ref, o_ref):
    x = x_ref[...].astype(F32)
    r = lax.rsqrt(jnp.mean(x * x, axis=-1, keepdims=True) + NORM_EPS)
    o_ref[...] = (x * r * w_ref[...]).astype(o_ref.dtype)


def _rms_add_kernel(x_ref, y_ref, w_ref, o_ref):
    x = x_ref[...] + y_ref[...]
    r = lax.rsqrt(jnp.mean(x * x, axis=-1, keepdims=True) + NORM_EPS)
    o_ref[...] = (x * r * w_ref[...]).astype(o_ref.dtype)


def rms_norm(x, w, out_dtype, tm, add=None):
    M, D = x.shape
    tm = min(tm, M)
    row = pl.BlockSpec((tm, D), lambda i: (i, 0))
    wspec = pl.BlockSpec((1, D), lambda i: (0, 0))
    if add is None:
        kern, specs, args = _rms_kernel, [row, wspec], (x, w.reshape(1, D))
    else:
        kern, specs, args = _rms_add_kernel, [row, row, wspec], (x, add, w.reshape(1, D))
    return pl.pallas_call(
        kern, grid=(M // tm,), in_specs=specs, out_specs=row,
        out_shape=jax.ShapeDtypeStruct((M, D), out_dtype),
        compiler_params=_cparams(("parallel",)), name="rms_norm")(*args)


def _mm_kernel(*refs, nk, relu2, has_res, out_scale):
    x_ref, w_ref = refs[0], refs[1]
    res_ref = refs[2] if has_res else None
    o_ref = refs[2 + has_res]

    def finish(r):
        if relu2:
            r = jnp.square(jnp.maximum(r, 0.0))
        if out_scale is not None:
            r = r * out_scale
        if has_res:
            r = r + res_ref[...]
        o_ref[...] = r.astype(o_ref.dtype)

    part = jnp.dot(x_ref[...], w_ref[...].astype(BF16), preferred_element_type=F32)
    if nk == 1:
        finish(part)
        return
    acc_ref = refs[3 + has_res]
    k = pl.program_id(2)

    @pl.when(k == 0)
    def _():
        acc_ref[...] = part

    @pl.when(jnp.logical_and(k > 0, k < nk - 1))
    def _():
        acc_ref[...] += part

    @pl.when(k == nk - 1)
    def _():
        finish(acc_ref[...] + part)


def matmul(x, w, *, tm, tn, tk, out_dtype, relu2=False, res=None, out_scale=None, name="matmul"):
    M, K = x.shape
    K2, N = w.shape
    assert K == K2
    tm, tn, tk = min(tm, M), min(tn, N), min(tk, K)
    assert M % tm == 0 and N % tn == 0 and K % tk == 0
    nk = K // tk
    in_specs = [pl.BlockSpec((tm, tk), lambda i, j, k: (i, k)),
                pl.BlockSpec((tk, tn), lambda i, j, k: (k, j))]
    args = [x, w]
    if res is not None:
        in_specs.append(pl.BlockSpec((tm, tn), lambda i, j, k: (i, j)))
        args.append(res)
    scratch = [pltpu.VMEM((tm, tn), F32)] if nk > 1 else []
    kern = functools.partial(_mm_kernel, nk=nk, relu2=relu2, has_res=res is not None, out_scale=out_scale)
    return pl.pallas_call(
        kern, grid=(M // tm, N // tn, nk), in_specs=in_specs,
        out_specs=pl.BlockSpec((tm, tn), lambda i, j, k: (i, j)),
        out_shape=jax.ShapeDtypeStruct((M, N), out_dtype),
        scratch_shapes=scratch,
        compiler_params=_cparams(("parallel", "parallel", "arbitrary")), name=name)(*args)


_NT_DIMS = (((1,), (1,)), ((), ()))


def _proj_nt_kernel(x_ref, wt_ref, o_ref):
    wb = wt_ref[...].astype(BF16)
    o_ref[...] = lax.dot_general(x_ref[...], wb, _NT_DIMS, preferred_element_type=F32).astype(o_ref.dtype)


def proj_nt(x, wt, src_row_of_block, n_out, *, tm, tn, name):
    M, K = x.shape
    return pl.pallas_call(
        _proj_nt_kernel, grid=(M // tm, n_out // tn),
        in_specs=[pl.BlockSpec((tm, K), lambda i, j: (i, 0)),
                  pl.BlockSpec((pl.Element(tn), pl.Element(K)), lambda i, j: (src_row_of_block(j), 0))],
        out_specs=pl.BlockSpec((tm, tn), lambda i, j: (i, j)),
        out_shape=jax.ShapeDtypeStruct((M, n_out), BF16),
        compiler_params=_cparams(("parallel", "parallel")), name=name)(x, wt)


def _proj_small_kernel(x_ref, wa_ref, wb_ref, o_ref):
    wa = wa_ref[...].astype(BF16)
    wb = wb_ref[...].astype(BF16)
    pad = jnp.zeros((LANES - wa.shape[0] - wb.shape[0], wa.shape[1]), BF16)
    w = jnp.concatenate([wa, wb, pad], axis=0)
    o_ref[...] = lax.dot_general(x_ref[...], w, _NT_DIMS, preferred_element_type=F32)


def proj_small(x, wt, row_a, n_a, row_b, n_b, *, tm):
    M, K = x.shape
    return pl.pallas_call(
        _proj_small_kernel, grid=(M // tm,),
        in_specs=[pl.BlockSpec((tm, K), lambda i: (i, 0)),
                  pl.BlockSpec((pl.Element(n_a), pl.Element(K)), lambda i: (row_a, 0)),
                  pl.BlockSpec((pl.Element(n_b), pl.Element(K)), lambda i: (row_b, 0))],
        out_specs=pl.BlockSpec((tm, LANES), lambda i: (i, 0)),
        out_shape=jax.ShapeDtypeStruct((M, LANES), F32),
        compiler_params=_cparams(("parallel",)), name="proj_small")(x, wt, wt)


def _mm2_kernel(x1_ref, x2_ref, w_ref, res_ref, o_ref):
    k1 = x1_ref.shape[1]
    wb = w_ref[...].astype(BF16)
    r = jnp.dot(x1_ref[...], wb[:k1], preferred_element_type=F32)
    r = r + jnp.dot(x2_ref[...], wb[k1:], preferred_element_type=F32)
    o_ref[...] = r + res_ref[...]


def matmul2_res(x1, x2, w, res, *, tm, tn, name):
    M, K1 = x1.shape
    K2 = x2.shape[1]
    N = w.shape[1]
    return pl.pallas_call(
        _mm2_kernel, grid=(M // tm, N // tn),
        in_specs=[pl.BlockSpec((tm, K1), lambda i, j: (i, 0)),
                  pl.BlockSpec((tm, K2), lambda i, j: (i, 0)),
                  pl.BlockSpec((K1 + K2, tn), lambda i, j: (0, j)),
                  pl.BlockSpec((tm, tn), lambda i, j: (i, j))],
        out_specs=pl.BlockSpec((tm, tn), lambda i, j: (i, j)),
        out_shape=jax.ShapeDtypeStruct((M, N), F32),
        compiler_params=_cparams(("parallel", "parallel")), name=name)(x1, x2, w, res)


def _mm_kacc_kernel(x_ref, w_ref, res_ref, o_ref, *, row_chunk):
    k = pl.program_id(2)
    wb = w_ref[...].astype(BF16)
    tm = x_ref.shape[0]

    @pl.when(k == 0)
    def _():
        for r in range(0, tm, row_chunk):
            rows = slice(r, r + row_chunk)
            o_ref[rows, :] = res_ref[rows, :] + jnp.dot(x_ref[rows, :], wb, preferred_element_type=F32)

    @pl.when(k > 0)
    def _():
        for r in range(0, tm, row_chunk):
            rows = slice(r, r + row_chunk)
            o_ref[rows, :] += jnp.dot(x_ref[rows, :], wb, preferred_element_type=F32)


def matmul_kacc(x, w, res, *, tm, tn, tk, row_chunk, name):
    M, K = x.shape
    N = w.shape[1]
    tm, tn, tk = min(tm, M), min(tn, N), min(tk, K)
    assert M % tm == 0 and N % tn == 0 and K % tk == 0 and tm % row_chunk == 0
    return pl.pallas_call(
        functools.partial(_mm_kacc_kernel, row_chunk=row_chunk), grid=(M // tm, N // tn, K // tk),
        in_specs=[pl.BlockSpec((tm, tk), lambda i, j, k: (i, k)),
                  pl.BlockSpec((tk, tn), lambda i, j, k: (k, j)),
                  pl.BlockSpec((tm, tn), lambda i, j, k: (i, j), pipeline_mode=pl.Buffered(1))],
        out_specs=pl.BlockSpec((tm, tn), lambda i, j, k: (i, j)),
        out_shape=jax.ShapeDtypeStruct((M, N), F32),
        compiler_params=_cparams(("parallel", "parallel", "arbitrary")), name=name)(x, w, res)


def _gates_kernel(y_ref, pos_ref, invf_ref, alog_ref, dtb_ref, aux_ref, cos_ref, sin_ref):
    y = y_ref[...]
    ang = pos_ref[...].astype(F32) * invf_ref[...]
    c = jnp.cos(ang)
    s = jnp.sin(ang)
    cos_ref[...] = c
    sin_ref[...] = s
    lane = lax.broadcasted_iota(jnp.int32, y.shape, 1)
    half = MLA_ROPE // 2
    rot = jnp.where(lane < half, -pltpu.roll(y, LANES - half, 1), pltpu.roll(y, half, 1))
    roped = y * c + rot * s
    z = y + dtb_ref[...]
    softplus = jnp.maximum(z, 0.0) + jnp.log1p(jnp.exp(-jnp.abs(z)))
    g = -jnp.exp(alog_ref[...]) * softplus
    beta = jax.nn.sigmoid(y)
    aux_ref[...] = jnp.where(lane < MLA_ROPE, roped, jnp.where(lane < MLA_ROPE + 16, g, beta))


def gates(y, pos_col, invf_row, alog_row, dtb_row, tm):
    M = y.shape[0]
    tm = min(tm, M)
    row = pl.BlockSpec((tm, LANES), lambda i: (i, 0))
    one = pl.BlockSpec((1, LANES), lambda i: (0, 0))
    shp = jax.ShapeDtypeStruct((M, LANES), F32)
    return pl.pallas_call(
        _gates_kernel, grid=(M // tm,),
        in_specs=[row, pl.BlockSpec((tm, 1), lambda i: (i, 0)), one, one, one],
        out_specs=[row, row, row], out_shape=[shp, shp, shp],
        compiler_params=_cparams(("parallel",)), name="gates")(y, pos_col, invf_row, alog_row, dtb_row)


def _mla_q_kernel(cq_ref, nw_ref, wq_ref, cos_ref, sin_ref, o_ref, *, heads, scale):
    x = cq_ref[...].astype(F32)
    r = lax.rsqrt(jnp.mean(x * x, axis=-1, keepdims=True) + NORM_EPS)
    xn = (x * r * nw_ref[...]).astype(BF16)
    q = jnp.dot(xn, wq_ref[...], preferred_element_type=F32) * scale
    half = MLA_ROPE // 2
    n_nope = heads * MLA_NOPE
    n_half = heads * half
    reps = n_half // LANES
    c = jnp.concatenate([cos_ref[...]] * reps, axis=1)
    s = jnp.concatenate([sin_ref[...]] * reps, axis=1)
    p1 = q[:, n_nope:n_nope + n_half]
    p2 = q[:, n_nope + n_half:]
    r1t = (p1 * c - p2 * s).T.astype(BF16)
    r2t = (p2 * c + p1 * s).T.astype(BF16)
    for h in range(heads):
        o_ref[h, 0:MLA_NOPE, :] = q[:, h * MLA_NOPE:(h + 1) * MLA_NOPE].T.astype(BF16)
        o_ref[h, MLA_NOPE:MLA_NOPE + half, :] = r1t[h * half:(h + 1) * half, :]
        o_ref[h, MLA_NOPE + half:MLA_NOPE + 2 * half, :] = r2t[h * half:(h + 1) * half, :]


def mla_q(proj, nw, wq, cos, sin, *, heads, tm, scale):
    S = proj.shape[0]
    R = nw.shape[0]
    dq = MLA_NOPE + MLA_ROPE
    kern = functools.partial(_mla_q_kernel, heads=heads, scale=scale)
    return pl.pallas_call(
        kern, grid=(S // tm,),
        in_specs=[pl.BlockSpec((tm, R), lambda i: (i, 0)),
                  pl.BlockSpec((1, R), lambda i: (0, 0)),
                  pl.BlockSpec(wq.shape, lambda i: (0, 0)),
                  pl.BlockSpec((tm, LANES), lambda i: (i, 0)),
                  pl.BlockSpec((tm, LANES), lambda i: (i, 0))],
        out_specs=pl.BlockSpec((heads, dq, tm), lambda i: (0, 0, i)),
        out_shape=jax.ShapeDtypeStruct((heads, dq, S), BF16),
        compiler_params=_cparams(("parallel",)), name="mla_q")(proj, nw.reshape(1, R), wq, cos, sin)


def _mla_kv_kernel(ckv_ref, nw_ref, wkv_ref, aux_ref, k_ref, vt_ref, *, heads):
    x = ckv_ref[...].astype(F32)
    r = lax.rsqrt(jnp.mean(x * x, axis=-1, keepdims=True) + NORM_EPS)
    xn = (x * r * nw_ref[...]).astype(BF16)
    kv = jnp.dot(xn, wkv_ref[...], preferred_element_type=F32)
    kpe = aux_ref[...][:, 0:MLA_ROPE].astype(BF16)
    n_nope = heads * MLA_NOPE
    for h in range(heads):
        k_ref[h, :, 0:MLA_NOPE] = kv[:, h * MLA_NOPE:(h + 1) * MLA_NOPE].astype(BF16)
        k_ref[h, :, MLA_NOPE:MLA_NOPE + MLA_ROPE] = kpe
        vt_ref[h, 0] = kv[:, n_nope + h * HEAD_DIM:n_nope + (h + 1) * HEAD_DIM].T.astype(BF16)


def mla_kv(proj, col_block, nw, wkv, aux, *, heads, tm):
    S = proj.shape[0]
    R = nw.shape[0]
    dq = MLA_NOPE + MLA_ROPE
    kern = functools.partial(_mla_kv_kernel, heads=heads)
    return pl.pallas_call(
        kern, grid=(S // tm,),
        in_specs=[pl.BlockSpec((tm, R), lambda i: (i, col_block)),
                  pl.BlockSpec((1, R), lambda i: (0, 0)),
                  pl.BlockSpec(wkv.shape, lambda i: (0, 0)),
                  pl.BlockSpec((tm, LANES), lambda i: (i, 0))],
        out_specs=[pl.BlockSpec((heads, tm, dq), lambda i: (0, i, 0)),
                   pl.BlockSpec((heads, 1, HEAD_DIM, tm), lambda i: (0, i, 0, 0))],
        out_shape=[jax.ShapeDtypeStruct((heads, S, dq), BF16),
                   jax.ShapeDtypeStruct((heads, S // tm, HEAD_DIM, tm), BF16)],
        compiler_params=_cparams(("parallel",)), name="mla_kv")(proj, nw.reshape(1, R), wkv, aux)


def _flash_kernel(qt_ref, k_ref, vt_ref, o_ref, acc_ref, m_ref, l_ref, sa_ref, sb_ref, *, t, hp):
    i = pl.program_id(1)
    m_ref[...] = jnp.full(m_ref.shape, -jnp.inf, F32)
    l_ref[...] = jnp.zeros(l_ref.shape, F32)
    acc_ref[...] = jnp.zeros(acc_ref.shape, F32)
    heads = range(hp)

    def scores(j, s_ref):
        for h in heads:
            s_ref[h] = jnp.dot(k_ref[h, j], qt_ref[h], preferred_element_type=F32)

    def softmax_pv(j, s_ref, masked):
        s = [s_ref[h] for h in heads]
        if masked:
            kpos = lax.broadcasted_iota(jnp.int32, (t, t), 0)
            qpos = lax.broadcasted_iota(jnp.int32, (t, t), 1)
            s = [jnp.where(kpos <= qpos, sh, -jnp.inf) for sh in s]
        m_old = [m_ref[h] for h in heads]
        m_new = [jnp.maximum(m_old[h], jnp.max(s[h], axis=0, keepdims=True)) for h in heads]
        alpha = [jnp.exp2(m_old[h] - m_new[h]) for h in heads]
        p = [jnp.exp2(s[h] - m_new[h]) for h in heads]
        for h in heads:
            l_ref[h] = alpha[h] * l_ref[h] + jnp.sum(p[h], axis=0, keepdims=True)
            m_ref[h] = m_new[h]
        pv = [jnp.dot(vt_ref[h, j], p[h].astype(BF16), preferred_element_type=F32) for h in heads]
        for h in heads:
            acc_ref[h] = alpha[h] * acc_ref[h] + pv[h]

    scores(0, sa_ref)

    def body(jj, carry):
        j = 2 * jj
        scores(j + 1, sb_ref)
        softmax_pv(j, sa_ref, False)
        scores(j + 2, sa_ref)
        softmax_pv(j + 1, sb_ref, False)
        return carry

    lax.fori_loop(0, i // 2, body, 0)

    @pl.when(i % 2 == 0)
    def _():
        softmax_pv(i, sa_ref, True)

    @pl.when(i % 2 == 1)
    def _():
        scores(i, sb_ref)
        softmax_pv(i - 1, sa_ref, False)
        softmax_pv(i, sb_ref, True)

    for h in range(hp):
        o_ref[:, h * HEAD_DIM:(h + 1) * HEAD_DIM] = (acc_ref[h] / l_ref[h]).T.astype(o_ref.dtype)


def flash_attention(qt, k4, vt4, *, t, hp):
    H, dq, S = qt.shape
    nk = S // t
    kern = functools.partial(_flash_kernel, t=t, hp=hp)
    return pl.pallas_call(
        kern, grid=(H // hp, nk),
        in_specs=[pl.BlockSpec((hp, dq, t), lambda g, i: (g, 0, i)),
                  pl.BlockSpec((hp, nk, t, dq), lambda g, i: (g, 0, 0, 0)),
                  pl.BlockSpec((hp, nk, HEAD_DIM, t), lambda g, i: (g, 0, 0, 0))],
        out_specs=pl.BlockSpec((t, hp * HEAD_DIM), lambda g, i: (i, g)),
        out_shape=jax.ShapeDtypeStruct((S, H * HEAD_DIM), BF16),
        scratch_shapes=[pltpu.VMEM((hp, HEAD_DIM, t), F32), pltpu.VMEM((hp, 1, t), F32),
                        pltpu.VMEM((hp, 1, t), F32), pltpu.VMEM((hp, t, t), F32), pltpu.VMEM((hp, t, t), F32)],
        compiler_params=_cparams(("parallel", "arbitrary")), name="mla_flash")(qt, k4, vt4)


def _conv_kernel(x_ref, halo_ref, w_ref, *refs, l2norm, out_scale, emit_t, heads_per_block):
    o_refs, xs_ref = refs[:-1], refs[-1]
    i = pl.program_id(0)
    tm = x_ref.shape[0]
    x = x_ref[...].astype(F32)
    halo = halo_ref[...].astype(F32)[8:16, :]
    xs_ref[0:8, :] = jnp.where(i == 0, 0.0, halo)
    xs_ref[8:, :] = x
    w = w_ref[...]
    acc = x * w[GDN_CONV - 1:GDN_CONV, :]
    for s in range(1, GDN_CONV):
        acc = acc + xs_ref[8 - s:8 - s + tm, :] * w[GDN_CONV - 1 - s:GDN_CONV - s, :]
    y = acc * jax.nn.sigmoid(acc)
    if l2norm:
        parts = []
        for h in range(heads_per_block):
            yh = y[:, h * HEAD_DIM:(h + 1) * HEAD_DIM]
            parts.append(yh * (lax.rsqrt(jnp.sum(yh * yh, axis=-1, keepdims=True) + L2_EPS) * out_scale))
        y = jnp.concatenate(parts, axis=1)
    o_refs[0][...] = y
    if emit_t:
        o_refs[1][...] = y.T


def gdn_conv(proj, col_block0, conv_w, wcol_block0, n_cols, *, tm, tc, l2norm, out_scale=1.0, emit_t=False):
    S = proj.shape[0]
    kern = functools.partial(_conv_kernel, l2norm=l2norm, out_scale=out_scale, emit_t=emit_t,
                             heads_per_block=tc // HEAD_DIM)
    out_specs = [pl.BlockSpec((tm, tc), lambda i, j: (i, j))]
    out_shape = [jax.ShapeDtypeStruct((S, n_cols), F32)]
    if emit_t:
        out_specs.append(pl.BlockSpec((tc, tm), lambda i, j: (j, i)))
        out_shape.append(jax.ShapeDtypeStruct((n_cols, S), F32))
    hb = tm // 16
    return pl.pallas_call(
        kern, grid=(S // tm, n_cols // tc),
        in_specs=[pl.BlockSpec((tm, tc), lambda i, j: (i, col_block0 + j)),
                  pl.BlockSpec((16, tc), lambda i, j: (jnp.maximum(i * hb - 1, 0), col_block0 + j)),
                  pl.BlockSpec((GDN_CONV, tc), lambda i, j: (0, wcol_block0 + j))],
        out_specs=out_specs, out_shape=out_shape,
        scratch_shapes=[pltpu.VMEM((tm + 8, tc), F32)],
        compiler_params=_cparams(("parallel", "parallel")), name="gdn_conv")(proj, proj, conv_w)


def _split_dot(a, b, passes):
    a_hi = a.astype(BF16)
    b_hi = b.astype(BF16)
    out = jnp.dot(a_hi, b_hi, preferred_element_type=F32)
    if passes == 3:
        a_lo = (a - a_hi.astype(F32)).astype(BF16)
        b_lo = (b - b_hi.astype(F32)).astype(BF16)
        out = out + jnp.dot(a_hi, b_lo, preferred_element_type=F32) + jnp.dot(a_lo, b_hi, preferred_element_type=F32)
    return out


def _cumsum_kernel(g_ref, o_ref, *, chunk):
    r = lax.broadcasted_iota(jnp.int32, (chunk, chunk), 0)
    c = lax.broadcasted_iota(jnp.int32, (chunk, chunk), 1)
    trilb = jnp.where(r >= c, 1.0, 0.0).astype(BF16)
    for r0 in range(0, g_ref.shape[0], chunk):
        g = g_ref[r0:r0 + chunk, :]
        g_hi = g.astype(BF16).astype(F32)
        g_mid = (g - g_hi).astype(BF16).astype(F32)
        g_lo = g - g_hi - g_mid
        o_ref[r0:r0 + chunk, :] = (jnp.dot(trilb, g_hi.astype(BF16), preferred_element_type=F32)
                                   + jnp.dot(trilb, g_mid.astype(BF16), preferred_element_type=F32)
                                   + jnp.dot(trilb, g_lo.astype(BF16), preferred_element_type=F32))


def chunk_cumsum(g, chunk, tm):
    M = g.shape[0]
    tm = min(tm, M)
    row = pl.BlockSpec((tm, LANES), lambda i: (i, 0))
    return pl.pallas_call(
        functools.partial(_cumsum_kernel, chunk=chunk), grid=(M // tm,), in_specs=[row], out_specs=row,
        out_shape=jax.ShapeDtypeStruct((M, LANES), F32),
        compiler_params=_cparams(("parallel",)), name="gdn_cumsum")(g)


def _gdn_kernel(q_ref, k_ref, kt_ref, v_ref, gate_ref, gcc_ref, bc_ref, gcr_ref, nw_ref, o_ref, s_ref,
                *, hb, rows, chunk, solve_passes, state_passes, refine):
    @pl.when(pl.program_id(1) == 0)
    def _():
        s_ref[...] = jnp.zeros(s_ref.shape, F32)

    C = chunk
    ri = lax.broadcasted_iota(jnp.int32, (C, C), 0)
    ci = lax.broadcasted_iota(jnp.int32, (C, C), 1)
    lower = ri >= ci
    strict = ri > ci
    nw = nw_ref[...]
    n_steps = C.bit_length() - 1
    n_chunks = rows // C
    bodies = [(c, h) for c in range(n_chunks) for h in range(hb)]

    def rows_of(c):
        return slice(c * C, (c + 1) * C)

    def cols_of(h):
        return slice(h * HEAD_DIM, (h + 1) * HEAD_DIM)

    xs, amats, rhs, qks, qds, tails, eglasts = {}, {}, {}, {}, {}, {}, {}
    for b in bodies:
        c, h = b
        q = q_ref[rows_of(c), cols_of(h)]
        k = k_ref[rows_of(c), cols_of(h)]
        v = v_ref[rows_of(c), cols_of(h)]
        kt = kt_ref[cols_of(h), rows_of(c)]
        gcc = gcc_ref[rows_of(c), h:h + 1]
        bc = bc_ref[rows_of(c), h:h + 1]
        gcr = gcr_ref[h:h + 1, rows_of(c)]
        glast = gcr[:, C - 1:C]
        decay = jnp.exp(jnp.where(lower, gcc - gcr, -jnp.inf))
        egc = jnp.exp(gcc)
        kb = k * bc
        amats[b] = jnp.where(strict, _split_dot(kb, kt, state_passes) * decay, 0.0)
        rhs[b] = jnp.concatenate([v * bc, kb * egc], axis=1)
        qks[b] = jnp.where(lower, _split_dot(q, kt, state_passes) * decay, 0.0)
        qds[b] = q * egc
        tails[b] = kt * jnp.exp(glast - gcr)
        eglasts[b] = jnp.exp(glast)

    xor = ri ^ ci
    tinv = {b: jnp.where(ri == ci, 1.0, 0.0) - jnp.where(strict & (xor == 1), amats[b], 0.0) for b in bodies}
    for level in range(1, n_steps):
        join = strict & ((xor >> level) == 1)
        mids = {b: _split_dot(jnp.where(join, amats[b], 0.0), tinv[b], solve_passes) for b in bodies}
        for b in bodies:
            tinv[b] = tinv[b] - _split_dot(tinv[b], mids[b], solve_passes)
    for b in bodies:
        xs[b] = _split_dot(tinv[b], rhs[b], solve_passes)
    if refine:
        for b in bodies:
            lx = xs[b] + _split_dot(amats[b], xs[b], 3)
            xs[b] = xs[b] + _split_dot(tinv[b], rhs[b] - lx, 1)

    states = [s_ref[h] for h in range(hb)]
    for c in range(n_chunks):
        ws = [_split_dot(jnp.concatenate([xs[(c, h)][:, HEAD_DIM:], qds[(c, h)]], axis=0), states[h], state_passes)
              for h in range(hb)]
        v_new = [xs[(c, h)][:, :HEAD_DIM] - ws[h][:C] for h in range(hb)]
        outs = [ws[h][C:] + _split_dot(qks[(c, h)], v_new[h], state_passes) for h in range(hb)]
        states = [states[h] * eglasts[(c, h)] + _split_dot(tails[(c, h)], v_new[h], state_passes)
                  for h in range(hb)]
        for h in range(hb):
            o = outs[h]
            o = o * lax.rsqrt(jnp.mean(o * o, axis=-1, keepdims=True) + NORM_EPS) * nw
            gate = gate_ref[rows_of(c), cols_of(h)].astype(F32)
            o_ref[rows_of(c), cols_of(h)] = (o * (gate * jax.nn.sigmoid(gate))).astype(o_ref.dtype)
    for h in range(hb):
        s_ref[h] = states[h]


def gdn_delta(q, k, kt, v, proj, gate_block0, gcc, bc, gcr, nw, *, hb, rows, chunk, solve_passes, state_passes,
              refine):
    S, W = q.shape
    tc = hb * HEAD_DIM
    kern = functools.partial(_gdn_kernel, hb=hb, rows=rows, chunk=chunk,
                             solve_passes=solve_passes, state_passes=state_passes, refine=refine)
    blk = pl.BlockSpec((rows, tc), lambda g, r: (r, g))
    return pl.pallas_call(
        kern, grid=(W // tc, S // rows),
        in_specs=[blk, blk, pl.BlockSpec((tc, rows), lambda g, r: (g, r)), blk,
                  pl.BlockSpec((rows, tc), lambda g, r: (r, gate_block0 + g)),
                  pl.BlockSpec((None, rows, hb), lambda g, r: (g, r, 0)),
                  pl.BlockSpec((None, rows, hb), lambda g, r: (g, r, 0)),
                  pl.BlockSpec((None, hb, rows), lambda g, r: (g, 0, r)),
                  pl.BlockSpec((1, HEAD_DIM), lambda g, r: (0, 0))],
        out_specs=blk,
        out_shape=jax.ShapeDtypeStruct((S, W), BF16),
        scratch_shapes=[pltpu.VMEM((hb, HEAD_DIM, HEAD_DIM), F32)],
        compiler_params=_cparams(("parallel", "arbitrary")), name="gdn_delta")(
            q, k, kt, v, proj, gcc, bc, gcr, nw.reshape(1, HEAD_DIM))


def _xattn_kernel(q_ref, kt_ref, v_ref, o_ref, *, heads):
    d = q_ref.shape[1] // heads
    cols = [slice(h * d, (h + 1) * d) for h in range(heads)]
    s = [jnp.dot(q_ref[:, c], kt_ref[c, :], preferred_element_type=F32) for c in cols]
    p = [jnp.exp2(sh - jnp.max(sh, axis=-1, keepdims=True)) for sh in s]
    l = [jnp.sum(ph, axis=-1, keepdims=True) for ph in p]
    o = [jnp.dot(ph.astype(BF16), v_ref[:, c], preferred_element_type=F32) for ph, c in zip(p, cols)]
    for c, oh, lh in zip(cols, o, l):
        o_ref[:, c] = (oh / lh).astype(o_ref.dtype)


def xattn_core(q, kt, v, *, heads, tq):
    S, W = q.shape
    M = v.shape[0]
    return pl.pallas_call(
        functools.partial(_xattn_kernel, heads=heads), grid=(S // tq,),
        in_specs=[pl.BlockSpec((tq, W), lambda i: (i, 0)),
                  pl.BlockSpec((W, M), lambda i: (0, 0)),
                  pl.BlockSpec((M, W), lambda i: (0, 0))],
        out_specs=pl.BlockSpec((tq, W), lambda i: (i, 0)),
        out_shape=jax.ShapeDtypeStruct((S, W), BF16),
        compiler_params=_cparams(("parallel",)), name="xattn_core")(q, kt, v)


def _block(x, mem, positions, attn_norm_w, w_in, mla_q_norm_w, mla_w_q_b, mla_kv_norm_w, mla_w_kv_b,
           gdn_conv_w, gdn_a_log, gdn_dt_bias, gdn_norm_w, w_out, xattn_norm_w, mem_norm_w,
           xattn_wq, xattn_wk, xattn_wv, xattn_wo, mlp_norm_w, mlp_w_up, mlp_w_down, final_norm_w):
    S, D = x.shape
    n_mix = w_out.shape[0]
    mla_heads = n_mix // 2 // HEAD_DIM
    gdn_heads = gdn_conv_w.shape[1] // (3 * HEAD_DIM)
    gw = gdn_heads * HEAD_DIM
    dq = MLA_NOPE + MLA_ROPE
    half = MLA_ROPE // 2

    o_kv = MLA_Q_RANK
    o_pe = o_kv + MLA_KV_RANK
    o_qkv = o_pe + MLA_ROPE
    o_a = o_qkv + 3 * gw
    o_b = o_a + gdn_heads
    o_gate = o_b + gdn_heads
    w_in_t = w_in.T
    c_qkv = o_pe
    c_gate = o_pe + 3 * gw
    n_proj = c_gate + (w_in.shape[1] - o_gate)
    tn_proj = 512

    skip1 = o_qkv - c_qkv
    skip2 = (o_gate - c_gate) - skip1
    unit = math.gcd(tn_proj, skip1, skip2)

    def proj_src_row(j):
        v = j * tn_proj
        past1 = (v >= c_qkv).astype(jnp.int32)
        past2 = (v >= c_gate).astype(jnp.int32)
        return (j * (tn_proj // unit) + past1 * (skip1 // unit) + past2 * (skip2 // unit)) * unit

    wq3 = mla_w_q_b.reshape(MLA_Q_RANK, mla_heads, dq)
    wq_perm = jnp.concatenate([wq3[:, :, :MLA_NOPE].reshape(MLA_Q_RANK, -1),
                               wq3[:, :, MLA_NOPE:MLA_NOPE + half].reshape(MLA_Q_RANK, -1),
                               wq3[:, :, MLA_NOPE + half:].reshape(MLA_Q_RANK, -1)], axis=1).astype(BF16)
    wkv3 = mla_w_kv_b.reshape(MLA_KV_RANK, mla_heads, MLA_NOPE + HEAD_DIM)
    wkv_perm = jnp.concatenate([wkv3[:, :, :MLA_NOPE].reshape(MLA_KV_RANK, -1),
                                wkv3[:, :, MLA_NOPE:].reshape(MLA_KV_RANK, -1)], axis=1).astype(BF16)

    inv_freq = ROPE_BASE ** (-jnp.arange(half, dtype=F32) / half)
    invf_row = jnp.tile(inv_freq, LANES // half).reshape(1, LANES)
    pad_l = jnp.zeros((MLA_ROPE,), F32)
    pad_r = jnp.zeros((LANES - MLA_ROPE - gdn_heads,), F32)
    alog_row = jnp.concatenate([pad_l, gdn_a_log.astype(F32), pad_r]).reshape(1, LANES)
    dtb_row = jnp.concatenate([pad_l, gdn_dt_bias.astype(F32), pad_r]).reshape(1, LANES)

    xn = rms_norm(x, attn_norm_w, BF16, 512)
    proj = proj_nt(xn, w_in_t, proj_src_row, n_proj, tm=1024, tn=tn_proj, name="proj_main")
    small = proj_small(xn, w_in_t, o_pe, MLA_ROPE, o_a, 2 * gdn_heads, tm=1024)
    aux, cos, sin = gates(small, positions.reshape(S, 1), invf_row, alog_row, dtb_row, 1024)

    t_att = 512
    qt = mla_q(proj, mla_q_norm_w, wq_perm, cos, sin, heads=mla_heads, tm=256, scale=dq ** -0.5 * LOG2E)
    k_full, vt4 = mla_kv(proj, o_kv // MLA_KV_RANK, mla_kv_norm_w, wkv_perm, aux, heads=mla_heads, tm=t_att)
    k4 = k_full.reshape(mla_heads, S // t_att, t_att, dq)
    mla_out = flash_attention(qt, k4, vt4, t=t_att, hp=2)

    tc = 512
    gq = gdn_conv(proj, c_qkv // tc, gdn_conv_w, 0, gw, tm=512, tc=tc, l2norm=True, out_scale=HEAD_DIM ** -0.5)[0]
    gk, gkt = gdn_conv(proj, (c_qkv + gw) // tc, gdn_conv_w, gw // tc, gw, tm=512, tc=tc, l2norm=True, emit_t=True)
    gv = gdn_conv(proj, (c_qkv + 2 * gw) // tc, gdn_conv_w, 2 * gw // tc, gw, tm=512, tc=tc, l2norm=False)[0]
    hb = 4
    gc_full = chunk_cumsum(aux, GDN_CHUNK, 1024)
    gc = gc_full[:, MLA_ROPE:MLA_ROPE + gdn_heads]
    beta = aux[:, MLA_ROPE + gdn_heads:MLA_ROPE + 2 * gdn_heads]
    gcc = gc.reshape(S, gdn_heads // hb, hb).transpose(1, 0, 2)
    bc = beta.reshape(S, gdn_heads // hb, hb).transpose(1, 0, 2)
    gcr = gc.T.reshape(gdn_heads // hb, hb, S)
    gdn_out = gdn_delta(gq, gk, gkt, gv, proj, c_gate // (hb * HEAD_DIM), gcc, bc, gcr, gdn_norm_w,
                        hb=hb, rows=512, chunk=GDN_CHUNK, solve_passes=GDN_SOLVE_PASSES,
                        state_passes=GDN_STATE_PASSES, refine=GDN_REFINE)

    h1 = matmul2_res(mla_out, gdn_out, w_out, x, tm=1024, tn=512, name="w_out")

    hn = rms_norm(h1, xattn_norm_w, BF16, 512)
    memn = rms_norm(mem, mem_norm_w, BF16, 256)
    xd = D // XATTN_HEADS
    xq = matmul(hn, xattn_wq, tm=1024, tn=512, tk=D, out_dtype=BF16, out_scale=xd ** -0.5 * LOG2E, name="xattn_q")
    xk = matmul(memn, xattn_wk, tm=256, tn=1024, tk=1024, out_dtype=BF16, name="xattn_k")
    xv = matmul(memn, xattn_wv, tm=256, tn=1024, tk=1024, out_dtype=BF16, name="xattn_v")
    xo = xattn_core(xq, xk.T, xv, heads=XATTN_HEADS, tq=512)
    h2 = matmul(xo, xattn_wo, tm=1024, tn=512, tk=D, out_dtype=F32, res=h1, name="xattn_o")

    hn2 = rms_norm(h2, mlp_norm_w, BF16, 512)
    up = matmul(hn2, mlp_w_up, tm=1024, tn=512, tk=D, out_dtype=BF16, relu2=True, name="mlp_up")
    h3 = matmul_kacc(up, mlp_w_down, h2, tm=2048, tn=1024, tk=1024, row_chunk=256, name="mlp_down")
    return rms_norm(h3, final_norm_w, F32, 512)


def kernel(x, mem, positions, attn_norm_w, w_in, mla_q_norm_w, mla_w_q_b, mla_kv_norm_w, mla_w_kv_b, gdn_conv_w, gdn_a_log, gdn_dt_bias, gdn_norm_w, w_out, xattn_norm_w, mem_norm_w, xattn_wq, xattn_wk, xattn_wv, xattn_wo, mlp_norm_w, mlp_w_up, mlp_w_down, final_norm_w):
    assert x.shape[0] == 1 and attn_norm_w.shape[0] == 1, "single sequence, single layer"
    out = _block(x[0], mem[0], positions[0], attn_norm_w[0], w_in[0], mla_q_norm_w[0], mla_w_q_b[0],
                 mla_kv_norm_w[0], mla_w_kv_b[0], gdn_conv_w[0], gdn_a_log[0], gdn_dt_bias[0], gdn_norm_w[0],
                 w_out[0], xattn_norm_w[0], mem_norm_w[0], xattn_wq[0], xattn_wk[0], xattn_wv[0], xattn_wo[0],
                 mlp_norm_w[0], mlp_w_up[0], mlp_w_down[0], final_norm_w)
    return out[None]
```

```python
import functools
import math

import jax
import jax.numpy as jnp
from jax import lax
from jax.experimental import pallas as pl
from jax.experimental.pallas import tpu as pltpu

F32 = jnp.float32
BF16 = jnp.bfloat16

HEAD_DIM = 128
MLA_NOPE = 128
MLA_ROPE = 64
MLA_Q_RANK = 1024
MLA_KV_RANK = 512
GDN_CONV = 4
XATTN_HEADS = 4
ROPE_BASE = 10000.0
NORM_EPS = 1e-6
L2_EPS = 1e-6
LOG2E = math.log2(math.e)

V7X_VMEM_BYTES = 64 * 1024 * 1024
VMEM_LIMIT = 52 * 1024 * 1024
LANES = 128

GDN_CHUNK = 128


def _cparams(sem):
    return pltpu.CompilerParams(dimension_semantics=sem, vmem_limit_bytes=VMEM_LIMIT)


def _rms_kernel(x_ref, w_ref, o_ref):
    x = x_ref[...].astype(F32)
    r = lax.rsqrt(jnp.mean(x * x, axis=-1, keepdims=True) + NORM_EPS)
    o_ref[...] = (x * r * w_ref[...]).astype(o_ref.dtype)


def _rms_add_kernel(x_ref, y_ref, w_ref, o_ref):
    x = x_ref[...] + y_ref[...]
    r = lax.rsqrt(jnp.mean(x * x, axis=-1, keepdims=True) + NORM_EPS)
    o_ref[...] = (x * r * w_ref[...]).astype(o_ref.dtype)


def rms_norm(x, w, out_dtype, tm, add=None):
    M, D = x.shape
    tm = min(tm, M)
    row = pl.BlockSpec((tm, D), lambda i: (i, 0))
    wspec = pl.BlockSpec((1, D), lambda i: (0, 0))
    if add is None:
        kern, specs, args = _rms_kernel, [row, wspec], (x, w.reshape(1, D))
    else:
        kern, specs, args = _rms_add_kernel, [row, row, wspec], (x, add, w.reshape(1, D))
    return pl.pallas_call(
        kern, grid=(M // tm,), in_specs=specs, out_specs=row,
        out_shape=jax.ShapeDtypeStruct((M, D), out_dtype),
        compiler_params=_cparams(("parallel",)), name="rms_norm")(*args)


def _mm_kernel(*refs, nk, relu2, has_res, out_scale):
    x_ref, w_ref = refs[0], refs[1]
    res_ref = refs[2] if has_res else None
    o_ref = refs[2 + has_res]

    def finish(r):
        if relu2:
            r = jnp.square(jnp.maximum(r, 0.0))
        if out_scale is not None:
            r = r * out_scale
        if has_res:
            r = r + res_ref[...]
        o_ref[...] = r.astype(o_ref.dtype)

    part = jnp.dot(x_ref[...], w_ref[...].astype(BF16), preferred_element_type=F32)
    if nk == 1:
        finish(part)
        return
    acc_ref = refs[3 + has_res]
    k = pl.program_id(2)

    @pl.when(k == 0)
    def _():
        acc_ref[...] = part

    @pl.when(jnp.logical_and(k > 0, k < nk - 1))
    def _():
        acc_ref[...] += part

    @pl.when(k == nk - 1)
    def _():
        finish(acc_ref[...] + part)


def matmul(x, w, *, tm, tn, tk, out_dtype, relu2=False, res=None, out_scale=None, name="matmul"):
    M, K = x.shape
    K2, N = w.shape
    assert K == K2
    tm, tn, tk = min(tm, M), min(tn, N), min(tk, K)
    assert M % tm == 0 and N % tn == 0 and K % tk == 0
    nk = K // tk
    in_specs = [pl.BlockSpec((tm, tk), lambda i, j, k: (i, k)),
                pl.BlockSpec((tk, tn), lambda i, j, k: (k, j))]
    args = [x, w]
    if res is not None:
        in_specs.append(pl.BlockSpec((tm, tn), lambda i, j, k: (i, j)))
        args.append(res)
    scratch = [pltpu.VMEM((tm, tn), F32)] if nk > 1 else []
    kern = functools.partial(_mm_kernel, nk=nk, relu2=relu2, has_res=res is not None, out_scale=out_scale)
    return pl.pallas_call(
        kern, grid=(M // tm, N // tn, nk), in_specs=in_specs,
        out_specs=pl.BlockSpec((tm, tn), lambda i, j, k: (i, j)),
        out_shape=jax.ShapeDtypeStruct((M, N), out_dtype),
        scratch_shapes=scratch,
        compiler_params=_cparams(("parallel", "parallel", "arbitrary")), name=name)(*args)


_NT_DIMS = (((1,), (1,)), ((), ()))


def _proj_nt_kernel(x_ref, wt_ref, o_ref):
    wb = wt_ref[...].astype(BF16)
    o_ref[...] = lax.dot_general(x_ref[...], wb, _NT_DIMS, preferred_element_type=F32).astype(o_ref.dtype)


def proj_nt(x, wt, src_row_of_block, n_out, *, tm, tn, name):
    M, K = x.shape
    return pl.pallas_call(
        _proj_nt_kernel, grid=(M // tm, n_out // tn),
        in_specs=[pl.BlockSpec((tm, K), lambda i, j: (i, 0)),
                  pl.BlockSpec((pl.Element(tn), pl.Element(K)), lambda i, j: (src_row_of_block(j), 0))],
        out_specs=pl.BlockSpec((tm, tn), lambda i, j: (i, j)),
        out_shape=jax.ShapeDtypeStruct((M, n_out), BF16),
        compiler_params=_cparams(("parallel", "parallel")), name=name)(x, wt)


def _proj_small_kernel(x_ref, wa_ref, wb_ref, o_ref):
    wa = wa_ref[...].astype(BF16)
    wb = wb_ref[...].astype(BF16)
    pad = jnp.zeros((LANES - wa.shape[0] - wb.shape[0], wa.shape[1]), BF16)
    w = jnp.concatenate([wa, wb, pad], axis=0)
    o_ref[...] = lax.dot_general(x_ref[...], w, _NT_DIMS, preferred_element_type=F32)


def proj_small(x, wt, row_a, n_a, row_b, n_b, *, tm):
    M, K = x.shape
    return pl.pallas_call(
        _proj_small_kernel, grid=(M // tm,),
        in_specs=[pl.BlockSpec((tm, K), lambda i: (i, 0)),
                  pl.BlockSpec((pl.Element(n_a), pl.Element(K)), lambda i: (row_a, 0)),
                  pl.BlockSpec((pl.Element(n_b), pl.Element(K)), lambda i: (row_b, 0))],
        out_specs=pl.BlockSpec((tm, LANES), lambda i: (i, 0)),
        out_shape=jax.ShapeDtypeStruct((M, LANES), F32),
        compiler_params=_cparams(("parallel",)), name="proj_small")(x, wt, wt)


def _mm2_kernel(x1_ref, x2_ref, w_ref, res_ref, o_ref):
    k1 = x1_ref.shape[1]
    wb = w_ref[...].astype(BF16)
    r = jnp.dot(x1_ref[...], wb[:k1], preferred_element_type=F32)
    r = r + jnp.dot(x2_ref[...], wb[k1:], preferred_element_type=F32)
    o_ref[...] = r + res_ref[...]


def matmul2_res(x1, x2, w, res, *, tm, tn, name):
    M, K1 = x1.shape
    K2 = x2.shape[1]
    N = w.shape[1]
    return pl.pallas_call(
        _mm2_kernel, grid=(M // tm, N // tn),
        in_specs=[pl.BlockSpec((tm, K1), lambda i, j: (i, 0)),
                  pl.BlockSpec((tm, K2), lambda i, j: (i, 0)),
                  pl.BlockSpec((K1 + K2, tn), lambda i, j: (0, j)),
                  pl.BlockSpec((tm, tn), lambda i, j: (i, j))],
        out_specs=pl.BlockSpec((tm, tn), lambda i, j: (i, j)),
        out_shape=jax.ShapeDtypeStruct((M, N), F32),
        compiler_params=_cparams(("parallel", "parallel")), name=name)(x1, x2, w, res)


def _mm_kacc_kernel(x_ref, w_ref, o_ref, *, row_chunk):
    k = pl.program_id(2)
    wb = w_ref[...].astype(BF16)
    tm = x_ref.shape[0]

    @pl.when(k == 0)
    def _():
        for r in range(0, tm, row_chunk):
            rows = slice(r, r + row_chunk)
            o_ref[rows, :] = jnp.dot(x_ref[rows, :], wb, preferred_element_type=F32)

    @pl.when(k > 0)
    def _():
        for r in range(0, tm, row_chunk):
            rows = slice(r, r + row_chunk)
            o_ref[rows, :] += jnp.dot(x_ref[rows, :], wb, preferred_element_type=F32)


def matmul_kacc(x, w, *, tm, tn, tk, row_chunk, name):
    M, K = x.shape
    N = w.shape[1]
    tm, tn, tk = min(tm, M), min(tn, N), min(tk, K)
    assert M % tm == 0 and N % tn == 0 and K % tk == 0 and tm % row_chunk == 0
    return pl.pallas_call(
        functools.partial(_mm_kacc_kernel, row_chunk=row_chunk), grid=(M // tm, N // tn, K // tk),
        in_specs=[pl.BlockSpec((tm, tk), lambda i, j, k: (i, k)),
                  pl.BlockSpec((tk, tn), lambda i, j, k: (k, j))],
        out_specs=pl.BlockSpec((tm, tn), lambda i, j, k: (i, j)),
        out_shape=jax.ShapeDtypeStruct((M, N), F32),
        compiler_params=_cparams(("parallel", "parallel", "arbitrary")), name=name)(x, w)


def _gates_kernel(y_ref, pos_ref, invf_ref, alog_ref, dtb_ref, aux_ref, cos_ref, sin_ref):
    y = y_ref[...]
    ang = pos_ref[...].astype(F32) * invf_ref[...]
    c = jnp.cos(ang)
    s = jnp.sin(ang)
    cos_ref[...] = c
    sin_ref[...] = s
    lane = lax.broadcasted_iota(jnp.int32, y.shape, 1)
    half = MLA_ROPE // 2
    rot = jnp.where(lane < half, -pltpu.roll(y, LANES - half, 1), pltpu.roll(y, half, 1))
    roped = y * c + rot * s
    z = y + dtb_ref[...]
    softplus = jnp.maximum(z, 0.0) + jnp.log1p(jnp.exp(-jnp.abs(z)))
    g = -jnp.exp(alog_ref[...]) * softplus
    beta = jax.nn.sigmoid(y)
    aux_ref[...] = jnp.where(lane < MLA_ROPE, roped, jnp.where(lane < MLA_ROPE + 16, g, beta))


def gates(y, pos_col, invf_row, alog_row, dtb_row, tm):
    M = y.shape[0]
    tm = min(tm, M)
    row = pl.BlockSpec((tm, LANES), lambda i: (i, 0))
    one = pl.BlockSpec((1, LANES), lambda i: (0, 0))
    shp = jax.ShapeDtypeStruct((M, LANES), F32)
    return pl.pallas_call(
        _gates_kernel, grid=(M // tm,),
        in_specs=[row, pl.BlockSpec((tm, 1), lambda i: (i, 0)), one, one, one],
        out_specs=[row, row, row], out_shape=[shp, shp, shp],
        compiler_params=_cparams(("parallel",)), name="gates")(y, pos_col, invf_row, alog_row, dtb_row)


def _mla_q_kernel(cq_ref, nw_ref, wq_ref, cos_ref, sin_ref, o_ref, *, heads, scale):
    x = cq_ref[...].astype(F32)
    r = lax.rsqrt(jnp.mean(x * x, axis=-1, keepdims=True) + NORM_EPS)
    xn = (x * r * nw_ref[...]).astype(BF16)
    q = jnp.dot(xn, wq_ref[...], preferred_element_type=F32) * scale
    half = MLA_ROPE // 2
    n_nope = heads * MLA_NOPE
    n_half = heads * half
    reps = n_half // LANES
    c = jnp.concatenate([cos_ref[...]] * reps, axis=1)
    s = jnp.concatenate([sin_ref[...]] * reps, axis=1)
    p1 = q[:, n_nope:n_nope + n_half]
    p2 = q[:, n_nope + n_half:]
    r1t = (p1 * c - p2 * s).T.astype(BF16)
    r2t = (p2 * c + p1 * s).T.astype(BF16)
    for h in range(heads):
        o_ref[h, 0:MLA_NOPE, :] = q[:, h * MLA_NOPE:(h + 1) * MLA_NOPE].T.astype(BF16)
        o_ref[h, MLA_NOPE:MLA_NOPE + half, :] = r1t[h * half:(h + 1) * half, :]
        o_ref[h, MLA_NOPE + half:MLA_NOPE + 2 * half, :] = r2t[h * half:(h + 1) * half, :]


def mla_q(proj, nw, wq, cos, sin, *, heads, tm, scale):
    S = proj.shape[0]
    R = nw.shape[0]
    dq = MLA_NOPE + MLA_ROPE
    kern = functools.partial(_mla_q_kernel, heads=heads, scale=scale)
    return pl.pallas_call(
        kern, grid=(S // tm,),
        in_specs=[pl.BlockSpec((tm, R), lambda i: (i, 0)),
                  pl.BlockSpec((1, R), lambda i: (0, 0)),
                  pl.BlockSpec(wq.shape, lambda i: (0, 0)),
                  pl.BlockSpec((tm, LANES), lambda i: (i, 0)),
                  pl.BlockSpec((tm, LANES), lambda i: (i, 0))],
        out_specs=pl.BlockSpec((heads, dq, tm), lambda i: (0, 0, i)),
        out_shape=jax.ShapeDtypeStruct((heads, dq, S), BF16),
        compiler_params=_cparams(("parallel",)), name="mla_q")(proj, nw.reshape(1, R), wq, cos, sin)


def _mla_kv_kernel(ckv_ref, nw_ref, wkv_ref, aux_ref, k_ref, vt_ref, *, heads):
    x = ckv_ref[...].astype(F32)
    r = lax.rsqrt(jnp.mean(x * x, axis=-1, keepdims=True) + NORM_EPS)
    xn = (x * r * nw_ref[...]).astype(BF16)
    kv = jnp.dot(xn, wkv_ref[...], preferred_element_type=F32)
    kpe = aux_ref[...][:, 0:MLA_ROPE].astype(BF16)
    n_nope = heads * MLA_NOPE
    for h in range(heads):
        k_ref[h, :, 0:MLA_NOPE] = kv[:, h * MLA_NOPE:(h + 1) * MLA_NOPE].astype(BF16)
        k_ref[h, :, MLA_NOPE:MLA_NOPE + MLA_ROPE] = kpe
        vt_ref[h, 0] = kv[:, n_nope + h * HEAD_DIM:n_nope + (h + 1) * HEAD_DIM].T.astype(BF16)


def mla_kv(proj, col_block, nw, wkv, aux, *, heads, tm):
    S = proj.shape[0]
    R = nw.shape[0]
    dq = MLA_NOPE + MLA_ROPE
    kern = functools.partial(_mla_kv_kernel, heads=heads)
    return pl.pallas_call(
        kern, grid=(S // tm,),
        in_specs=[pl.BlockSpec((tm, R), lambda i: (i, col_block)),
                  pl.BlockSpec((1, R), lambda i: (0, 0)),
                  pl.BlockSpec(wkv.shape, lambda i: (0, 0)),
                  pl.BlockSpec((tm, LANES), lambda i: (i, 0))],
        out_specs=[pl.BlockSpec((heads, tm, dq), lambda i: (0, i, 0)),
                   pl.BlockSpec((heads, 1, HEAD_DIM, tm), lambda i: (0, i, 0, 0))],
        out_shape=[jax.ShapeDtypeStruct((heads, S, dq), BF16),
                   jax.ShapeDtypeStruct((heads, S // tm, HEAD_DIM, tm), BF16)],
        compiler_params=_cparams(("parallel",)), name="mla_kv")(proj, nw.reshape(1, R), wkv, aux)


def _flash_kernel(qt_ref, k_ref, vt_ref, o_ref, acc_ref, m_ref, l_ref, sa_ref, sb_ref, *, t, hp):
    i = pl.program_id(1)
    m_ref[...] = jnp.full(m_ref.shape, -jnp.inf, F32)
    l_ref[...] = jnp.zeros(l_ref.shape, F32)
    acc_ref[...] = jnp.zeros(acc_ref.shape, F32)
    heads = range(hp)

    def scores(j, s_ref):
        for h in heads:
            s_ref[h] = jnp.dot(k_ref[h, j], qt_ref[h], preferred_element_type=F32)

    def softmax_pv(j, s_ref, masked):
        s = [s_ref[h] for h in heads]
        if masked:
            kpos = lax.broadcasted_iota(jnp.int32, (t, t), 0)
            qpos = lax.broadcasted_iota(jnp.int32, (t, t), 1)
            s = [jnp.where(kpos <= qpos, sh, -jnp.inf) for sh in s]
        m_old = [m_ref[h] for h in heads]
        m_new = [jnp.maximum(m_old[h], jnp.max(s[h], axis=0, keepdims=True)) for h in heads]
        alpha = [jnp.exp2(m_old[h] - m_new[h]) for h in heads]
        p = [jnp.exp2(s[h] - m_new[h]) for h in heads]
        for h in heads:
            l_ref[h] = alpha[h] * l_ref[h] + jnp.sum(p[h], axis=0, keepdims=True)
            m_ref[h] = m_new[h]
        pv = [jnp.dot(vt_ref[h, j], p[h].astype(BF16), preferred_element_type=F32) for h in heads]
        for h in heads:
            acc_ref[h] = alpha[h] * acc_ref[h] + pv[h]

    scores(0, sa_ref)

    def body(jj, carry):
        j = 2 * jj
        scores(j + 1, sb_ref)
        softmax_pv(j, sa_ref, False)
        scores(j + 2, sa_ref)
        softmax_pv(j + 1, sb_ref, False)
        return carry

    lax.fori_loop(0, i // 2, body, 0)

    @pl.when(i % 2 == 0)
    def _():
        softmax_pv(i, sa_ref, True)

    @pl.when(i % 2 == 1)
    def _():
        scores(i, sb_ref)
        softmax_pv(i - 1, sa_ref, False)
        softmax_pv(i, sb_ref, True)

    for h in range(hp):
        o_ref[:, h * HEAD_DIM:(h + 1) * HEAD_DIM] = (acc_ref[h] / l_ref[h]).T.astype(o_ref.dtype)


def flash_attention(qt, k4, vt4, *, t, hp):
    H, dq, S = qt.shape
    nk = S // t
    kern = functools.partial(_flash_kernel, t=t, hp=hp)
    return pl.pallas_call(
        kern, grid=(H // hp, nk),
        in_specs=[pl.BlockSpec((hp, dq, t), lambda g, i: (g, 0, i)),
                  pl.BlockSpec((hp, nk, t, dq), lambda g, i: (g, 0, 0, 0)),
                  pl.BlockSpec((hp, nk, HEAD_DIM, t), lambda g, i: (g, 0, 0, 0))],
        out_specs=pl.BlockSpec((t, hp * HEAD_DIM), lambda g, i: (i, g)),
        out_shape=jax.ShapeDtypeStruct((S, H * HEAD_DIM), BF16),
        scratch_shapes=[pltpu.VMEM((hp, HEAD_DIM, t), F32), pltpu.VMEM((hp, 1, t), F32),
                        pltpu.VMEM((hp, 1, t), F32), pltpu.VMEM((hp, t, t), F32), pltpu.VMEM((hp, t, t), F32)],
        compiler_params=_cparams(("parallel", "arbitrary")), name="mla_flash")(qt, k4, vt4)


def _cumsum_kernel(g_ref, o_ref, *, chunk):
    r = lax.broadcasted_iota(jnp.int32, (chunk, chunk), 0)
    c = lax.broadcasted_iota(jnp.int32, (chunk, chunk), 1)
    trilb = jnp.where(r >= c, 1.0, 0.0).astype(BF16)
    for r0 in range(0, g_ref.shape[0], chunk):
        g = g_ref[r0:r0 + chunk, :]
        g_hi = g.astype(BF16).astype(F32)
        g_mid = (g - g_hi).astype(BF16).astype(F32)
        g_lo = g - g_hi - g_mid
        o_ref[r0:r0 + chunk, :] = (jnp.dot(trilb, g_hi.astype(BF16), preferred_element_type=F32)
                                   + jnp.dot(trilb, g_mid.astype(BF16), preferred_element_type=F32)
                                   + jnp.dot(trilb, g_lo.astype(BF16), preferred_element_type=F32))


def chunk_cumsum(g, chunk, tm):
    M = g.shape[0]
    tm = min(tm, M)
    row = pl.BlockSpec((tm, LANES), lambda i: (i, 0))
    return pl.pallas_call(
        functools.partial(_cumsum_kernel, chunk=chunk), grid=(M // tm,), in_specs=[row], out_specs=row,
        out_shape=jax.ShapeDtypeStruct((M, LANES), F32),
        compiler_params=_cparams(("parallel",)), name="gdn_cumsum")(g)


def _bdot(a, b):
    return jnp.dot(a.astype(BF16), b.astype(BF16), preferred_element_type=F32)


def _conv_silu(x_ref, halo_ref, w_ref, xs_ref, first):
    rows = x_ref.shape[0]
    x = x_ref[...].astype(F32)
    xs_ref[0:8, :] = jnp.where(first, 0.0, halo_ref[...].astype(F32)[8:16, :])
    xs_ref[8:, :] = x
    w = w_ref[...]
    acc = x * w[GDN_CONV - 1:GDN_CONV, :]
    for s in range(1, GDN_CONV):
        acc = acc + xs_ref[8 - s:8 - s + rows, :] * w[GDN_CONV - 1 - s:GDN_CONV - s, :]
    return acc * jax.nn.sigmoid(acc)


def _l2n(y, scale):
    return y * (lax.rsqrt(jnp.sum(y * y, axis=-1, keepdims=True) + L2_EPS) * scale)


def _gdn_kernel(xq_ref, hq_ref, wq_ref, xk_ref, hk_ref, wk_ref, xv_ref, hv_ref, wv_ref, gate_ref,
                gcc_ref, bc_ref, gcr_ref, nw_ref, o_ref, s_ref, sq_ref, sk_ref, sv_ref, *, hb, rows, chunk):
    first = pl.program_id(1) == 0

    @pl.when(first)
    def _():
        s_ref[...] = jnp.zeros(s_ref.shape, F32)

    C = chunk
    ri = lax.broadcasted_iota(jnp.int32, (C, C), 0)
    ci = lax.broadcasted_iota(jnp.int32, (C, C), 1)
    lower = ri >= ci
    strict = ri > ci
    xor = ri ^ ci
    nw = nw_ref[...]
    n_levels = C.bit_length() - 1
    n_chunks = rows // C
    bodies = [(c, h) for c in range(n_chunks) for h in range(hb)]

    def rows_of(c):
        return slice(c * C, (c + 1) * C)

    def cols_of(h):
        return slice(h * HEAD_DIM, (h + 1) * HEAD_DIM)

    yq = _conv_silu(xq_ref, hq_ref, wq_ref, sq_ref, first)
    yk = _conv_silu(xk_ref, hk_ref, wk_ref, sk_ref, first)
    yv = _conv_silu(xv_ref, hv_ref, wv_ref, sv_ref, first)

    amats, rhs, qks, qds, tails, eglasts = {}, {}, {}, {}, {}, {}
    for b in bodies:
        c, h = b
        q = _l2n(yq[rows_of(c), cols_of(h)], HEAD_DIM ** -0.5)
        k = _l2n(yk[rows_of(c), cols_of(h)], 1.0)
        v = yv[rows_of(c), cols_of(h)]
        kt = k.T
        gcc = gcc_ref[rows_of(c), h:h + 1]
        bc = bc_ref[rows_of(c), h:h + 1]
        gcr = gcr_ref[h:h + 1, rows_of(c)]
        glast = gcr[:, C - 1:C]
        decay = jnp.exp(jnp.where(lower, gcc - gcr, -jnp.inf))
        egc = jnp.exp(gcc)
        kb = k * bc
        amats[b] = jnp.where(strict, _bdot(kb, kt) * decay, 0.0)
        rhs[b] = jnp.concatenate([v * bc, kb * egc], axis=1)
        qks[b] = jnp.where(lower, _bdot(q, kt) * decay, 0.0)
        qds[b] = q * egc
        tails[b] = kt * jnp.exp(glast - gcr)
        eglasts[b] = jnp.exp(glast)

    tinv = {b: jnp.where(ri == ci, 1.0, 0.0) - jnp.where(strict & (xor == 1), amats[b], 0.0) for b in bodies}
    for level in range(1, n_levels):
        join = strict & ((xor >> level) == 1)
        mids = {b: _bdot(jnp.where(join, amats[b], 0.0), tinv[b]) for b in bodies}
        for b in bodies:
            tinv[b] = tinv[b] - _bdot(tinv[b], mids[b])
    xs = {b: _bdot(tinv[b], rhs[b]) for b in bodies}

    states = [s_ref[h] for h in range(hb)]
    for c in range(n_chunks):
        ws = [_bdot(jnp.concatenate([xs[(c, h)][:, HEAD_DIM:], qds[(c, h)]], axis=0), states[h]) for h in range(hb)]
        v_new = [xs[(c, h)][:, :HEAD_DIM] - ws[h][:C] for h in range(hb)]
        outs = [ws[h][C:] + _bdot(qks[(c, h)], v_new[h]) for h in range(hb)]
        states = [states[h] * eglasts[(c, h)] + _bdot(tails[(c, h)], v_new[h]) for h in range(hb)]
        for h in range(hb):
            o = outs[h]
            o = o * lax.rsqrt(jnp.mean(o * o, axis=-1, keepdims=True) + NORM_EPS) * nw
            gate = gate_ref[rows_of(c), cols_of(h)].astype(F32)
            o_ref[rows_of(c), cols_of(h)] = (o * (gate * jax.nn.sigmoid(gate))).astype(o_ref.dtype)
    for h in range(hb):
        s_ref[h] = states[h]


def gdn_mixer(proj, qkv_col0, gate_col0, conv_w, gcc, bc, gcr, nw, *, heads, hb, rows, chunk):
    S = proj.shape[0]
    tc = hb * HEAD_DIM
    gw = heads * HEAD_DIM
    halo_blocks = rows // 16

    def section(col0, w_col0):
        cb, wb = col0 // tc, w_col0 // tc
        return [pl.BlockSpec((rows, tc), lambda g, r: (r, cb + g)),
                pl.BlockSpec((16, tc), lambda g, r: (jnp.maximum(r * halo_blocks - 1, 0), cb + g)),
                pl.BlockSpec((GDN_CONV, tc), lambda g, r: (0, wb + g))]

    in_specs = (section(qkv_col0, 0) + section(qkv_col0 + gw, gw) + section(qkv_col0 + 2 * gw, 2 * gw)
                + [pl.BlockSpec((rows, tc), lambda g, r: (r, gate_col0 // tc + g)),
                   pl.BlockSpec((None, rows, hb), lambda g, r: (g, r, 0)),
                   pl.BlockSpec((None, rows, hb), lambda g, r: (g, r, 0)),
                   pl.BlockSpec((None, hb, rows), lambda g, r: (g, 0, r)),
                   pl.BlockSpec((1, HEAD_DIM), lambda g, r: (0, 0))])
    stage = pltpu.VMEM((rows + 8, tc), F32)
    return pl.pallas_call(
        functools.partial(_gdn_kernel, hb=hb, rows=rows, chunk=chunk), grid=(heads // hb, S // rows),
        in_specs=in_specs,
        out_specs=pl.BlockSpec((rows, tc), lambda g, r: (r, g)),
        out_shape=jax.ShapeDtypeStruct((S, gw), BF16),
        scratch_shapes=[pltpu.VMEM((hb, HEAD_DIM, HEAD_DIM), F32), stage, stage, stage],
        compiler_params=_cparams(("parallel", "arbitrary")), name="gdn_mixer")(
            proj, proj, conv_w, proj, proj, conv_w, proj, proj, conv_w, proj, gcc, bc, gcr, nw.reshape(1, HEAD_DIM))


def _xattn_kernel(q_ref, kt_ref, v_ref, o_ref, *, heads):
    d = q_ref.shape[1] // heads
    cols = [slice(h * d, (h + 1) * d) for h in range(heads)]
    s = [jnp.dot(q_ref[:, c], kt_ref[c, :], preferred_element_type=F32) for c in cols]
    p = [jnp.exp2(sh - jnp.max(sh, axis=-1, keepdims=True)) for sh in s]
    l = [jnp.sum(ph, axis=-1, keepdims=True) for ph in p]
    o = [jnp.dot(ph.astype(BF16), v_ref[:, c], preferred_element_type=F32) for ph, c in zip(p, cols)]
    for c, oh, lh in zip(cols, o, l):
        o_ref[:, c] = (oh / lh).astype(o_ref.dtype)


def xattn_core(q, kt, v, *, heads, tq):
    S, W = q.shape
    M = v.shape[0]
    return pl.pallas_call(
        functools.partial(_xattn_kernel, heads=heads), grid=(S // tq,),
        in_specs=[pl.BlockSpec((tq, W), lambda i: (i, 0)),
                  pl.BlockSpec((W, M), lambda i: (0, 0)),
                  pl.BlockSpec((M, W), lambda i: (0, 0))],
        out_specs=pl.BlockSpec((tq, W), lambda i: (i, 0)),
        out_shape=jax.ShapeDtypeStruct((S, W), BF16),
        compiler_params=_cparams(("parallel",)), name="xattn_core")(q, kt, v)


def _block(x, mem, positions, attn_norm_w, w_in, mla_q_norm_w, mla_w_q_b, mla_kv_norm_w, mla_w_kv_b,
           gdn_conv_w, gdn_a_log, gdn_dt_bias, gdn_norm_w, w_out, xattn_norm_w, mem_norm_w,
           xattn_wq, xattn_wk, xattn_wv, xattn_wo, mlp_norm_w, mlp_w_up, mlp_w_down, final_norm_w):
    S, D = x.shape
    n_mix = w_out.shape[0]
    mla_heads = n_mix // 2 // HEAD_DIM
    gdn_heads = gdn_conv_w.shape[1] // (3 * HEAD_DIM)
    gw = gdn_heads * HEAD_DIM
    dq = MLA_NOPE + MLA_ROPE
    half = MLA_ROPE // 2

    o_kv = MLA_Q_RANK
    o_pe = o_kv + MLA_KV_RANK
    o_qkv = o_pe + MLA_ROPE
    o_a = o_qkv + 3 * gw
    o_b = o_a + gdn_heads
    o_gate = o_b + gdn_heads
    w_in_t = w_in.T
    c_qkv = o_pe
    c_gate = o_pe + 3 * gw
    n_proj = c_gate + (w_in.shape[1] - o_gate)
    tn_proj = 512

    skip1 = o_qkv - c_qkv
    skip2 = (o_gate - c_gate) - skip1
    unit = math.gcd(tn_proj, skip1, skip2)

    def proj_src_row(j):
        v = j * tn_proj
        past1 = (v >= c_qkv).astype(jnp.int32)
        past2 = (v >= c_gate).astype(jnp.int32)
        return (j * (tn_proj // unit) + past1 * (skip1 // unit) + past2 * (skip2 // unit)) * unit

    wq3 = mla_w_q_b.reshape(MLA_Q_RANK, mla_heads, dq)
    wq_perm = jnp.concatenate([wq3[:, :, :MLA_NOPE].reshape(MLA_Q_RANK, -1),
                               wq3[:, :, MLA_NOPE:MLA_NOPE + half].reshape(MLA_Q_RANK, -1),
                               wq3[:, :, MLA_NOPE + half:].reshape(MLA_Q_RANK, -1)], axis=1).astype(BF16)
    wkv3 = mla_w_kv_b.reshape(MLA_KV_RANK, mla_heads, MLA_NOPE + HEAD_DIM)
    wkv_perm = jnp.concatenate([wkv3[:, :, :MLA_NOPE].reshape(MLA_KV_RANK, -1),
                                wkv3[:, :, MLA_NOPE:].reshape(MLA_KV_RANK, -1)], axis=1).astype(BF16)

    inv_freq = ROPE_BASE ** (-jnp.arange(half, dtype=F32) / half)
    invf_row = jnp.tile(inv_freq, LANES // half).reshape(1, LANES)
    pad_l = jnp.zeros((MLA_ROPE,), F32)
    pad_r = jnp.zeros((LANES - MLA_ROPE - gdn_heads,), F32)
    alog_row = jnp.concatenate([pad_l, gdn_a_log.astype(F32), pad_r]).reshape(1, LANES)
    dtb_row = jnp.concatenate([pad_l, gdn_dt_bias.astype(F32), pad_r]).reshape(1, LANES)

    xn = rms_norm(x, attn_norm_w, BF16, 512)
    proj = proj_nt(xn, w_in_t, proj_src_row, n_proj, tm=1024, tn=tn_proj, name="proj_main")
    small = proj_small(xn, w_in_t, o_pe, MLA_ROPE, o_a, 2 * gdn_heads, tm=1024)
    aux, cos, sin = gates(small, positions.reshape(S, 1), invf_row, alog_row, dtb_row, 1024)

    t_att = 512
    qt = mla_q(proj, mla_q_norm_w, wq_perm, cos, sin, heads=mla_heads, tm=256, scale=dq ** -0.5 * LOG2E)
    k_full, vt4 = mla_kv(proj, o_kv // MLA_KV_RANK, mla_kv_norm_w, wkv_perm, aux, heads=mla_heads, tm=t_att)
    k4 = k_full.reshape(mla_heads, S // t_att, t_att, dq)
    mla_out = flash_attention(qt, k4, vt4, t=t_att, hp=2)

    hb = 4
    gc_full = chunk_cumsum(aux, GDN_CHUNK, 1024)
    gc = gc_full[:, MLA_ROPE:MLA_ROPE + gdn_heads]
    beta = aux[:, MLA_ROPE + gdn_heads:MLA_ROPE + 2 * gdn_heads]
    gcc = gc.reshape(S, gdn_heads // hb, hb).transpose(1, 0, 2)
    bc = beta.reshape(S, gdn_heads // hb, hb).transpose(1, 0, 2)
    gcr = gc.T.reshape(gdn_heads // hb, hb, S)
    gdn_out = gdn_mixer(proj, c_qkv, c_gate, gdn_conv_w, gcc, bc, gcr, gdn_norm_w,
                        heads=gdn_heads, hb=hb, rows=512, chunk=GDN_CHUNK)

    h1 = matmul2_res(mla_out, gdn_out, w_out, x, tm=1024, tn=512, name="w_out")

    hn = rms_norm(h1, xattn_norm_w, BF16, 512)
    memn = rms_norm(mem, mem_norm_w, BF16, 256)
    xd = D // XATTN_HEADS
    xq = matmul(hn, xattn_wq, tm=1024, tn=512, tk=D, out_dtype=BF16, out_scale=xd ** -0.5 * LOG2E, name="xattn_q")
    xk = matmul(memn, xattn_wk, tm=256, tn=1024, tk=1024, out_dtype=BF16, name="xattn_k")
    xv = matmul(memn, xattn_wv, tm=256, tn=1024, tk=1024, out_dtype=BF16, name="xattn_v")
    xo = xattn_core(xq, xk.T, xv, heads=XATTN_HEADS, tq=512)
    h2 = matmul(xo, xattn_wo, tm=1024, tn=512, tk=D, out_dtype=F32, res=h1, name="xattn_o")

    hn2 = rms_norm(h2, mlp_norm_w, BF16, 512)
    up = matmul(hn2, mlp_w_up, tm=1024, tn=512, tk=D, out_dtype=BF16, relu2=True, name="mlp_up")
    down = matmul_kacc(up, mlp_w_down, tm=2048, tn=1024, tk=1024, row_chunk=256, name="mlp_down")
    return rms_norm(h2, final_norm_w, F32, 256, add=down)


def kernel(x, mem, positions, attn_norm_w, w_in, mla_q_norm_w, mla_w_q_b, mla_kv_norm_w, mla_w_kv_b, gdn_conv_w, gdn_a_log, gdn_dt_bias, gdn_norm_w, w_out, xattn_norm_w, mem_norm_w, xattn_wq, xattn_wk, xattn_wv, xattn_wo, mlp_norm_w, mlp_w_up, mlp_w_down, final_norm_w):
    assert x.shape[0] == 1 and attn_norm_w.shape[0] == 1, "single sequence, single layer"
    out = _block(x[0], mem[0], positions[0], attn_norm_w[0], w_in[0], mla_q_norm_w[0], mla_w_q_b[0],
                 mla_kv_norm_w[0], mla_w_kv_b[0], gdn_conv_w[0], gdn_a_log[0], gdn_dt_bias[0], gdn_norm_w[0],
                 w_out[0], xattn_norm_w[0], mem_norm_w[0], xattn_wq[0], xattn_wk[0], xattn_wv[0], xattn_wo[0],
                 mlp_norm_w[0], mlp_w_up[0], mlp_w_down[0], final_norm_w)
    return out[None]
```

```python
import functools
import math

import jax
import jax.numpy as jnp
from jax import lax
from jax.experimental import pallas as pl
from jax.experimental.pallas import tpu as pltpu

F32 = jnp.float32
BF16 = jnp.bfloat16

HEAD_DIM = 128
MLA_NOPE = 128
MLA_ROPE = 64
MLA_Q_RANK = 1024
MLA_KV_RANK = 512
GDN_CONV = 4
XATTN_HEADS = 4
ROPE_BASE = 10000.0
NORM_EPS = 1e-6
L2_EPS = 1e-6
LOG2E = math.log2(math.e)

V7X_VMEM_BYTES = 64 * 1024 * 1024
VMEM_LIMIT = 52 * 1024 * 1024
LANES = 128

GDN_CHUNK = 128


def _cparams(sem):
    return pltpu.CompilerParams(dimension_semantics=sem, vmem_limit_bytes=VMEM_LIMIT)


def _rms_kernel(x_ref, w_ref, o_ref):
    x = x_ref[...].astype(F32)
    r = lax.rsqrt(jnp.mean(x * x, axis=-1, keepdims=True) + NORM_EPS)
    o_ref[...] = (x * r * w_ref[...]).astype(o_ref.dtype)


def _rms_add_kernel(x_ref, y_ref, w_ref, o_ref):
    x = x_ref[...] + y_ref[...]
    r = lax.rsqrt(jnp.mean(x * x, axis=-1, keepdims=True) + NORM_EPS)
    o_ref[...] = (x * r * w_ref[...]).astype(o_ref.dtype)


def rms_norm(x, w, out_dtype, tm, add=None):
    M, D = x.shape
    tm = min(tm, M)
    row = pl.BlockSpec((tm, D), lambda i: (i, 0))
    wspec = pl.BlockSpec((1, D), lambda i: (0, 0))
    if add is None:
        kern, specs, args = _rms_kernel, [row, wspec], (x, w.reshape(1, D))
    else:
        kern, specs, args = _rms_add_kernel, [row, row, wspec], (x, add, w.reshape(1, D))
    return pl.pallas_call(
        kern, grid=(M // tm,), in_specs=specs, out_specs=row,
        out_shape=jax.ShapeDtypeStruct((M, D), out_dtype),
        compiler_params=_cparams(("parallel",)), name="rms_norm")(*args)


def _mm_kernel(*refs, nk, relu2, has_res, out_scale, ssq_dim):
    x_ref, w_ref = refs[0], refs[1]
    res_ref = refs[2] if has_res else None
    ssq_ref = refs[2 + has_res] if ssq_dim else None
    o_ref = refs[2 + has_res + bool(ssq_dim)]

    def finish(r):
        if ssq_dim:
            r = r * lax.rsqrt(ssq_ref[:, 0:1] * (1.0 / ssq_dim) + NORM_EPS)
        if relu2:
            r = jnp.square(jnp.maximum(r, 0.0))
        if out_scale is not None:
            r = r * out_scale
        if has_res:
            r = r + res_ref[...]
        o_ref[...] = r.astype(o_ref.dtype)

    part = jnp.dot(x_ref[...], w_ref[...].astype(BF16), preferred_element_type=F32)
    if nk == 1:
        finish(part)
        return
    acc_ref = refs[3 + has_res + bool(ssq_dim)]
    k = pl.program_id(2)

    @pl.when(k == 0)
    def _():
        acc_ref[...] = part

    @pl.when(jnp.logical_and(k > 0, k < nk - 1))
    def _():
        acc_ref[...] += part

    @pl.when(k == nk - 1)
    def _():
        finish(acc_ref[...] + part)


def matmul(x, w, *, tm, tn, tk, out_dtype, relu2=False, res=None, out_scale=None, lhs_ssq=None, name="matmul"):
    M, K = x.shape
    K2, N = w.shape
    assert K == K2
    tm, tn, tk = min(tm, M), min(tn, N), min(tk, K)
    assert M % tm == 0 and N % tn == 0 and K % tk == 0
    nk = K // tk
    in_specs = [pl.BlockSpec((tm, tk), lambda i, j, k: (i, k)),
                pl.BlockSpec((tk, tn), lambda i, j, k: (k, j))]
    args = [x, w]
    if res is not None:
        in_specs.append(pl.BlockSpec((tm, tn), lambda i, j, k: (i, j)))
        args.append(res)
    if lhs_ssq is not None:
        in_specs.append(pl.BlockSpec((tm, LANES), lambda i, j, k: (i, 0)))
        args.append(lhs_ssq)
    scratch = [pltpu.VMEM((tm, tn), F32)] if nk > 1 else []
    kern = functools.partial(_mm_kernel, nk=nk, relu2=relu2, has_res=res is not None, out_scale=out_scale,
                             ssq_dim=K if lhs_ssq is not None else 0)
    return pl.pallas_call(
        kern, grid=(M // tm, N // tn, nk), in_specs=in_specs,
        out_specs=pl.BlockSpec((tm, tn), lambda i, j, k: (i, j)),
        out_shape=jax.ShapeDtypeStruct((M, N), out_dtype),
        scratch_shapes=scratch,
        compiler_params=_cparams(("parallel", "parallel", "arbitrary")), name=name)(*args)


_NT_DIMS = (((1,), (1,)), ((), ()))


def _proj_nt_kernel(x_ref, wt_ref, o_ref):
    wb = wt_ref[...].astype(BF16)
    o_ref[...] = lax.dot_general(x_ref[...], wb, _NT_DIMS, preferred_element_type=F32).astype(o_ref.dtype)


def proj_nt(x, wt, src_row_of_block, n_out, *, tm, tn, name):
    M, K = x.shape
    return pl.pallas_call(
        _proj_nt_kernel, grid=(M // tm, n_out // tn),
        in_specs=[pl.BlockSpec((tm, K), lambda i, j: (i, 0)),
                  pl.BlockSpec((pl.Element(tn), pl.Element(K)), lambda i, j: (src_row_of_block(j), 0))],
        out_specs=pl.BlockSpec((tm, tn), lambda i, j: (i, j)),
        out_shape=jax.ShapeDtypeStruct((M, n_out), BF16),
        compiler_params=_cparams(("parallel", "parallel")), name=name)(x, wt)


def _proj_small_kernel(x_ref, wa_ref, wb_ref, o_ref):
    wa = wa_ref[...].astype(BF16)
    wb = wb_ref[...].astype(BF16)
    pad = jnp.zeros((LANES - wa.shape[0] - wb.shape[0], wa.shape[1]), BF16)
    w = jnp.concatenate([wa, wb, pad], axis=0)
    o_ref[...] = lax.dot_general(x_ref[...], w, _NT_DIMS, preferred_element_type=F32)


def proj_small(x, wt, row_a, n_a, row_b, n_b, *, tm):
    M, K = x.shape
    return pl.pallas_call(
        _proj_small_kernel, grid=(M // tm,),
        in_specs=[pl.BlockSpec((tm, K), lambda i: (i, 0)),
                  pl.BlockSpec((pl.Element(n_a), pl.Element(K)), lambda i: (row_a, 0)),
                  pl.BlockSpec((pl.Element(n_b), pl.Element(K)), lambda i: (row_b, 0))],
        out_specs=pl.BlockSpec((tm, LANES), lambda i: (i, 0)),
        out_shape=jax.ShapeDtypeStruct((M, LANES), F32),
        compiler_params=_cparams(("parallel",)), name="proj_small")(x, wt, wt)


def _mm_res_kernel(*refs, n_x):
    x_refs = refs[:n_x]
    w_ref, res_ref, nw_ref, h_ref, hw_ref, ssq_ref = refs[n_x:]
    j = pl.program_id(1)
    wb = w_ref[...].astype(BF16)
    k0 = 0
    acc = res_ref[...]
    for x_ref in x_refs:
        kx = x_ref.shape[1]
        acc = acc + jnp.dot(x_ref[...], wb[k0:k0 + kx], preferred_element_type=F32)
        k0 += kx
    h_ref[...] = acc
    hw_ref[...] = (acc * nw_ref[...]).astype(hw_ref.dtype)
    part = jnp.broadcast_to(jnp.sum(acc * acc, axis=-1, keepdims=True), ssq_ref.shape)

    @pl.when(j == 0)
    def _():
        ssq_ref[...] = part

    @pl.when(j > 0)
    def _():
        ssq_ref[...] += part


def matmul_res_norm(xs, w, res, norm_w, *, tm, tn, name):
    M = xs[0].shape[0]
    N = w.shape[1]
    ktot = sum(x.shape[1] for x in xs)
    tile = pl.BlockSpec((tm, tn), lambda i, j: (i, j))
    return pl.pallas_call(
        functools.partial(_mm_res_kernel, n_x=len(xs)), grid=(M // tm, N // tn),
        in_specs=[pl.BlockSpec((tm, x.shape[1]), lambda i, j: (i, 0)) for x in xs]
        + [pl.BlockSpec((ktot, tn), lambda i, j: (0, j)), tile, pl.BlockSpec((1, tn), lambda i, j: (0, j))],
        out_specs=[tile, tile, pl.BlockSpec((tm, LANES), lambda i, j: (i, 0))],
        out_shape=[jax.ShapeDtypeStruct((M, N), F32), jax.ShapeDtypeStruct((M, N), BF16),
                   jax.ShapeDtypeStruct((M, LANES), F32)],
        compiler_params=_cparams(("parallel", "arbitrary")), name=name)(*xs, w, res, norm_w.reshape(1, N))


def _mm_kacc_kernel(x_ref, w_ref, o_ref, *, row_chunk):
    k = pl.program_id(2)
    wb = w_ref[...].astype(BF16)
    tm = x_ref.shape[0]

    @pl.when(k == 0)
    def _():
        for r in range(0, tm, row_chunk):
            rows = slice(r, r + row_chunk)
            o_ref[rows, :] = jnp.dot(x_ref[rows, :], wb, preferred_element_type=F32)

    @pl.when(k > 0)
    def _():
        for r in range(0, tm, row_chunk):
            rows = slice(r, r + row_chunk)
            o_ref[rows, :] += jnp.dot(x_ref[rows, :], wb, preferred_element_type=F32)


def matmul_kacc(x, w, *, tm, tn, tk, row_chunk, name):
    M, K = x.shape
    N = w.shape[1]
    tm, tn, tk = min(tm, M), min(tn, N), min(tk, K)
    assert M % tm == 0 and N % tn == 0 and K % tk == 0 and tm % row_chunk == 0
    return pl.pallas_call(
        functools.partial(_mm_kacc_kernel, row_chunk=row_chunk), grid=(M // tm, N // tn, K // tk),
        in_specs=[pl.BlockSpec((tm, tk), lambda i, j, k: (i, k)),
                  pl.BlockSpec((tk, tn), lambda i, j, k: (k, j))],
        out_specs=pl.BlockSpec((tm, tn), lambda i, j, k: (i, j)),
        out_shape=jax.ShapeDtypeStruct((M, N), F32),
        compiler_params=_cparams(("parallel", "parallel", "arbitrary")), name=name)(x, w)


def _gates_kernel(y_ref, pos_ref, invf_ref, alog_ref, dtb_ref, aux_ref, cos_ref, sin_ref):
    y = y_ref[...]
    ang = pos_ref[...].astype(F32) * invf_ref[...]
    c = jnp.cos(ang)
    s = jnp.sin(ang)
    cos_ref[...] = c
    sin_ref[...] = s
    lane = lax.broadcasted_iota(jnp.int32, y.shape, 1)
    half = MLA_ROPE // 2
    rot = jnp.where(lane < half, -pltpu.roll(y, LANES - half, 1), pltpu.roll(y, half, 1))
    roped = y * c + rot * s
    z = y + dtb_ref[...]
    softplus = jnp.maximum(z, 0.0) + jnp.log1p(jnp.exp(-jnp.abs(z)))
    g = -jnp.exp(alog_ref[...]) * softplus
    beta = jax.nn.sigmoid(y)
    aux_ref[...] = jnp.where(lane < MLA_ROPE, roped, jnp.where(lane < MLA_ROPE + 16, g, beta))


def gates(y, pos_col, invf_row, alog_row, dtb_row, tm):
    M = y.shape[0]
    tm = min(tm, M)
    row = pl.BlockSpec((tm, LANES), lambda i: (i, 0))
    one = pl.BlockSpec((1, LANES), lambda i: (0, 0))
    shp = jax.ShapeDtypeStruct((M, LANES), F32)
    return pl.pallas_call(
        _gates_kernel, grid=(M // tm,),
        in_specs=[row, pl.BlockSpec((tm, 1), lambda i: (i, 0)), one, one, one],
        out_specs=[row, row, row], out_shape=[shp, shp, shp],
        compiler_params=_cparams(("parallel",)), name="gates")(y, pos_col, invf_row, alog_row, dtb_row)


def _mla_q_kernel(cq_ref, nw_ref, wq_ref, cos_ref, sin_ref, o_ref, *, heads, scale):
    x = cq_ref[...].astype(F32)
    r = lax.rsqrt(jnp.mean(x * x, axis=-1, keepdims=True) + NORM_EPS)
    xn = (x * r * nw_ref[...]).astype(BF16)
    q = jnp.dot(xn, wq_ref[...], preferred_element_type=F32) * scale
    half = MLA_ROPE // 2
    n_nope = heads * MLA_NOPE
    n_half = heads * half
    reps = n_half // LANES
    c = jnp.concatenate([cos_ref[...]] * reps, axis=1)
    s = jnp.concatenate([sin_ref[...]] * reps, axis=1)
    p1 = q[:, n_nope:n_nope + n_half]
    p2 = q[:, n_nope + n_half:]
    r1t = (p1 * c - p2 * s).T.astype(BF16)
    r2t = (p2 * c + p1 * s).T.astype(BF16)
    for h in range(heads):
        o_ref[h, 0:MLA_NOPE, :] = q[:, h * MLA_NOPE:(h + 1) * MLA_NOPE].T.astype(BF16)
        o_ref[h, MLA_NOPE:MLA_NOPE + half, :] = r1t[h * half:(h + 1) * half, :]
        o_ref[h, MLA_NOPE + half:MLA_NOPE + 2 * half, :] = r2t[h * half:(h + 1) * half, :]


def mla_q(proj, nw, wq, cos, sin, *, heads, tm, scale):
    S = proj.shape[0]
    R = nw.shape[0]
    dq = MLA_NOPE + MLA_ROPE
    kern = functools.partial(_mla_q_kernel, heads=heads, scale=scale)
    return pl.pallas_call(
        kern, grid=(S // tm,),
        in_specs=[pl.BlockSpec((tm, R), lambda i: (i, 0)),
                  pl.BlockSpec((1, R), lambda i: (0, 0)),
                  pl.BlockSpec(wq.shape, lambda i: (0, 0)),
                  pl.BlockSpec((tm, LANES), lambda i: (i, 0)),
                  pl.BlockSpec((tm, LANES), lambda i: (i, 0))],
        out_specs=pl.BlockSpec((heads, dq, tm), lambda i: (0, 0, i)),
        out_shape=jax.ShapeDtypeStruct((heads, dq, S), BF16),
        compiler_params=_cparams(("parallel",)), name="mla_q")(proj, nw.reshape(1, R), wq, cos, sin)


def _mla_kv_kernel(ckv_ref, nw_ref, wkv_ref, aux_ref, k_ref, vt_ref, *, heads):
    x = ckv_ref[...].astype(F32)
    r = lax.rsqrt(jnp.mean(x * x, axis=-1, keepdims=True) + NORM_EPS)
    xn = (x * r * nw_ref[...]).astype(BF16)
    kv = jnp.dot(xn, wkv_ref[...], preferred_element_type=F32)
    kpe = aux_ref[...][:, 0:MLA_ROPE].astype(BF16)
    n_nope = heads * MLA_NOPE
    for h in range(heads):
        k_ref[h, :, 0:MLA_NOPE] = kv[:, h * MLA_NOPE:(h + 1) * MLA_NOPE].astype(BF16)
        k_ref[h, :, MLA_NOPE:MLA_NOPE + MLA_ROPE] = kpe
        vt_ref[h, 0] = kv[:, n_nope + h * HEAD_DIM:n_nope + (h + 1) * HEAD_DIM].T.astype(BF16)


def mla_kv(proj, col_block, nw, wkv, aux, *, heads, tm):
    S = proj.shape[0]
    R = nw.shape[0]
    dq = MLA_NOPE + MLA_ROPE
    kern = functools.partial(_mla_kv_kernel, heads=heads)
    return pl.pallas_call(
        kern, grid=(S // tm,),
        in_specs=[pl.BlockSpec((tm, R), lambda i: (i, col_block)),
                  pl.BlockSpec((1, R), lambda i: (0, 0)),
                  pl.BlockSpec(wkv.shape, lambda i: (0, 0)),
                  pl.BlockSpec((tm, LANES), lambda i: (i, 0))],
        out_specs=[pl.BlockSpec((heads, tm, dq), lambda i: (0, i, 0)),
                   pl.BlockSpec((heads, 1, HEAD_DIM, tm), lambda i: (0, i, 0, 0))],
        out_shape=[jax.ShapeDtypeStruct((heads, S, dq), BF16),
                   jax.ShapeDtypeStruct((heads, S // tm, HEAD_DIM, tm), BF16)],
        compiler_params=_cparams(("parallel",)), name="mla_kv")(proj, nw.reshape(1, R), wkv, aux)


def _flash_kernel(qt_ref, k_ref, vt_ref, o_ref, acc_ref, m_ref, l_ref, sa_ref, sb_ref, *, t, hp):
    i = pl.program_id(1)
    m_ref[...] = jnp.full(m_ref.shape, -jnp.inf, F32)
    l_ref[...] = jnp.zeros(l_ref.shape, F32)
    acc_ref[...] = jnp.zeros(acc_ref.shape, F32)
    heads = range(hp)

    def scores(j, s_ref):
        for h in heads:
            s_ref[h] = jnp.dot(k_ref[h, j], qt_ref[h], preferred_element_type=F32)

    def softmax_pv(j, s_ref, masked):
        s = [s_ref[h] for h in heads]
        if masked:
            kpos = lax.broadcasted_iota(jnp.int32, (t, t), 0)
            qpos = lax.broadcasted_iota(jnp.int32, (t, t), 1)
            s = [jnp.where(kpos <= qpos, sh, -jnp.inf) for sh in s]
        m_old = [m_ref[h] for h in heads]
        m_new = [jnp.maximum(m_old[h], jnp.max(s[h], axis=0, keepdims=True)) for h in heads]
        alpha = [jnp.exp2(m_old[h] - m_new[h]) for h in heads]
        p = [jnp.exp2(s[h] - m_new[h]) for h in heads]
        for h in heads:
            l_ref[h] = alpha[h] * l_ref[h] + jnp.sum(p[h], axis=0, keepdims=True)
            m_ref[h] = m_new[h]
        pv = [jnp.dot(vt_ref[h, j], p[h].astype(BF16), preferred_element_type=F32) for h in heads]
        for h in heads:
            acc_ref[h] = alpha[h] * acc_ref[h] + pv[h]

    scores(0, sa_ref)

    def body(jj, carry):
        j = 2 * jj
        scores(j + 1, sb_ref)
        softmax_pv(j, sa_ref, False)
        scores(j + 2, sa_ref)
        softmax_pv(j + 1, sb_ref, False)
        return carry

    lax.fori_loop(0, i // 2, body, 0)

    @pl.when(i % 2 == 0)
    def _():
        softmax_pv(i, sa_ref, True)

    @pl.when(i % 2 == 1)
    def _():
        scores(i, sb_ref)
        softmax_pv(i - 1, sa_ref, False)
        softmax_pv(i, sb_ref, True)

    for h in range(hp):
        o_ref[:, h * HEAD_DIM:(h + 1) * HEAD_DIM] = (acc_ref[h] / l_ref[h]).T.astype(o_ref.dtype)


def flash_attention(qt, k4, vt4, *, t, hp):
    H, dq, S = qt.shape
    nk = S // t
    kern = functools.partial(_flash_kernel, t=t, hp=hp)
    return pl.pallas_call(
        kern, grid=(H // hp, nk),
        in_specs=[pl.BlockSpec((hp, dq, t), lambda g, i: (g, 0, i)),
                  pl.BlockSpec((hp, nk, t, dq), lambda g, i: (g, 0, 0, 0)),
                  pl.BlockSpec((hp, nk, HEAD_DIM, t), lambda g, i: (g, 0, 0, 0))],
        out_specs=pl.BlockSpec((t, hp * HEAD_DIM), lambda g, i: (i, g)),
        out_shape=jax.ShapeDtypeStruct((S, H * HEAD_DIM), BF16),
        scratch_shapes=[pltpu.VMEM((hp, HEAD_DIM, t), F32), pltpu.VMEM((hp, 1, t), F32),
                        pltpu.VMEM((hp, 1, t), F32), pltpu.VMEM((hp, t, t), F32), pltpu.VMEM((hp, t, t), F32)],
        compiler_params=_cparams(("parallel", "arbitrary")), name="mla_flash")(qt, k4, vt4)


def _cumsum_kernel(g_ref, o_ref, *, chunk):
    r = lax.broadcasted_iota(jnp.int32, (chunk, chunk), 0)
    c = lax.broadcasted_iota(jnp.int32, (chunk, chunk), 1)
    trilb = jnp.where(r >= c, 1.0, 0.0).astype(BF16)
    for r0 in range(0, g_ref.shape[0], chunk):
        g = g_ref[r0:r0 + chunk, :]
        g_hi = g.astype(BF16).astype(F32)
        g_mid = (g - g_hi).astype(BF16).astype(F32)
        g_lo = g - g_hi - g_mid
        o_ref[r0:r0 + chunk, :] = (jnp.dot(trilb, g_hi.astype(BF16), preferred_element_type=F32)
                                   + jnp.dot(trilb, g_mid.astype(BF16), preferred_element_type=F32)
                                   + jnp.dot(trilb, g_lo.astype(BF16), preferred_element_type=F32))


def chunk_cumsum(g, chunk, tm):
    M = g.shape[0]
    tm = min(tm, M)
    row = pl.BlockSpec((tm, LANES), lambda i: (i, 0))
    return pl.pallas_call(
        functools.partial(_cumsum_kernel, chunk=chunk), grid=(M // tm,), in_specs=[row], out_specs=row,
        out_shape=jax.ShapeDtypeStruct((M, LANES), F32),
        compiler_params=_cparams(("parallel",)), name="gdn_cumsum")(g)


def _bdot(a, b):
    return jnp.dot(a.astype(BF16), b.astype(BF16), preferred_element_type=F32)


def _conv_silu(x_ref, halo_ref, w_ref, xs_ref, first):
    rows = x_ref.shape[0]
    x = x_ref[...].astype(F32)
    xs_ref[0:8, :] = jnp.where(first, 0.0, halo_ref[...].astype(F32)[8:16, :])
    xs_ref[8:, :] = x
    w = w_ref[...]
    acc = x * w[GDN_CONV - 1:GDN_CONV, :]
    for s in range(1, GDN_CONV):
        acc = acc + xs_ref[8 - s:8 - s + rows, :] * w[GDN_CONV - 1 - s:GDN_CONV - s, :]
    return acc * jax.nn.sigmoid(acc)


def _l2n(y, scale):
    return y * (lax.rsqrt(jnp.sum(y * y, axis=-1, keepdims=True) + L2_EPS) * scale)


def _gdn_kernel(xq_ref, hq_ref, wq_ref, xk_ref, hk_ref, wk_ref, xv_ref, hv_ref, wv_ref, gate_ref,
                gcc_ref, bc_ref, gcr_ref, nw_ref, o_ref, s_ref, sq_ref, sk_ref, sv_ref, *, hb, rows, chunk):
    first = pl.program_id(1) == 0

    @pl.when(first)
    def _():
        s_ref[...] = jnp.zeros(s_ref.shape, F32)

    C = chunk
    ri = lax.broadcasted_iota(jnp.int32, (C, C), 0)
    ci = lax.broadcasted_iota(jnp.int32, (C, C), 1)
    lower = ri >= ci
    strict = ri > ci
    xor = ri ^ ci
    nw = nw_ref[...]
    n_levels = C.bit_length() - 1
    n_chunks = rows // C
    bodies = [(c, h) for c in range(n_chunks) for h in range(hb)]

    def rows_of(c):
        return slice(c * C, (c + 1) * C)

    def cols_of(h):
        return slice(h * HEAD_DIM, (h + 1) * HEAD_DIM)

    yq = _conv_silu(xq_ref, hq_ref, wq_ref, sq_ref, first)
    yk = _conv_silu(xk_ref, hk_ref, wk_ref, sk_ref, first)
    yv = _conv_silu(xv_ref, hv_ref, wv_ref, sv_ref, first)

    amats, rhs, qks, qds, tails, eglasts = {}, {}, {}, {}, {}, {}
    for b in bodies:
        c, h = b
        q = _l2n(yq[rows_of(c), cols_of(h)], HEAD_DIM ** -0.5)
        k = _l2n(yk[rows_of(c), cols_of(h)], 1.0)
        v = yv[rows_of(c), cols_of(h)]
        kt = k.T
        gcc = gcc_ref[rows_of(c), h:h + 1]
        bc = bc_ref[rows_of(c), h:h + 1]
        gcr = gcr_ref[h:h + 1, rows_of(c)]
        glast = gcr[:, C - 1:C]
        decay = jnp.exp(jnp.where(lower, gcc - gcr, -jnp.inf))
        egc = jnp.exp(gcc)
        kb = k * bc
        amats[b] = jnp.where(strict, _bdot(kb, kt) * decay, 0.0)
        rhs[b] = jnp.concatenate([v * bc, kb * egc], axis=1)
        qks[b] = jnp.where(lower, _bdot(q, kt) * decay, 0.0)
        qds[b] = q * egc
        tails[b] = kt * jnp.exp(glast - gcr)
        eglasts[b] = jnp.exp(glast)

    tinv = {b: jnp.where(ri == ci, 1.0, 0.0) - jnp.where(strict & (xor == 1), amats[b], 0.0) for b in bodies}
    for level in range(1, n_levels):
        join = strict & ((xor >> level) == 1)
        mids = {b: _bdot(jnp.where(join, amats[b], 0.0), tinv[b]) for b in bodies}
        for b in bodies:
            tinv[b] = tinv[b] - _bdot(tinv[b], mids[b])
    xs = {b: _bdot(tinv[b], rhs[b]) for b in bodies}

    states = [s_ref[h] for h in range(hb)]
    for c in range(n_chunks):
        ws = [_bdot(jnp.concatenate([xs[(c, h)][:, HEAD_DIM:], qds[(c, h)]], axis=0), states[h]) for h in range(hb)]
        v_new = [xs[(c, h)][:, :HEAD_DIM] - ws[h][:C] for h in range(hb)]
        outs = [ws[h][C:] + _bdot(qks[(c, h)], v_new[h]) for h in range(hb)]
        states = [states[h] * eglasts[(c, h)] + _bdot(tails[(c, h)], v_new[h]) for h in range(hb)]
        for h in range(hb):
            o = outs[h]
            o = o * lax.rsqrt(jnp.mean(o * o, axis=-1, keepdims=True) + NORM_EPS) * nw
            gate = gate_ref[rows_of(c), cols_of(h)].astype(F32)
            o_ref[rows_of(c), cols_of(h)] = (o * (gate * jax.nn.sigmoid(gate))).astype(o_ref.dtype)
    for h in range(hb):
        s_ref[h] = states[h]


def gdn_mixer(proj, qkv_col0, gate_col0, conv_w, gcc, bc, gcr, nw, *, heads, hb, rows, chunk):
    S = proj.shape[0]
    tc = hb * HEAD_DIM
    gw = heads * HEAD_DIM
    halo_blocks = rows // 16

    def section(col0, w_col0):
        cb, wb = col0 // tc, w_col0 // tc
        return [pl.BlockSpec((rows, tc), lambda g, r: (r, cb + g)),
                pl.BlockSpec((16, tc), lambda g, r: (jnp.maximum(r * halo_blocks - 1, 0), cb + g)),
                pl.BlockSpec((GDN_CONV, tc), lambda g, r: (0, wb + g))]

    in_specs = (section(qkv_col0, 0) + section(qkv_col0 + gw, gw) + section(qkv_col0 + 2 * gw, 2 * gw)
                + [pl.BlockSpec((rows, tc), lambda g, r: (r, gate_col0 // tc + g)),
                   pl.BlockSpec((None, rows, hb), lambda g, r: (g, r, 0)),
                   pl.BlockSpec((None, rows, hb), lambda g, r: (g, r, 0)),
                   pl.BlockSpec((None, hb, rows), lambda g, r: (g, 0, r)),
                   pl.BlockSpec((1, HEAD_DIM), lambda g, r: (0, 0))])
    stage = pltpu.VMEM((rows + 8, tc), F32)
    return pl.pallas_call(
        functools.partial(_gdn_kernel, hb=hb, rows=rows, chunk=chunk), grid=(heads // hb, S // rows),
        in_specs=in_specs,
        out_specs=pl.BlockSpec((rows, tc), lambda g, r: (r, g)),
        out_shape=jax.ShapeDtypeStruct((S, gw), BF16),
        scratch_shapes=[pltpu.VMEM((hb, HEAD_DIM, HEAD_DIM), F32), stage, stage, stage],
        compiler_params=_cparams(("parallel", "arbitrary")), name="gdn_mixer")(
            proj, proj, conv_w, proj, proj, conv_w, proj, proj, conv_w, proj, gcc, bc, gcr, nw.reshape(1, HEAD_DIM))


def _xattn_kernel(q_ref, kt_ref, v_ref, o_ref, *, heads):
    d = q_ref.shape[1] // heads
    cols = [slice(h * d, (h + 1) * d) for h in range(heads)]
    s = [jnp.dot(q_ref[:, c], kt_ref[c, :], preferred_element_type=F32) for c in cols]
    p = [jnp.exp2(sh - jnp.max(sh, axis=-1, keepdims=True)) for sh in s]
    l = [jnp.sum(ph, axis=-1, keepdims=True) for ph in p]
    o = [jnp.dot(ph.astype(BF16), v_ref[:, c], preferred_element_type=F32) for ph, c in zip(p, cols)]
    for c, oh, lh in zip(cols, o, l):
        o_ref[:, c] = (oh / lh).astype(o_ref.dtype)


def xattn_core(q, kt, v, *, heads, tq):
    S, W = q.shape
    M = v.shape[0]
    return pl.pallas_call(
        functools.partial(_xattn_kernel, heads=heads), grid=(S // tq,),
        in_specs=[pl.BlockSpec((tq, W), lambda i: (i, 0)),
                  pl.BlockSpec((W, M), lambda i: (0, 0)),
                  pl.BlockSpec((M, W), lambda i: (0, 0))],
        out_specs=pl.BlockSpec((tq, W), lambda i: (i, 0)),
        out_shape=jax.ShapeDtypeStruct((S, W), BF16),
        compiler_params=_cparams(("parallel",)), name="xattn_core")(q, kt, v)


def _block(x, mem, positions, attn_norm_w, w_in, mla_q_norm_w, mla_w_q_b, mla_kv_norm_w, mla_w_kv_b,
           gdn_conv_w, gdn_a_log, gdn_dt_bias, gdn_norm_w, w_out, xattn_norm_w, mem_norm_w,
           xattn_wq, xattn_wk, xattn_wv, xattn_wo, mlp_norm_w, mlp_w_up, mlp_w_down, final_norm_w):
    S, D = x.shape
    n_mix = w_out.shape[0]
    mla_heads = n_mix // 2 // HEAD_DIM
    gdn_heads = gdn_conv_w.shape[1] // (3 * HEAD_DIM)
    gw = gdn_heads * HEAD_DIM
    dq = MLA_NOPE + MLA_ROPE
    half = MLA_ROPE // 2

    o_kv = MLA_Q_RANK
    o_pe = o_kv + MLA_KV_RANK
    o_qkv = o_pe + MLA_ROPE
    o_a = o_qkv + 3 * gw
    o_b = o_a + gdn_heads
    o_gate = o_b + gdn_heads
    w_in_t = w_in.T
    c_qkv = o_pe
    c_gate = o_pe + 3 * gw
    n_proj = c_gate + (w_in.shape[1] - o_gate)
    tn_proj = 512

    skip1 = o_qkv - c_qkv
    skip2 = (o_gate - c_gate) - skip1
    unit = math.gcd(tn_proj, skip1, skip2)

    def proj_src_row(j):
        v = j * tn_proj
        past1 = (v >= c_qkv).astype(jnp.int32)
        past2 = (v >= c_gate).astype(jnp.int32)
        return (j * (tn_proj // unit) + past1 * (skip1 // unit) + past2 * (skip2 // unit)) * unit

    wq3 = mla_w_q_b.reshape(MLA_Q_RANK, mla_heads, dq)
    wq_perm = jnp.concatenate([wq3[:, :, :MLA_NOPE].reshape(MLA_Q_RANK, -1),
                               wq3[:, :, MLA_NOPE:MLA_NOPE + half].reshape(MLA_Q_RANK, -1),
                               wq3[:, :, MLA_NOPE + half:].reshape(MLA_Q_RANK, -1)], axis=1).astype(BF16)
    wkv3 = mla_w_kv_b.reshape(MLA_KV_RANK, mla_heads, MLA_NOPE + HEAD_DIM)
    wkv_perm = jnp.concatenate([wkv3[:, :, :MLA_NOPE].reshape(MLA_KV_RANK, -1),
                                wkv3[:, :, MLA_NOPE:].reshape(MLA_KV_RANK, -1)], axis=1).astype(BF16)

    inv_freq = ROPE_BASE ** (-jnp.arange(half, dtype=F32) / half)
    invf_row = jnp.tile(inv_freq, LANES // half).reshape(1, LANES)
    pad_l = jnp.zeros((MLA_ROPE,), F32)
    pad_r = jnp.zeros((LANES - MLA_ROPE - gdn_heads,), F32)
    alog_row = jnp.concatenate([pad_l, gdn_a_log.astype(F32), pad_r]).reshape(1, LANES)
    dtb_row = jnp.concatenate([pad_l, gdn_dt_bias.astype(F32), pad_r]).reshape(1, LANES)

    xn = rms_norm(x, attn_norm_w, BF16, 512)
    proj = proj_nt(xn, w_in_t, proj_src_row, n_proj, tm=1024, tn=tn_proj, name="proj_main")
    small = proj_small(xn, w_in_t, o_pe, MLA_ROPE, o_a, 2 * gdn_heads, tm=1024)
    aux, cos, sin = gates(small, positions.reshape(S, 1), invf_row, alog_row, dtb_row, 1024)

    t_att = 512
    qt = mla_q(proj, mla_q_norm_w, wq_perm, cos, sin, heads=mla_heads, tm=256, scale=dq ** -0.5 * LOG2E)
    k_full, vt4 = mla_kv(proj, o_kv // MLA_KV_RANK, mla_kv_norm_w, wkv_perm, aux, heads=mla_heads, tm=t_att)
    k4 = k_full.reshape(mla_heads, S // t_att, t_att, dq)
    mla_out = flash_attention(qt, k4, vt4, t=t_att, hp=2)

    hb = 4
    gc_full = chunk_cumsum(aux, GDN_CHUNK, 1024)
    gc = gc_full[:, MLA_ROPE:MLA_ROPE + gdn_heads]
    beta = aux[:, MLA_ROPE + gdn_heads:MLA_ROPE + 2 * gdn_heads]
    gcc = gc.reshape(S, gdn_heads // hb, hb).transpose(1, 0, 2)
    bc = beta.reshape(S, gdn_heads // hb, hb).transpose(1, 0, 2)
    gcr = gc.T.reshape(gdn_heads // hb, hb, S)
    gdn_out = gdn_mixer(proj, c_qkv, c_gate, gdn_conv_w, gcc, bc, gcr, gdn_norm_w,
                        heads=gdn_heads, hb=hb, rows=512, chunk=GDN_CHUNK)

    h1, h1w, h1_ssq = matmul_res_norm([mla_out, gdn_out], w_out, x, xattn_norm_w, tm=1024, tn=512, name="w_out")

    memn = rms_norm(mem, mem_norm_w, BF16, 256)
    xd = D // XATTN_HEADS
    xq = matmul(h1w, xattn_wq, tm=1024, tn=512, tk=D, out_dtype=BF16, out_scale=xd ** -0.5 * LOG2E,
                lhs_ssq=h1_ssq, name="xattn_q")
    xk = matmul(memn, xattn_wk, tm=256, tn=1024, tk=1024, out_dtype=BF16, name="xattn_k")
    xv = matmul(memn, xattn_wv, tm=256, tn=1024, tk=1024, out_dtype=BF16, name="xattn_v")
    xo = xattn_core(xq, xk.T, xv, heads=XATTN_HEADS, tq=512)
    h2, h2w, h2_ssq = matmul_res_norm([xo], xattn_wo, h1, mlp_norm_w, tm=1024, tn=512, name="xattn_o")

    up = matmul(h2w, mlp_w_up, tm=1024, tn=512, tk=D, out_dtype=BF16, relu2=True, lhs_ssq=h2_ssq, name="mlp_up")
    down = matmul_kacc(up, mlp_w_down, tm=2048, tn=1024, tk=1024, row_chunk=256, name="mlp_down")
    return rms_norm(h2, final_norm_w, F32, 256, add=down)


def kernel(x, mem, positions, attn_norm_w, w_in, mla_q_norm_w, mla_w_q_b, mla_kv_norm_w, mla_w_kv_b, gdn_conv_w, gdn_a_log, gdn_dt_bias, gdn_norm_w, w_out, xattn_norm_w, mem_norm_w, xattn_wq, xattn_wk, xattn_wv, xattn_wo, mlp_norm_w, mlp_w_up, mlp_w_down, final_norm_w):
    assert x.shape[0] == 1 and attn_norm_w.shape[0] == 1, "single sequence, single layer"
    out = _block(x[0], mem[0], positions[0], attn_norm_w[0], w_in[0], mla_q_norm_w[0], mla_w_q_b[0],
                 mla_kv_norm_w[0], mla_w_kv_b[0], gdn_conv_w[0], gdn_a_log[0], gdn_dt_bias[0], gdn_norm_w[0],
                 w_out[0], xattn_norm_w[0], mem_norm_w[0], xattn_wq[0], xattn_wk[0], xattn_wv[0], xattn_wo[0],
                 mlp_norm_w[0], mlp_w_up[0], mlp_w_down[0], final_norm_w)
    return out[None]
```

```python
import functools
import math

import jax
import jax.numpy as jnp
from jax import lax
from jax.experimental import pallas as pl
from jax.experimental.pallas import tpu as pltpu

F32 = jnp.float32
BF16 = jnp.bfloat16

HEAD_DIM = 128
MLA_NOPE = 128
MLA_ROPE = 64
MLA_Q_RANK = 1024
MLA_KV_RANK = 512
GDN_CONV = 4
XATTN_HEADS = 4
ROPE_BASE = 10000.0
NORM_EPS = 1e-6
L2_EPS = 1e-6
LOG2E = math.log2(math.e)

V7X_VMEM_BYTES = 64 * 1024 * 1024
VMEM_LIMIT = 58 * 1024 * 1024
LANES = 128
MM_ROW_CHUNK = 2048

GDN_CHUNK = 128


def _cparams(sem):
    return pltpu.CompilerParams(dimension_semantics=sem, vmem_limit_bytes=VMEM_LIMIT)


def _rms_kernel(x_ref, w_ref, o_ref):
    x = x_ref[...].astype(F32)
    r = lax.rsqrt(jnp.mean(x * x, axis=-1, keepdims=True) + NORM_EPS)
    o_ref[...] = (x * r * w_ref[...]).astype(o_ref.dtype)


def _rms_add_kernel(x_ref, y_ref, w_ref, o_ref):
    x = x_ref[...] + y_ref[...]
    r = lax.rsqrt(jnp.mean(x * x, axis=-1, keepdims=True) + NORM_EPS)
    o_ref[...] = (x * r * w_ref[...]).astype(o_ref.dtype)


def rms_norm(x, w, out_dtype, tm, add=None):
    M, D = x.shape
    tm = min(tm, M)
    row = pl.BlockSpec((tm, D), lambda i: (i, 0))
    wspec = pl.BlockSpec((1, D), lambda i: (0, 0))
    if add is None:
        kern, specs, args = _rms_kernel, [row, wspec], (x, w.reshape(1, D))
    else:
        kern, specs, args = _rms_add_kernel, [row, row, wspec], (x, add, w.reshape(1, D))
    return pl.pallas_call(
        kern, grid=(M // tm,), in_specs=specs, out_specs=row,
        out_shape=jax.ShapeDtypeStruct((M, D), out_dtype),
        compiler_params=_cparams(("parallel",)), name="rms_norm")(*args)


def _mm_kernel(*refs, relu2, has_res, out_scale, ssq_dim, row_chunk):
    x_ref, w_ref = refs[0], refs[1]
    res_ref = refs[2] if has_res else None
    ssq_ref = refs[2 + has_res] if ssq_dim else None
    o_ref = refs[2 + has_res + bool(ssq_dim)]
    wb = w_ref[...].astype(BF16)
    for r0 in range(0, x_ref.shape[0], row_chunk):
        rows = slice(r0, r0 + row_chunk)
        r = jnp.dot(x_ref[rows, :], wb, preferred_element_type=F32)
        if ssq_dim:
            r = r * lax.rsqrt(ssq_ref[rows, 0:1] * (1.0 / ssq_dim) + NORM_EPS)
        if relu2:
            r = jnp.square(jnp.maximum(r, 0.0))
        if out_scale is not None:
            r = r * out_scale
        if has_res:
            r = r + res_ref[rows, :]
        o_ref[rows, :] = r.astype(o_ref.dtype)


def matmul(x, w, *, tm, tn, out_dtype, relu2=False, res=None, out_scale=None, lhs_ssq=None, row_chunk=MM_ROW_CHUNK,
           x_buffers=2, name="matmul"):
    M, K = x.shape
    K2, N = w.shape
    assert K == K2
    tm, tn = min(tm, M), min(tn, N)
    row_chunk = min(row_chunk, tm)
    assert M % tm == 0 and N % tn == 0 and tm % row_chunk == 0
    in_specs = [pl.BlockSpec((tm, K), lambda i, j: (i, 0), pipeline_mode=pl.Buffered(x_buffers)),
                pl.BlockSpec((K, tn), lambda i, j: (0, j))]
    args = [x, w]
    if res is not None:
        in_specs.append(pl.BlockSpec((tm, tn), lambda i, j: (i, j)))
        args.append(res)
    if lhs_ssq is not None:
        in_specs.append(pl.BlockSpec((tm, LANES), lambda i, j: (i, 0)))
        args.append(lhs_ssq)
    kern = functools.partial(_mm_kernel, relu2=relu2, has_res=res is not None, out_scale=out_scale,
                             ssq_dim=K if lhs_ssq is not None else 0, row_chunk=row_chunk)
    return pl.pallas_call(
        kern, grid=(M // tm, N // tn), in_specs=in_specs,
        out_specs=pl.BlockSpec((tm, tn), lambda i, j: (i, j)),
        out_shape=jax.ShapeDtypeStruct((M, N), out_dtype),
        compiler_params=_cparams(("parallel", "parallel")), name=name)(*args)


_NT_DIMS = (((1,), (1,)), ((), ()))


def _proj_nt_kernel(x_ref, wt_ref, o_ref):
    wb = wt_ref[...].astype(BF16)
    o_ref[...] = lax.dot_general(x_ref[...], wb, _NT_DIMS, preferred_element_type=F32).astype(o_ref.dtype)


def proj_nt(x, wt, src_row_of_block, n_out, *, tm, tn, name):
    M, K = x.shape
    return pl.pallas_call(
        _proj_nt_kernel, grid=(M // tm, n_out // tn),
        in_specs=[pl.BlockSpec((tm, K), lambda i, j: (i, 0)),
                  pl.BlockSpec((pl.Element(tn), pl.Element(K)), lambda i, j: (src_row_of_block(j), 0))],
        out_specs=pl.BlockSpec((tm, tn), lambda i, j: (i, j)),
        out_shape=jax.ShapeDtypeStruct((M, n_out), BF16),
        compiler_params=_cparams(("parallel", "parallel")), name=name)(x, wt)


def _proj_small_kernel(x_ref, wa_ref, wb_ref, o_ref):
    wa = wa_ref[...].astype(BF16)
    wb = wb_ref[...].astype(BF16)
    pad = jnp.zeros((LANES - wa.shape[0] - wb.shape[0], wa.shape[1]), BF16)
    w = jnp.concatenate([wa, wb, pad], axis=0)
    o_ref[...] = lax.dot_general(x_ref[...], w, _NT_DIMS, preferred_element_type=F32)


def proj_small(x, wt, row_a, n_a, row_b, n_b, *, tm):
    M, K = x.shape
    return pl.pallas_call(
        _proj_small_kernel, grid=(M // tm,),
        in_specs=[pl.BlockSpec((tm, K), lambda i: (i, 0)),
                  pl.BlockSpec((pl.Element(n_a), pl.Element(K)), lambda i: (row_a, 0)),
                  pl.BlockSpec((pl.Element(n_b), pl.Element(K)), lambda i: (row_b, 0))],
        out_specs=pl.BlockSpec((tm, LANES), lambda i: (i, 0)),
        out_shape=jax.ShapeDtypeStruct((M, LANES), F32),
        compiler_params=_cparams(("parallel",)), name="proj_small")(x, wt, wt)


def _mm_res_kernel(*refs, n_x, row_chunk):
    x_refs = refs[:n_x]
    w_ref, res_ref, nw_ref, h_ref, hw_ref, ssq_ref = refs[n_x:]
    j = pl.program_id(1)
    wb = w_ref[...].astype(BF16)
    nw = nw_ref[...]
    parts = []
    for r0 in range(0, res_ref.shape[0], row_chunk):
        rows = slice(r0, r0 + row_chunk)
        k0 = 0
        acc = res_ref[rows, :]
        for x_ref in x_refs:
            kx = x_ref.shape[1]
            acc = acc + jnp.dot(x_ref[rows, :], wb[k0:k0 + kx], preferred_element_type=F32)
            k0 += kx
        h_ref[rows, :] = acc
        hw_ref[rows, :] = (acc * nw).astype(hw_ref.dtype)
        parts.append(jnp.sum(acc * acc, axis=-1, keepdims=True))
    part = jnp.broadcast_to(jnp.concatenate(parts, axis=0), ssq_ref.shape)

    @pl.when(j == 0)
    def _():
        ssq_ref[...] = part

    @pl.when(j > 0)
    def _():
        ssq_ref[...] += part


def matmul_res_norm(xs, w, res, norm_w, *, tm, tn, name, row_chunk=MM_ROW_CHUNK):
    M = xs[0].shape[0]
    N = w.shape[1]
    ktot = sum(x.shape[1] for x in xs)
    tile = pl.BlockSpec((tm, tn), lambda i, j: (i, j))
    return pl.pallas_call(
        functools.partial(_mm_res_kernel, n_x=len(xs), row_chunk=min(row_chunk, tm)), grid=(M // tm, N // tn),
        in_specs=[pl.BlockSpec((tm, x.shape[1]), lambda i, j: (i, 0)) for x in xs]
        + [pl.BlockSpec((ktot, tn), lambda i, j: (0, j)), tile, pl.BlockSpec((1, tn), lambda i, j: (0, j))],
        out_specs=[tile, tile, pl.BlockSpec((tm, LANES), lambda i, j: (i, 0))],
        out_shape=[jax.ShapeDtypeStruct((M, N), F32), jax.ShapeDtypeStruct((M, N), BF16),
                   jax.ShapeDtypeStruct((M, LANES), F32)],
        compiler_params=_cparams(("parallel", "arbitrary")), name=name)(*xs, w, res, norm_w.reshape(1, N))


def _mm_kacc_kernel(x_ref, w_ref, o_ref, *, row_chunk):
    k = pl.program_id(2)
    wb = w_ref[...].astype(BF16)
    tm = x_ref.shape[0]

    @pl.when(k == 0)
    def _():
        for r in range(0, tm, row_chunk):
            rows = slice(r, r + row_chunk)
            o_ref[rows, :] = jnp.dot(x_ref[rows, :], wb, preferred_element_type=F32)

    @pl.when(k > 0)
    def _():
        for r in range(0, tm, row_chunk):
            rows = slice(r, r + row_chunk)
            o_ref[rows, :] += jnp.dot(x_ref[rows, :], wb, preferred_element_type=F32)


def matmul_kacc(x, w, *, tm, tn, tk, row_chunk, name):
    M, K = x.shape
    N = w.shape[1]
    tm, tn, tk = min(tm, M), min(tn, N), min(tk, K)
    assert M % tm == 0 and N % tn == 0 and K % tk == 0 and tm % row_chunk == 0
    return pl.pallas_call(
        functools.partial(_mm_kacc_kernel, row_chunk=row_chunk), grid=(M // tm, N // tn, K // tk),
        in_specs=[pl.BlockSpec((tm, tk), lambda i, j, k: (i, k)),
                  pl.BlockSpec((tk, tn), lambda i, j, k: (k, j))],
        out_specs=pl.BlockSpec((tm, tn), lambda i, j, k: (i, j)),
        out_shape=jax.ShapeDtypeStruct((M, N), F32),
        compiler_params=_cparams(("parallel", "parallel", "arbitrary")), name=name)(x, w)


def _gates_kernel(y_ref, pos_ref, invf_ref, alog_ref, dtb_ref, aux_ref, cos_ref, sin_ref):
    y = y_ref[...]
    ang = pos_ref[...].astype(F32) * invf_ref[...]
    c = jnp.cos(ang)
    s = jnp.sin(ang)
    cos_ref[...] = c
    sin_ref[...] = s
    lane = lax.broadcasted_iota(jnp.int32, y.shape, 1)
    half = MLA_ROPE // 2
    rot = jnp.where(lane < half, -pltpu.roll(y, LANES - half, 1), pltpu.roll(y, half, 1))
    roped = y * c + rot * s
    z = y + dtb_ref[...]
    softplus = jnp.maximum(z, 0.0) + jnp.log1p(jnp.exp(-jnp.abs(z)))
    g = -jnp.exp(alog_ref[...]) * softplus
    beta = jax.nn.sigmoid(y)
    aux_ref[...] = jnp.where(lane < MLA_ROPE, roped, jnp.where(lane < MLA_ROPE + 16, g, beta))


def gates(y, pos_col, invf_row, alog_row, dtb_row, tm):
    M = y.shape[0]
    tm = min(tm, M)
    row = pl.BlockSpec((tm, LANES), lambda i: (i, 0))
    one = pl.BlockSpec((1, LANES), lambda i: (0, 0))
    shp = jax.ShapeDtypeStruct((M, LANES), F32)
    return pl.pallas_call(
        _gates_kernel, grid=(M // tm,),
        in_specs=[row, pl.BlockSpec((tm, 1), lambda i: (i, 0)), one, one, one],
        out_specs=[row, row, row], out_shape=[shp, shp, shp],
        compiler_params=_cparams(("parallel",)), name="gates")(y, pos_col, invf_row, alog_row, dtb_row)


def _mla_q_kernel(cq_ref, nw_ref, wq_ref, cos_ref, sin_ref, o_ref, *, heads, scale):
    x = cq_ref[...].astype(F32)
    r = lax.rsqrt(jnp.mean(x * x, axis=-1, keepdims=True) + NORM_EPS)
    xn = (x * r * nw_ref[...]).astype(BF16)
    q = jnp.dot(xn, wq_ref[...], preferred_element_type=F32) * scale
    half = MLA_ROPE // 2
    n_nope = heads * MLA_NOPE
    n_half = heads * half
    reps = n_half // LANES
    c = jnp.concatenate([cos_ref[...]] * reps, axis=1)
    s = jnp.concatenate([sin_ref[...]] * reps, axis=1)
    p1 = q[:, n_nope:n_nope + n_half]
    p2 = q[:, n_nope + n_half:]
    r1t = (p1 * c - p2 * s).T.astype(BF16)
    r2t = (p2 * c + p1 * s).T.astype(BF16)
    for h in range(heads):
        o_ref[h, 0:MLA_NOPE, :] = q[:, h * MLA_NOPE:(h + 1) * MLA_NOPE].T.astype(BF16)
        o_ref[h, MLA_NOPE:MLA_NOPE + half, :] = r1t[h * half:(h + 1) * half, :]
        o_ref[h, MLA_NOPE + half:MLA_NOPE + 2 * half, :] = r2t[h * half:(h + 1) * half, :]


def mla_q(proj, nw, wq, cos, sin, *, heads, tm, scale):
    S = proj.shape[0]
    R = nw.shape[0]
    dq = MLA_NOPE + MLA_ROPE
    kern = functools.partial(_mla_q_kernel, heads=heads, scale=scale)
    return pl.pallas_call(
        kern, grid=(S // tm,),
        in_specs=[pl.BlockSpec((tm, R), lambda i: (i, 0)),
                  pl.BlockSpec((1, R), lambda i: (0, 0)),
                  pl.BlockSpec(wq.shape, lambda i: (0, 0)),
                  pl.BlockSpec((tm, LANES), lambda i: (i, 0)),
                  pl.BlockSpec((tm, LANES), lambda i: (i, 0))],
        out_specs=pl.BlockSpec((heads, dq, tm), lambda i: (0, 0, i)),
        out_shape=jax.ShapeDtypeStruct((heads, dq, S), BF16),
        compiler_params=_cparams(("parallel",)), name="mla_q")(proj, nw.reshape(1, R), wq, cos, sin)


def _mla_kv_kernel(ckv_ref, nw_ref, wkv_ref, aux_ref, k_ref, vt_ref, *, heads):
    x = ckv_ref[...].astype(F32)
    r = lax.rsqrt(jnp.mean(x * x, axis=-1, keepdims=True) + NORM_EPS)
    xn = (x * r * nw_ref[...]).astype(BF16)
    kv = jnp.dot(xn, wkv_ref[...], preferred_element_type=F32)
    kpe = aux_ref[...][:, 0:MLA_ROPE].astype(BF16)
    n_nope = heads * MLA_NOPE
    for h in range(heads):
        k_ref[h, :, 0:MLA_NOPE] = kv[:, h * MLA_NOPE:(h + 1) * MLA_NOPE].astype(BF16)
        k_ref[h, :, MLA_NOPE:MLA_NOPE + MLA_ROPE] = kpe
        vt_ref[h, 0] = kv[:, n_nope + h * HEAD_DIM:n_nope + (h + 1) * HEAD_DIM].T.astype(BF16)


def mla_kv(proj, col_block, nw, wkv, aux, *, heads, tm):
    S = proj.shape[0]
    R = nw.shape[0]
    dq = MLA_NOPE + MLA_ROPE
    kern = functools.partial(_mla_kv_kernel, heads=heads)
    return pl.pallas_call(
        kern, grid=(S // tm,),
        in_specs=[pl.BlockSpec((tm, R), lambda i: (i, col_block)),
                  pl.BlockSpec((1, R), lambda i: (0, 0)),
                  pl.BlockSpec(wkv.shape, lambda i: (0, 0)),
                  pl.BlockSpec((tm, LANES), lambda i: (i, 0))],
        out_specs=[pl.BlockSpec((heads, tm, dq), lambda i: (0, i, 0)),
                   pl.BlockSpec((heads, 1, HEAD_DIM, tm), lambda i: (0, i, 0, 0))],
        out_shape=[jax.ShapeDtypeStruct((heads, S, dq), BF16),
                   jax.ShapeDtypeStruct((heads, S // tm, HEAD_DIM, tm), BF16)],
        compiler_params=_cparams(("parallel",)), name="mla_kv")(proj, nw.reshape(1, R), wkv, aux)


def _flash_kernel(qt_ref, k_ref, vt_ref, o_ref, acc_ref, m_ref, l_ref, sa_ref, sb_ref, *, t, hp):
    i = pl.program_id(1)
    m_ref[...] = jnp.full(m_ref.shape, -jnp.inf, F32)
    l_ref[...] = jnp.zeros(l_ref.shape, F32)
    acc_ref[...] = jnp.zeros(acc_ref.shape, F32)
    heads = range(hp)

    def scores(j, s_ref):
        for h in heads:
            s_ref[h] = jnp.dot(k_ref[h, j], qt_ref[h], preferred_element_type=F32)

    def softmax_pv(j, s_ref, masked):
        s = [s_ref[h] for h in heads]
        if masked:
            kpos = lax.broadcasted_iota(jnp.int32, (t, t), 0)
            qpos = lax.broadcasted_iota(jnp.int32, (t, t), 1)
            s = [jnp.where(kpos <= qpos, sh, -jnp.inf) for sh in s]
        m_old = [m_ref[h] for h in heads]
        m_new = [jnp.maximum(m_old[h], jnp.max(s[h], axis=0, keepdims=True)) for h in heads]
        alpha = [jnp.exp2(m_old[h] - m_new[h]) for h in heads]
        p = [jnp.exp2(s[h] - m_new[h]) for h in heads]
        for h in heads:
            l_ref[h] = alpha[h] * l_ref[h] + jnp.sum(p[h], axis=0, keepdims=True)
            m_ref[h] = m_new[h]
        pv = [jnp.dot(vt_ref[h, j], p[h].astype(BF16), preferred_element_type=F32) for h in heads]
        for h in heads:
            acc_ref[h] = alpha[h] * acc_ref[h] + pv[h]

    scores(0, sa_ref)

    def body(jj, carry):
        j = 2 * jj
        scores(j + 1, sb_ref)
        softmax_pv(j, sa_ref, False)
        scores(j + 2, sa_ref)
        softmax_pv(j + 1, sb_ref, False)
        return carry

    lax.fori_loop(0, i // 2, body, 0)

    @pl.when(i % 2 == 0)
    def _():
        softmax_pv(i, sa_ref, True)

    @pl.when(i % 2 == 1)
    def _():
        scores(i, sb_ref)
        softmax_pv(i - 1, sa_ref, False)
        softmax_pv(i, sb_ref, True)

    for h in range(hp):
        o_ref[:, h * HEAD_DIM:(h + 1) * HEAD_DIM] = (acc_ref[h] / l_ref[h]).T.astype(o_ref.dtype)


def flash_attention(qt, k4, vt4, *, t, hp):
    H, dq, S = qt.shape
    nk = S // t
    kern = functools.partial(_flash_kernel, t=t, hp=hp)
    return pl.pallas_call(
        kern, grid=(H // hp, nk),
        in_specs=[pl.BlockSpec((hp, dq, t), lambda g, i: (g, 0, i)),
                  pl.BlockSpec((hp, nk, t, dq), lambda g, i: (g, 0, 0, 0)),
                  pl.BlockSpec((hp, nk, HEAD_DIM, t), lambda g, i: (g, 0, 0, 0))],
        out_specs=pl.BlockSpec((t, hp * HEAD_DIM), lambda g, i: (i, g)),
        out_shape=jax.ShapeDtypeStruct((S, H * HEAD_DIM), BF16),
        scratch_shapes=[pltpu.VMEM((hp, HEAD_DIM, t), F32), pltpu.VMEM((hp, 1, t), F32),
                        pltpu.VMEM((hp, 1, t), F32), pltpu.VMEM((hp, t, t), F32), pltpu.VMEM((hp, t, t), F32)],
        compiler_params=_cparams(("parallel", "arbitrary")), name="mla_flash")(qt, k4, vt4)


def _cumsum_kernel(g_ref, o_ref, *, chunk):
    r = lax.broadcasted_iota(jnp.int32, (chunk, chunk), 0)
    c = lax.broadcasted_iota(jnp.int32, (chunk, chunk), 1)
    trilb = jnp.where(r >= c, 1.0, 0.0).astype(BF16)
    for r0 in range(0, g_ref.shape[0], chunk):
        g = g_ref[r0:r0 + chunk, :]
        g_hi = g.astype(BF16).astype(F32)
        g_mid = (g - g_hi).astype(BF16).astype(F32)
        g_lo = g - g_hi - g_mid
        o_ref[r0:r0 + chunk, :] = (jnp.dot(trilb, g_hi.astype(BF16), preferred_element_type=F32)
                                   + jnp.dot(trilb, g_mid.astype(BF16), preferred_element_type=F32)
                                   + jnp.dot(trilb, g_lo.astype(BF16), preferred_element_type=F32))


def chunk_cumsum(g, chunk, tm):
    M = g.shape[0]
    tm = min(tm, M)
    row = pl.BlockSpec((tm, LANES), lambda i: (i, 0))
    return pl.pallas_call(
        functools.partial(_cumsum_kernel, chunk=chunk), grid=(M // tm,), in_specs=[row], out_specs=row,
        out_shape=jax.ShapeDtypeStruct((M, LANES), F32),
        compiler_params=_cparams(("parallel",)), name="gdn_cumsum")(g)


def _bdot(a, b):
    return jnp.dot(a.astype(BF16), b.astype(BF16), preferred_element_type=F32)


def _conv_silu(x_ref, halo_ref, w_ref, xs_ref, first):
    rows = x_ref.shape[0]
    x = x_ref[...].astype(F32)
    xs_ref[0:8, :] = jnp.where(first, 0.0, halo_ref[...].astype(F32)[8:16, :])
    xs_ref[8:, :] = x
    w = w_ref[...]
    acc = x * w[GDN_CONV - 1:GDN_CONV, :]
    for s in range(1, GDN_CONV):
        acc = acc + xs_ref[8 - s:8 - s + rows, :] * w[GDN_CONV - 1 - s:GDN_CONV - s, :]
    return acc * jax.nn.sigmoid(acc)


def _l2n(y, scale):
    return y * (lax.rsqrt(jnp.sum(y * y, axis=-1, keepdims=True) + L2_EPS) * scale)


def _gdn_kernel(xq_ref, hq_ref, wq_ref, xk_ref, hk_ref, wk_ref, xv_ref, hv_ref, wv_ref, gate_ref,
                gcc_ref, bc_ref, gcr_ref, nw_ref, o_ref, s_ref, sq_ref, sk_ref, sv_ref, *, hb, rows, chunk):
    first = pl.program_id(1) == 0

    @pl.when(first)
    def _():
        s_ref[...] = jnp.zeros(s_ref.shape, F32)

    C = chunk
    ri = lax.broadcasted_iota(jnp.int32, (C, C), 0)
    ci = lax.broadcasted_iota(jnp.int32, (C, C), 1)
    lower = ri >= ci
    strict = ri > ci
    xor = ri ^ ci
    nw = nw_ref[...]
    n_levels = C.bit_length() - 1
    n_chunks = rows // C
    bodies = [(c, h) for c in range(n_chunks) for h in range(hb)]

    def rows_of(c):
        return slice(c * C, (c + 1) * C)

    def cols_of(h):
        return slice(h * HEAD_DIM, (h + 1) * HEAD_DIM)

    yq = _conv_silu(xq_ref, hq_ref, wq_ref, sq_ref, first)
    yk = _conv_silu(xk_ref, hk_ref, wk_ref, sk_ref, first)
    yv = _conv_silu(xv_ref, hv_ref, wv_ref, sv_ref, first)

    amats, rhs, qks, qds, tails, eglasts = {}, {}, {}, {}, {}, {}
    for b in bodies:
        c, h = b
        q = _l2n(yq[rows_of(c), cols_of(h)], HEAD_DIM ** -0.5)
        k = _l2n(yk[rows_of(c), cols_of(h)], 1.0)
        v = yv[rows_of(c), cols_of(h)]
        kt = k.T
        gcc = gcc_ref[rows_of(c), h:h + 1]
        bc = bc_ref[rows_of(c), h:h + 1]
        gcr = gcr_ref[h:h + 1, rows_of(c)]
        glast = gcr[:, C - 1:C]
        decay = jnp.exp(jnp.where(lower, gcc - gcr, -jnp.inf))
        egc = jnp.exp(gcc)
        kb = k * bc
        amats[b] = jnp.where(strict, _bdot(kb, kt) * decay, 0.0)
        rhs[b] = jnp.concatenate([v * bc, kb * egc], axis=1)
        qks[b] = _bdot(q, kt) * decay
        qds[b] = q * egc
        tails[b] = kt * jnp.exp(glast - gcr)
        eglasts[b] = jnp.exp(glast)

    tinv = {b: jnp.where(ri == ci, 1.0, 0.0) - jnp.where(strict & (xor == 1), amats[b], 0.0) for b in bodies}
    for level in range(1, n_levels):
        join = strict & ((xor >> level) == 1)
        mids = {b: _bdot(jnp.where(join, amats[b], 0.0), tinv[b]) for b in bodies}
        for b in bodies:
            tinv[b] = tinv[b] - _bdot(tinv[b], mids[b])
    xs = {b: _bdot(tinv[b], rhs[b]) for b in bodies}

    states = [s_ref[h] for h in range(hb)]
    for c in range(n_chunks):
        ws = [_bdot(jnp.concatenate([xs[(c, h)][:, HEAD_DIM:], qds[(c, h)]], axis=0), states[h]) for h in range(hb)]
        v_new = [xs[(c, h)][:, :HEAD_DIM] - ws[h][:C] for h in range(hb)]
        outs = [ws[h][C:] + _bdot(qks[(c, h)], v_new[h]) for h in range(hb)]
        states = [states[h] * eglasts[(c, h)] + _bdot(tails[(c, h)], v_new[h]) for h in range(hb)]
        for h in range(hb):
            o = outs[h]
            o = o * lax.rsqrt(jnp.mean(o * o, axis=-1, keepdims=True) + NORM_EPS) * nw
            gate = gate_ref[rows_of(c), cols_of(h)].astype(F32)
            o_ref[rows_of(c), cols_of(h)] = (o * (gate * jax.nn.sigmoid(gate))).astype(o_ref.dtype)
    for h in range(hb):
        s_ref[h] = states[h]


def gdn_mixer(proj, qkv_col0, gate_col0, conv_w, gcc, bc, gcr, nw, *, heads, hb, rows, chunk):
    S = proj.shape[0]
    tc = hb * HEAD_DIM
    gw = heads * HEAD_DIM
    halo_blocks = rows // 16
    assert qkv_col0 % tc == 0 and gate_col0 % tc == 0 and gw % tc == 0 and S % rows == 0 and rows % chunk == 0

    def section(col0, w_col0):
        cb, wb = col0 // tc, w_col0 // tc
        return [pl.BlockSpec((rows, tc), lambda g, r: (r, cb + g)),
                pl.BlockSpec((16, tc), lambda g, r: (jnp.maximum(r * halo_blocks - 1, 0), cb + g)),
                pl.BlockSpec((GDN_CONV, tc), lambda g, r: (0, wb + g))]

    in_specs = (section(qkv_col0, 0) + section(qkv_col0 + gw, gw) + section(qkv_col0 + 2 * gw, 2 * gw)
                + [pl.BlockSpec((rows, tc), lambda g, r: (r, gate_col0 // tc + g)),
                   pl.BlockSpec((None, rows, hb), lambda g, r: (g, r, 0)),
                   pl.BlockSpec((None, rows, hb), lambda g, r: (g, r, 0)),
                   pl.BlockSpec((None, hb, rows), lambda g, r: (g, 0, r)),
                   pl.BlockSpec((1, HEAD_DIM), lambda g, r: (0, 0))])
    stage = pltpu.VMEM((rows + 8, tc), F32)
    return pl.pallas_call(
        functools.partial(_gdn_kernel, hb=hb, rows=rows, chunk=chunk), grid=(heads // hb, S // rows),
        in_specs=in_specs,
        out_specs=pl.BlockSpec((rows, tc), lambda g, r: (r, g)),
        out_shape=jax.ShapeDtypeStruct((S, gw), BF16),
        scratch_shapes=[pltpu.VMEM((hb, HEAD_DIM, HEAD_DIM), F32), stage, stage, stage],
        compiler_params=_cparams(("parallel", "arbitrary")), name="gdn_mixer")(
            proj, proj, conv_w, proj, proj, conv_w, proj, proj, conv_w, proj, gcc, bc, gcr, nw.reshape(1, HEAD_DIM))


def _xattn_kernel(q_ref, kt_ref, v_ref, o_ref, *, heads):
    d = q_ref.shape[1] // heads
    cols = [slice(h * d, (h + 1) * d) for h in range(heads)]
    s = [jnp.dot(q_ref[:, c], kt_ref[c, :], preferred_element_type=F32) for c in cols]
    p = [jnp.exp2(sh - jnp.max(sh, axis=-1, keepdims=True)) for sh in s]
    l = [jnp.sum(ph, axis=-1, keepdims=True) for ph in p]
    o = [jnp.dot(ph.astype(BF16), v_ref[:, c], preferred_element_type=F32) for ph, c in zip(p, cols)]
    for c, oh, lh in zip(cols, o, l):
        o_ref[:, c] = (oh / lh).astype(o_ref.dtype)


def xattn_core(q, kt, v, *, heads, tq):
    S, W = q.shape
    M = v.shape[0]
    return pl.pallas_call(
        functools.partial(_xattn_kernel, heads=heads), grid=(S // tq,),
        in_specs=[pl.BlockSpec((tq, W), lambda i: (i, 0)),
                  pl.BlockSpec((W, M), lambda i: (0, 0)),
                  pl.BlockSpec((M, W), lambda i: (0, 0))],
        out_specs=pl.BlockSpec((tq, W), lambda i: (i, 0)),
        out_shape=jax.ShapeDtypeStruct((S, W), BF16),
        compiler_params=_cparams(("parallel",)), name="xattn_core")(q, kt, v)


def _block(x, mem, positions, attn_norm_w, w_in, mla_q_norm_w, mla_w_q_b, mla_kv_norm_w, mla_w_kv_b,
           gdn_conv_w, gdn_a_log, gdn_dt_bias, gdn_norm_w, w_out, xattn_norm_w, mem_norm_w,
           xattn_wq, xattn_wk, xattn_wv, xattn_wo, mlp_norm_w, mlp_w_up, mlp_w_down, final_norm_w):
    S, D = x.shape
    n_mix = w_out.shape[0]
    mla_heads = n_mix // 2 // HEAD_DIM
    gdn_heads = gdn_conv_w.shape[1] // (3 * HEAD_DIM)
    gw = gdn_heads * HEAD_DIM
    dq = MLA_NOPE + MLA_ROPE
    half = MLA_ROPE // 2

    o_kv = MLA_Q_RANK
    o_pe = o_kv + MLA_KV_RANK
    o_qkv = o_pe + MLA_ROPE
    o_a = o_qkv + 3 * gw
    o_b = o_a + gdn_heads
    o_gate = o_b + gdn_heads
    w_in_t = w_in.T
    c_qkv = o_pe
    c_gate = o_pe + 3 * gw
    n_proj = c_gate + (w_in.shape[1] - o_gate)
    tn_proj = 512

    skip1 = o_qkv - c_qkv
    skip2 = (o_gate - c_gate) - skip1
    unit = math.gcd(tn_proj, skip1, skip2)

    def proj_src_row(j):
        v = j * tn_proj
        past1 = (v >= c_qkv).astype(jnp.int32)
        past2 = (v >= c_gate).astype(jnp.int32)
        return (j * (tn_proj // unit) + past1 * (skip1 // unit) + past2 * (skip2 // unit)) * unit

    wq3 = mla_w_q_b.reshape(MLA_Q_RANK, mla_heads, dq)
    wq_perm = jnp.concatenate([wq3[:, :, :MLA_NOPE].reshape(MLA_Q_RANK, -1),
                               wq3[:, :, MLA_NOPE:MLA_NOPE + half].reshape(MLA_Q_RANK, -1),
                               wq3[:, :, MLA_NOPE + half:].reshape(MLA_Q_RANK, -1)], axis=1).astype(BF16)
    wkv3 = mla_w_kv_b.reshape(MLA_KV_RANK, mla_heads, MLA_NOPE + HEAD_DIM)
    wkv_perm = jnp.concatenate([wkv3[:, :, :MLA_NOPE].reshape(MLA_KV_RANK, -1),
                                wkv3[:, :, MLA_NOPE:].reshape(MLA_KV_RANK, -1)], axis=1).astype(BF16)

    inv_freq = ROPE_BASE ** (-jnp.arange(half, dtype=F32) / half)
    invf_row = jnp.tile(inv_freq, LANES // half).reshape(1, LANES)
    pad_l = jnp.zeros((MLA_ROPE,), F32)
    pad_r = jnp.zeros((LANES - MLA_ROPE - gdn_heads,), F32)
    alog_row = jnp.concatenate([pad_l, gdn_a_log.astype(F32), pad_r]).reshape(1, LANES)
    dtb_row = jnp.concatenate([pad_l, gdn_dt_bias.astype(F32), pad_r]).reshape(1, LANES)

    xn = rms_norm(x, attn_norm_w, BF16, 512)
    proj = proj_nt(xn, w_in_t, proj_src_row, n_proj, tm=1024, tn=tn_proj, name="proj_main")
    small = proj_small(xn, w_in_t, o_pe, MLA_ROPE, o_a, 2 * gdn_heads, tm=1024)
    aux, cos, sin = gates(small, positions.reshape(S, 1), invf_row, alog_row, dtb_row, 1024)

    t_att = 512
    qt = mla_q(proj, mla_q_norm_w, wq_perm, cos, sin, heads=mla_heads, tm=256, scale=dq ** -0.5 * LOG2E)
    k_full, vt4 = mla_kv(proj, o_kv // MLA_KV_RANK, mla_kv_norm_w, wkv_perm, aux, heads=mla_heads, tm=t_att)
    k4 = k_full.reshape(mla_heads, S // t_att, t_att, dq)
    mla_out = flash_attention(qt, k4, vt4, t=t_att, hp=2)

    hb = 4
    gc_full = chunk_cumsum(aux, GDN_CHUNK, 1024)
    gc = gc_full[:, MLA_ROPE:MLA_ROPE + gdn_heads]
    beta = aux[:, MLA_ROPE + gdn_heads:MLA_ROPE + 2 * gdn_heads]
    gcc = gc.reshape(S, gdn_heads // hb, hb).transpose(1, 0, 2)
    bc = beta.reshape(S, gdn_heads // hb, hb).transpose(1, 0, 2)
    gcr = gc.T.reshape(gdn_heads // hb, hb, S)
    gdn_out = gdn_mixer(proj, c_qkv, c_gate, gdn_conv_w, gcc, bc, gcr, gdn_norm_w,
                        heads=gdn_heads, hb=hb, rows=512, chunk=GDN_CHUNK)

    h1, h1w, h1_ssq = matmul_res_norm([mla_out, gdn_out], w_out, x, xattn_norm_w, tm=1024, tn=512, name="w_out")

    memn = rms_norm(mem, mem_norm_w, BF16, 256)
    xd = D // XATTN_HEADS
    xq = matmul(h1w, xattn_wq, tm=1024, tn=512, out_dtype=BF16, out_scale=xd ** -0.5 * LOG2E,
                lhs_ssq=h1_ssq, name="xattn_q")
    xk = matmul(memn, xattn_wk, tm=256, tn=512, out_dtype=BF16, name="xattn_k")
    xv = matmul(memn, xattn_wv, tm=256, tn=512, out_dtype=BF16, name="xattn_v")
    xo = xattn_core(xq, xk.T, xv, heads=XATTN_HEADS, tq=512)
    h2, h2w, h2_ssq = matmul_res_norm([xo], xattn_wo, h1, mlp_norm_w, tm=1024, tn=512, name="xattn_o")

    up = matmul(h2w, mlp_w_up, tm=1024, tn=512, out_dtype=BF16, relu2=True, lhs_ssq=h2_ssq, name="mlp_up")
    down = matmul_kacc(up, mlp_w_down, tm=2048, tn=1024, tk=2048, row_chunk=256, name="mlp_down")
    return rms_norm(h2, final_norm_w, F32, 256, add=down)


def kernel(x, mem, positions, attn_norm_w, w_in, mla_q_norm_w, mla_w_q_b, mla_kv_norm_w, mla_w_kv_b, gdn_conv_w, gdn_a_log, gdn_dt_bias, gdn_norm_w, w_out, xattn_norm_w, mem_norm_w, xattn_wq, xattn_wk, xattn_wv, xattn_wo, mlp_norm_w, mlp_w_up, mlp_w_down, final_norm_w):
    assert x.shape[0] == 1 and attn_norm_w.shape[0] == 1, "single sequence, single layer"
    out = _block(x[0], mem[0], positions[0], attn_norm_w[0], w_in[0], mla_q_norm_w[0], mla_w_q_b[0],
                 mla_kv_norm_w[0], mla_w_kv_b[0], gdn_conv_w[0], gdn_a_log[0], gdn_dt_bias[0], gdn_norm_w[0],
                 w_out[0], xattn_norm_w[0], mem_norm_w[0], xattn_wq[0], xattn_wk[0], xattn_wv[0], xattn_wo[0],
                 mlp_norm_w[0], mlp_w_up[0], mlp_w_down[0], final_norm_w)
    return out[None]
```

```python
import functools
import math

import jax
import jax.numpy as jnp
from jax import lax
from jax.experimental import pallas as pl
from jax.experimental.pallas import tpu as pltpu

F32 = jnp.float32
BF16 = jnp.bfloat16

HEAD_DIM = 128
MLA_NOPE = 128
MLA_ROPE = 64
MLA_Q_RANK = 1024
MLA_KV_RANK = 512
GDN_CONV = 4
XATTN_HEADS = 4
ROPE_BASE = 10000.0
NORM_EPS = 1e-6
L2_EPS = 1e-6
LOG2E = math.log2(math.e)

V7X_VMEM_BYTES = 64 * 1024 * 1024
VMEM_LIMIT = 58 * 1024 * 1024
LANES = 128
MM_ROW_CHUNK = 2048

GDN_CHUNK = 128


def _cparams(sem):
    return pltpu.CompilerParams(dimension_semantics=sem, vmem_limit_bytes=VMEM_LIMIT)


def _rms_kernel(x_ref, w_ref, o_ref):
    x = x_ref[...].astype(F32)
    r = lax.rsqrt(jnp.mean(x * x, axis=-1, keepdims=True) + NORM_EPS)
    o_ref[...] = (x * r * w_ref[...]).astype(o_ref.dtype)


def _rms_add_kernel(x_ref, y_ref, w_ref, o_ref):
    x = x_ref[...] + y_ref[...]
    r = lax.rsqrt(jnp.mean(x * x, axis=-1, keepdims=True) + NORM_EPS)
    o_ref[...] = (x * r * w_ref[...]).astype(o_ref.dtype)


def rms_norm(x, w, out_dtype, tm, add=None):
    M, D = x.shape
    tm = min(tm, M)
    row = pl.BlockSpec((tm, D), lambda i: (i, 0))
    wspec = pl.BlockSpec((1, D), lambda i: (0, 0))
    if add is None:
        kern, specs, args = _rms_kernel, [row, wspec], (x, w.reshape(1, D))
    else:
        kern, specs, args = _rms_add_kernel, [row, row, wspec], (x, add, w.reshape(1, D))
    return pl.pallas_call(
        kern, grid=(M // tm,), in_specs=specs, out_specs=row,
        out_shape=jax.ShapeDtypeStruct((M, D), out_dtype),
        compiler_params=_cparams(("parallel",)), name="rms_norm")(*args)


def _mm_kernel(*refs, relu2, has_res, out_scale, ssq_dim, row_chunk):
    x_ref, w_ref = refs[0], refs[1]
    res_ref = refs[2] if has_res else None
    ssq_ref = refs[2 + has_res] if ssq_dim else None
    o_ref = refs[2 + has_res + bool(ssq_dim)]
    wb = w_ref[...].astype(BF16)
    for r0 in range(0, x_ref.shape[0], row_chunk):
        rows = slice(r0, r0 + row_chunk)
        r = jnp.dot(x_ref[rows, :], wb, preferred_element_type=F32)
        if ssq_dim:
            r = r * lax.rsqrt(ssq_ref[rows, 0:1] * (1.0 / ssq_dim) + NORM_EPS)
        if relu2:
            r = jnp.square(jnp.maximum(r, 0.0))
        if out_scale is not None:
            r = r * out_scale
        if has_res:
            r = r + res_ref[rows, :]
        o_ref[rows, :] = r.astype(o_ref.dtype)


def matmul(x, w, *, tm, tn, out_dtype, relu2=False, res=None, out_scale=None, lhs_ssq=None, row_chunk=MM_ROW_CHUNK,
           x_buffers=2, name="matmul"):
    M, K = x.shape
    K2, N = w.shape
    assert K == K2
    tm, tn = min(tm, M), min(tn, N)
    row_chunk = min(row_chunk, tm)
    assert M % tm == 0 and N % tn == 0 and tm % row_chunk == 0
    in_specs = [pl.BlockSpec((tm, K), lambda i, j: (i, 0), pipeline_mode=pl.Buffered(x_buffers)),
                pl.BlockSpec((K, tn), lambda i, j: (0, j))]
    args = [x, w]
    if res is not None:
        in_specs.append(pl.BlockSpec((tm, tn), lambda i, j: (i, j)))
        args.append(res)
    if lhs_ssq is not None:
        in_specs.append(pl.BlockSpec((tm, LANES), lambda i, j: (i, 0)))
        args.append(lhs_ssq)
    kern = functools.partial(_mm_kernel, relu2=relu2, has_res=res is not None, out_scale=out_scale,
                             ssq_dim=K if lhs_ssq is not None else 0, row_chunk=row_chunk)
    return pl.pallas_call(
        kern, grid=(M // tm, N // tn), in_specs=in_specs,
        out_specs=pl.BlockSpec((tm, tn), lambda i, j: (i, j)),
        out_shape=jax.ShapeDtypeStruct((M, N), out_dtype),
        compiler_params=_cparams(("parallel", "parallel")), name=name)(*args)


_NT_DIMS = (((1,), (1,)), ((), ()))


def _proj_nt_kernel(x_ref, wt_ref, o_ref):
    wb = wt_ref[...].astype(BF16)
    o_ref[...] = lax.dot_general(x_ref[...], wb, _NT_DIMS, preferred_element_type=F32).astype(o_ref.dtype)


def proj_nt(x, wt, src_row_of_block, n_out, *, tm, tn, name):
    M, K = x.shape
    return pl.pallas_call(
        _proj_nt_kernel, grid=(M // tm, n_out // tn),
        in_specs=[pl.BlockSpec((tm, K), lambda i, j: (i, 0)),
                  pl.BlockSpec((pl.Element(tn), pl.Element(K)), lambda i, j: (src_row_of_block(j), 0))],
        out_specs=pl.BlockSpec((tm, tn), lambda i, j: (i, j)),
        out_shape=jax.ShapeDtypeStruct((M, n_out), BF16),
        compiler_params=_cparams(("parallel", "parallel")), name=name)(x, wt)


def _proj_small_kernel(x_ref, wa_ref, wb_ref, o_ref):
    wa = wa_ref[...].astype(BF16)
    wb = wb_ref[...].astype(BF16)
    pad = jnp.zeros((LANES - wa.shape[0] - wb.shape[0], wa.shape[1]), BF16)
    w = jnp.concatenate([wa, wb, pad], axis=0)
    o_ref[...] = lax.dot_general(x_ref[...], w, _NT_DIMS, preferred_element_type=F32)


def proj_small(x, wt, row_a, n_a, row_b, n_b, *, tm):
    M, K = x.shape
    return pl.pallas_call(
        _proj_small_kernel, grid=(M // tm,),
        in_specs=[pl.BlockSpec((tm, K), lambda i: (i, 0)),
                  pl.BlockSpec((pl.Element(n_a), pl.Element(K)), lambda i: (row_a, 0)),
                  pl.BlockSpec((pl.Element(n_b), pl.Element(K)), lambda i: (row_b, 0))],
        out_specs=pl.BlockSpec((tm, LANES), lambda i: (i, 0)),
        out_shape=jax.ShapeDtypeStruct((M, LANES), F32),
        compiler_params=_cparams(("parallel",)), name="proj_small")(x, wt, wt)


def _mm_res_kernel(*refs, n_x, row_chunk):
    x_refs = refs[:n_x]
    w_ref, res_ref, nw_ref, h_ref, hw_ref, ssq_ref = refs[n_x:]
    j = pl.program_id(1)
    wb = w_ref[...].astype(BF16)
    nw = nw_ref[...]
    parts = []
    for r0 in range(0, res_ref.shape[0], row_chunk):
        rows = slice(r0, r0 + row_chunk)
        k0 = 0
        acc = res_ref[rows, :]
        for x_ref in x_refs:
            kx = x_ref.shape[1]
            acc = acc + jnp.dot(x_ref[rows, :], wb[k0:k0 + kx], preferred_element_type=F32)
            k0 += kx
        h_ref[rows, :] = acc
        hw_ref[rows, :] = (acc * nw).astype(hw_ref.dtype)
        parts.append(jnp.sum(acc * acc, axis=-1, keepdims=True))
    part = jnp.broadcast_to(jnp.concatenate(parts, axis=0), ssq_ref.shape)

    @pl.when(j == 0)
    def _():
        ssq_ref[...] = part

    @pl.when(j > 0)
    def _():
        ssq_ref[...] += part


def matmul_res_norm(xs, w, res, norm_w, *, tm, tn, name, row_chunk=MM_ROW_CHUNK):
    M = xs[0].shape[0]
    N = w.shape[1]
    ktot = sum(x.shape[1] for x in xs)
    tile = pl.BlockSpec((tm, tn), lambda i, j: (i, j))
    return pl.pallas_call(
        functools.partial(_mm_res_kernel, n_x=len(xs), row_chunk=min(row_chunk, tm)), grid=(M // tm, N // tn),
        in_specs=[pl.BlockSpec((tm, x.shape[1]), lambda i, j: (i, 0)) for x in xs]
        + [pl.BlockSpec((ktot, tn), lambda i, j: (0, j)), tile, pl.BlockSpec((1, tn), lambda i, j: (0, j))],
        out_specs=[tile, tile, pl.BlockSpec((tm, LANES), lambda i, j: (i, 0))],
        out_shape=[jax.ShapeDtypeStruct((M, N), F32), jax.ShapeDtypeStruct((M, N), BF16),
                   jax.ShapeDtypeStruct((M, LANES), F32)],
        compiler_params=_cparams(("parallel", "arbitrary")), name=name)(*xs, w, res, norm_w.reshape(1, N))


def _mm_kacc_kernel(x_ref, w_ref, o_ref, *, row_chunk):
    k = pl.program_id(2)
    wb = w_ref[...].astype(BF16)
    tm = x_ref.shape[0]

    @pl.when(k == 0)
    def _():
        for r in range(0, tm, row_chunk):
            rows = slice(r, r + row_chunk)
            o_ref[rows, :] = jnp.dot(x_ref[rows, :], wb, preferred_element_type=F32)

    @pl.when(k > 0)
    def _():
        for r in range(0, tm, row_chunk):
            rows = slice(r, r + row_chunk)
            o_ref[rows, :] += jnp.dot(x_ref[rows, :], wb, preferred_element_type=F32)


def matmul_kacc(x, w, *, tm, tn, tk, row_chunk, name):
    M, K = x.shape
    N = w.shape[1]
    tm, tn, tk = min(tm, M), min(tn, N), min(tk, K)
    assert M % tm == 0 and N % tn == 0 and K % tk == 0 and tm % row_chunk == 0
    return pl.pallas_call(
        functools.partial(_mm_kacc_kernel, row_chunk=row_chunk), grid=(M // tm, N // tn, K // tk),
        in_specs=[pl.BlockSpec((tm, tk), lambda i, j, k: (i, k)),
                  pl.BlockSpec((tk, tn), lambda i, j, k: (k, j))],
        out_specs=pl.BlockSpec((tm, tn), lambda i, j, k: (i, j)),
        out_shape=jax.ShapeDtypeStruct((M, N), F32),
        compiler_params=_cparams(("parallel", "parallel", "arbitrary")), name=name)(x, w)


def _gates_kernel(y_ref, pos_ref, invf_ref, alog_ref, dtb_ref, aux_ref, cos_ref, sin_ref):
    y = y_ref[...]
    ang = pos_ref[...].astype(F32) * invf_ref[...]
    c = jnp.cos(ang)
    s = jnp.sin(ang)
    cos_ref[...] = c
    sin_ref[...] = s
    lane = lax.broadcasted_iota(jnp.int32, y.shape, 1)
    half = MLA_ROPE // 2
    rot = jnp.where(lane < half, -pltpu.roll(y, LANES - half, 1), pltpu.roll(y, half, 1))
    roped = y * c + rot * s
    z = y + dtb_ref[...]
    softplus = jnp.maximum(z, 0.0) + jnp.log1p(jnp.exp(-jnp.abs(z)))
    g = -jnp.exp(alog_ref[...]) * softplus
    beta = jax.nn.sigmoid(y)
    aux_ref[...] = jnp.where(lane < MLA_ROPE, roped, jnp.where(lane < MLA_ROPE + 16, g, beta))


def gates(y, pos_col, invf_row, alog_row, dtb_row, tm):
    M = y.shape[0]
    tm = min(tm, M)
    row = pl.BlockSpec((tm, LANES), lambda i: (i, 0))
    one = pl.BlockSpec((1, LANES), lambda i: (0, 0))
    shp = jax.ShapeDtypeStruct((M, LANES), F32)
    return pl.pallas_call(
        _gates_kernel, grid=(M // tm,),
        in_specs=[row, pl.BlockSpec((tm, 1), lambda i: (i, 0)), one, one, one],
        out_specs=[row, row, row], out_shape=[shp, shp, shp],
        compiler_params=_cparams(("parallel",)), name="gates")(y, pos_col, invf_row, alog_row, dtb_row)


def _mla_q_kernel(cq_ref, nw_ref, wq_ref, cos_ref, sin_ref, o_ref, *, heads, scale):
    x = cq_ref[...].astype(F32)
    r = lax.rsqrt(jnp.mean(x * x, axis=-1, keepdims=True) + NORM_EPS)
    xn = (x * r * nw_ref[...]).astype(BF16)
    q = jnp.dot(xn, wq_ref[...], preferred_element_type=F32) * scale
    half = MLA_ROPE // 2
    n_nope = heads * MLA_NOPE
    n_half = heads * half
    reps = n_half // LANES
    c = jnp.concatenate([cos_ref[...]] * reps, axis=1)
    s = jnp.concatenate([sin_ref[...]] * reps, axis=1)
    p1 = q[:, n_nope:n_nope + n_half]
    p2 = q[:, n_nope + n_half:]
    r1t = (p1 * c - p2 * s).T.astype(BF16)
    r2t = (p2 * c + p1 * s).T.astype(BF16)
    for h in range(heads):
        o_ref[h, 0:MLA_NOPE, :] = q[:, h * MLA_NOPE:(h + 1) * MLA_NOPE].T.astype(BF16)
        o_ref[h, MLA_NOPE:MLA_NOPE + half, :] = r1t[h * half:(h + 1) * half, :]
        o_ref[h, MLA_NOPE + half:MLA_NOPE + 2 * half, :] = r2t[h * half:(h + 1) * half, :]


def mla_q(proj, nw, wq, cos, sin, *, heads, tm, scale):
    S = proj.shape[0]
    R = nw.shape[0]
    dq = MLA_NOPE + MLA_ROPE
    kern = functools.partial(_mla_q_kernel, heads=heads, scale=scale)
    return pl.pallas_call(
        kern, grid=(S // tm,),
        in_specs=[pl.BlockSpec((tm, R), lambda i: (i, 0)),
                  pl.BlockSpec((1, R), lambda i: (0, 0)),
                  pl.BlockSpec(wq.shape, lambda i: (0, 0)),
                  pl.BlockSpec((tm, LANES), lambda i: (i, 0)),
                  pl.BlockSpec((tm, LANES), lambda i: (i, 0))],
        out_specs=pl.BlockSpec((heads, dq, tm), lambda i: (0, 0, i)),
        out_shape=jax.ShapeDtypeStruct((heads, dq, S), BF16),
        compiler_params=_cparams(("parallel",)), name="mla_q")(proj, nw.reshape(1, R), wq, cos, sin)


def _mla_kv_kernel(ckv_ref, nw_ref, wkv_ref, aux_ref, k_ref, vt_ref, *, heads):
    x = ckv_ref[...].astype(F32)
    r = lax.rsqrt(jnp.mean(x * x, axis=-1, keepdims=True) + NORM_EPS)
    xn = (x * r * nw_ref[...]).astype(BF16)
    kv = jnp.dot(xn, wkv_ref[...], preferred_element_type=F32)
    kpe = aux_ref[...][:, 0:MLA_ROPE].astype(BF16)
    n_nope = heads * MLA_NOPE
    for h in range(heads):
        k_ref[h, :, 0:MLA_NOPE] = kv[:, h * MLA_NOPE:(h + 1) * MLA_NOPE].astype(BF16)
        k_ref[h, :, MLA_NOPE:MLA_NOPE + MLA_ROPE] = kpe
        vt_ref[h, 0] = kv[:, n_nope + h * HEAD_DIM:n_nope + (h + 1) * HEAD_DIM].T.astype(BF16)


def mla_kv(proj, col_block, nw, wkv, aux, *, heads, tm):
    S = proj.shape[0]
    R = nw.shape[0]
    dq = MLA_NOPE + MLA_ROPE
    kern = functools.partial(_mla_kv_kernel, heads=heads)
    return pl.pallas_call(
        kern, grid=(S // tm,),
        in_specs=[pl.BlockSpec((tm, R), lambda i: (i, col_block)),
                  pl.BlockSpec((1, R), lambda i: (0, 0)),
                  pl.BlockSpec(wkv.shape, lambda i: (0, 0)),
                  pl.BlockSpec((tm, LANES), lambda i: (i, 0))],
        out_specs=[pl.BlockSpec((heads, tm, dq), lambda i: (0, i, 0)),
                   pl.BlockSpec((heads, 1, HEAD_DIM, tm), lambda i: (0, i, 0, 0))],
        out_shape=[jax.ShapeDtypeStruct((heads, S, dq), BF16),
                   jax.ShapeDtypeStruct((heads, S // tm, HEAD_DIM, tm), BF16)],
        compiler_params=_cparams(("parallel",)), name="mla_kv")(proj, nw.reshape(1, R), wkv, aux)


def _flash_kernel(qt_ref, k_ref, vt_ref, o_ref, acc_ref, m_ref, l_ref, sa_ref, sb_ref, *, t, hp):
    i = pl.program_id(1)
    m_ref[...] = jnp.full(m_ref.shape, -jnp.inf, F32)
    l_ref[...] = jnp.zeros(l_ref.shape, F32)
    acc_ref[...] = jnp.zeros(acc_ref.shape, F32)
    heads = range(hp)

    def scores(j, s_ref):
        for h in heads:
            s_ref[h] = jnp.dot(k_ref[h, j], qt_ref[h], preferred_element_type=F32)

    def softmax_pv(j, s_ref, masked):
        s = [s_ref[h] for h in heads]
        if masked:
            kpos = lax.broadcasted_iota(jnp.int32, (t, t), 0)
            qpos = lax.broadcasted_iota(jnp.int32, (t, t), 1)
            s = [jnp.where(kpos <= qpos, sh, -jnp.inf) for sh in s]
        m_old = [m_ref[h] for h in heads]
        m_new = [jnp.maximum(m_old[h], jnp.max(s[h], axis=0, keepdims=True)) for h in heads]
        alpha = [jnp.exp2(m_old[h] - m_new[h]) for h in heads]
        p = [jnp.exp2(s[h] - m_new[h]) for h in heads]
        for h in heads:
            l_ref[h] = alpha[h] * l_ref[h] + jnp.sum(p[h], axis=0, keepdims=True)
            m_ref[h] = m_new[h]
        pv = [jnp.dot(vt_ref[h, j], p[h].astype(BF16), preferred_element_type=F32) for h in heads]
        for h in heads:
            acc_ref[h] = alpha[h] * acc_ref[h] + pv[h]

    scores(0, sa_ref)

    def body(jj, carry):
        j = 2 * jj
        scores(j + 1, sb_ref)
        softmax_pv(j, sa_ref, False)
        scores(j + 2, sa_ref)
        softmax_pv(j + 1, sb_ref, False)
        return carry

    lax.fori_loop(0, i // 2, body, 0)

    @pl.when(i % 2 == 0)
    def _():
        softmax_pv(i, sa_ref, True)

    @pl.when(i % 2 == 1)
    def _():
        scores(i, sb_ref)
        softmax_pv(i - 1, sa_ref, False)
        softmax_pv(i, sb_ref, True)

    for h in range(hp):
        o_ref[:, h * HEAD_DIM:(h + 1) * HEAD_DIM] = (acc_ref[h] / l_ref[h]).T.astype(o_ref.dtype)


def flash_attention(qt, k4, vt4, *, t, hp, kv_buffers=2):
    H, dq, S = qt.shape
    nk = S // t
    kern = functools.partial(_flash_kernel, t=t, hp=hp)
    return pl.pallas_call(
        kern, grid=(H // hp, nk),
        in_specs=[pl.BlockSpec((hp, dq, t), lambda g, i: (g, 0, i)),
                  pl.BlockSpec((hp, nk, t, dq), lambda g, i: (g, 0, 0, 0), pipeline_mode=pl.Buffered(kv_buffers)),
                  pl.BlockSpec((hp, nk, HEAD_DIM, t), lambda g, i: (g, 0, 0, 0),
                               pipeline_mode=pl.Buffered(kv_buffers))],
        out_specs=pl.BlockSpec((t, hp * HEAD_DIM), lambda g, i: (i, g)),
        out_shape=jax.ShapeDtypeStruct((S, H * HEAD_DIM), BF16),
        scratch_shapes=[pltpu.VMEM((hp, HEAD_DIM, t), F32), pltpu.VMEM((hp, 1, t), F32),
                        pltpu.VMEM((hp, 1, t), F32), pltpu.VMEM((hp, t, t), F32), pltpu.VMEM((hp, t, t), F32)],
        compiler_params=_cparams(("parallel", "arbitrary")), name="mla_flash")(qt, k4, vt4)


def _cumsum_kernel(g_ref, o_ref, *, chunk):
    r = lax.broadcasted_iota(jnp.int32, (chunk, chunk), 0)
    c = lax.broadcasted_iota(jnp.int32, (chunk, chunk), 1)
    trilb = jnp.where(r >= c, 1.0, 0.0).astype(BF16)
    for r0 in range(0, g_ref.shape[0], chunk):
        g = g_ref[r0:r0 + chunk, :]
        g_hi = g.astype(BF16).astype(F32)
        g_mid = (g - g_hi).astype(BF16).astype(F32)
        g_lo = g - g_hi - g_mid
        o_ref[r0:r0 + chunk, :] = (jnp.dot(trilb, g_hi.astype(BF16), preferred_element_type=F32)
                                   + jnp.dot(trilb, g_mid.astype(BF16), preferred_element_type=F32)
                                   + jnp.dot(trilb, g_lo.astype(BF16), preferred_element_type=F32))


def chunk_cumsum(g, chunk, tm):
    M = g.shape[0]
    tm = min(tm, M)
    row = pl.BlockSpec((tm, LANES), lambda i: (i, 0))
    return pl.pallas_call(
        functools.partial(_cumsum_kernel, chunk=chunk), grid=(M // tm,), in_specs=[row], out_specs=row,
        out_shape=jax.ShapeDtypeStruct((M, LANES), F32),
        compiler_params=_cparams(("parallel",)), name="gdn_cumsum")(g)


def _bdot(a, b):
    return jnp.dot(a.astype(BF16), b.astype(BF16), preferred_element_type=F32)


def _conv_silu(x_ref, halo_ref, w_ref, xs_ref, first):
    rows = x_ref.shape[0]
    x = x_ref[...].astype(F32)
    xs_ref[0:8, :] = jnp.where(first, 0.0, halo_ref[...].astype(F32)[8:16, :])
    xs_ref[8:, :] = x
    w = w_ref[...]
    acc = x * w[GDN_CONV - 1:GDN_CONV, :]
    for s in range(1, GDN_CONV):
        acc = acc + xs_ref[8 - s:8 - s + rows, :] * w[GDN_CONV - 1 - s:GDN_CONV - s, :]
    return acc * jax.nn.sigmoid(acc)


def _l2n(y, scale):
    return y * (lax.rsqrt(jnp.sum(y * y, axis=-1, keepdims=True) + L2_EPS) * scale)


def _gdn_kernel(xq_ref, hq_ref, wq_ref, xk_ref, hk_ref, wk_ref, xv_ref, hv_ref, wv_ref, gate_ref,
                gcc_ref, bc_ref, gcr_ref, nw_ref, o_ref, s_ref, sq_ref, sk_ref, sv_ref, *, hb, rows, chunk):
    first = pl.program_id(1) == 0

    @pl.when(first)
    def _():
        s_ref[...] = jnp.zeros(s_ref.shape, F32)

    C = chunk
    ri = lax.broadcasted_iota(jnp.int32, (C, C), 0)
    ci = lax.broadcasted_iota(jnp.int32, (C, C), 1)
    lower = ri >= ci
    strict = ri > ci
    xor = ri ^ ci
    nw = nw_ref[...]
    n_levels = C.bit_length() - 1
    n_chunks = rows // C
    bodies = [(c, h) for c in range(n_chunks) for h in range(hb)]

    def rows_of(c):
        return slice(c * C, (c + 1) * C)

    def cols_of(h):
        return slice(h * HEAD_DIM, (h + 1) * HEAD_DIM)

    yq = _conv_silu(xq_ref, hq_ref, wq_ref, sq_ref, first)
    yk = _conv_silu(xk_ref, hk_ref, wk_ref, sk_ref, first)
    yv = _conv_silu(xv_ref, hv_ref, wv_ref, sv_ref, first)

    amats, rhs, qks, qds, tails, eglasts = {}, {}, {}, {}, {}, {}
    for b in bodies:
        c, h = b
        q = _l2n(yq[rows_of(c), cols_of(h)], HEAD_DIM ** -0.5)
        k = _l2n(yk[rows_of(c), cols_of(h)], 1.0)
        v = yv[rows_of(c), cols_of(h)]
        kt = k.T
        gcc = gcc_ref[rows_of(c), h:h + 1]
        bc = bc_ref[rows_of(c), h:h + 1]
        gcr = gcr_ref[h:h + 1, rows_of(c)]
        glast = gcr[:, C - 1:C]
        decay = jnp.exp(jnp.where(lower, gcc - gcr, -jnp.inf))
        egc = jnp.exp(gcc)
        kb = k * bc
        amats[b] = jnp.where(strict, _bdot(kb, kt) * decay, 0.0)
        rhs[b] = jnp.concatenate([v * bc, kb * egc], axis=1)
        qks[b] = _bdot(q, kt) * decay
        qds[b] = q * egc
        tails[b] = kt * jnp.exp(glast - gcr)
        eglasts[b] = jnp.exp(glast)

    tinv = {b: jnp.where(ri == ci, 1.0, 0.0) - jnp.where(strict & (xor == 1), amats[b], 0.0) for b in bodies}
    for level in range(1, n_levels):
        join = strict & ((xor >> level) == 1)
        mids = {b: _bdot(jnp.where(join, amats[b], 0.0), tinv[b]) for b in bodies}
        for b in bodies:
            tinv[b] = tinv[b] - _bdot(tinv[b], mids[b])
    xs = {b: _bdot(tinv[b], rhs[b]) for b in bodies}

    states = [s_ref[h] for h in range(hb)]
    for c in range(n_chunks):
        ws = [_bdot(jnp.concatenate([xs[(c, h)][:, HEAD_DIM:], qds[(c, h)]], axis=0), states[h]) for h in range(hb)]
        v_new = [xs[(c, h)][:, :HEAD_DIM] - ws[h][:C] for h in range(hb)]
        outs = [ws[h][C:] + _bdot(qks[(c, h)], v_new[h]) for h in range(hb)]
        states = [states[h] * eglasts[(c, h)] + _bdot(tails[(c, h)], v_new[h]) for h in range(hb)]
        for h in range(hb):
            o = outs[h]
            o = o * lax.rsqrt(jnp.mean(o * o, axis=-1, keepdims=True) + NORM_EPS) * nw
            gate = gate_ref[rows_of(c), cols_of(h)].astype(F32)
            o_ref[rows_of(c), cols_of(h)] = (o * (gate * jax.nn.sigmoid(gate))).astype(o_ref.dtype)
    for h in range(hb):
        s_ref[h] = states[h]


def gdn_mixer(proj, qkv_col0, gate_col0, conv_w, gcc, bc, gcr, nw, *, heads, hb, rows, chunk):
    S = proj.shape[0]
    tc = hb * HEAD_DIM
    gw = heads * HEAD_DIM
    halo_blocks = rows // 16
    assert qkv_col0 % tc == 0 and gate_col0 % tc == 0 and gw % tc == 0 and S % rows == 0 and rows % chunk == 0

    def section(col0, w_col0):
        cb, wb = col0 // tc, w_col0 // tc
        return [pl.BlockSpec((rows, tc), lambda g, r: (r, cb + g)),
                pl.BlockSpec((16, tc), lambda g, r: (jnp.maximum(r * halo_blocks - 1, 0), cb + g)),
                pl.BlockSpec((GDN_CONV, tc), lambda g, r: (0, wb + g))]

    in_specs = (section(qkv_col0, 0) + section(qkv_col0 + gw, gw) + section(qkv_col0 + 2 * gw, 2 * gw)
                + [pl.BlockSpec((rows, tc), lambda g, r: (r, gate_col0 // tc + g)),
                   pl.BlockSpec((None, rows, hb), lambda g, r: (g, r, 0)),
                   pl.BlockSpec((None, rows, hb), lambda g, r: (g, r, 0)),
                   pl.BlockSpec((None, hb, rows), lambda g, r: (g, 0, r)),
                   pl.BlockSpec((1, HEAD_DIM), lambda g, r: (0, 0))])
    stage = pltpu.VMEM((rows + 8, tc), F32)
    return pl.pallas_call(
        functools.partial(_gdn_kernel, hb=hb, rows=rows, chunk=chunk), grid=(heads // hb, S // rows),
        in_specs=in_specs,
        out_specs=pl.BlockSpec((rows, tc), lambda g, r: (r, g)),
        out_shape=jax.ShapeDtypeStruct((S, gw), BF16),
        scratch_shapes=[pltpu.VMEM((hb, HEAD_DIM, HEAD_DIM), F32), stage, stage, stage],
        compiler_params=_cparams(("parallel", "arbitrary")), name="gdn_mixer")(
            proj, proj, conv_w, proj, proj, conv_w, proj, proj, conv_w, proj, gcc, bc, gcr, nw.reshape(1, HEAD_DIM))


def _xattn_kernel(q_ref, kt_ref, v_ref, o_ref, *, heads):
    d = q_ref.shape[1] // heads
    cols = [slice(h * d, (h + 1) * d) for h in range(heads)]
    s = [jnp.dot(q_ref[:, c], kt_ref[c, :], preferred_element_type=F32) for c in cols]
    p = [jnp.exp2(sh - jnp.max(sh, axis=-1, keepdims=True)) for sh in s]
    l = [jnp.sum(ph, axis=-1, keepdims=True) for ph in p]
    o = [jnp.dot(ph.astype(BF16), v_ref[:, c], preferred_element_type=F32) for ph, c in zip(p, cols)]
    for c, oh, lh in zip(cols, o, l):
        o_ref[:, c] = (oh / lh).astype(o_ref.dtype)


def xattn_core(q, kt, v, *, heads, tq):
    S, W = q.shape
    M = v.shape[0]
    return pl.pallas_call(
        functools.partial(_xattn_kernel, heads=heads), grid=(S // tq,),
        in_specs=[pl.BlockSpec((tq, W), lambda i: (i, 0)),
                  pl.BlockSpec((W, M), lambda i: (0, 0)),
                  pl.BlockSpec((M, W), lambda i: (0, 0))],
        out_specs=pl.BlockSpec((tq, W), lambda i: (i, 0)),
        out_shape=jax.ShapeDtypeStruct((S, W), BF16),
        compiler_params=_cparams(("parallel",)), name="xattn_core")(q, kt, v)


def _block(x, mem, positions, attn_norm_w, w_in, mla_q_norm_w, mla_w_q_b, mla_kv_norm_w, mla_w_kv_b,
           gdn_conv_w, gdn_a_log, gdn_dt_bias, gdn_norm_w, w_out, xattn_norm_w, mem_norm_w,
           xattn_wq, xattn_wk, xattn_wv, xattn_wo, mlp_norm_w, mlp_w_up, mlp_w_down, final_norm_w):
    S, D = x.shape
    n_mix = w_out.shape[0]
    mla_heads = n_mix // 2 // HEAD_DIM
    gdn_heads = gdn_conv_w.shape[1] // (3 * HEAD_DIM)
    gw = gdn_heads * HEAD_DIM
    dq = MLA_NOPE + MLA_ROPE
    half = MLA_ROPE // 2

    o_kv = MLA_Q_RANK
    o_pe = o_kv + MLA_KV_RANK
    o_qkv = o_pe + MLA_ROPE
    o_a = o_qkv + 3 * gw
    o_b = o_a + gdn_heads
    o_gate = o_b + gdn_heads
    w_in_t = w_in.T
    c_qkv = o_pe
    c_gate = o_pe + 3 * gw
    n_proj = c_gate + (w_in.shape[1] - o_gate)
    tn_proj = 512

    skip1 = o_qkv - c_qkv
    skip2 = (o_gate - c_gate) - skip1
    unit = math.gcd(tn_proj, skip1, skip2)

    def proj_src_row(j):
        v = j * tn_proj
        past1 = (v >= c_qkv).astype(jnp.int32)
        past2 = (v >= c_gate).astype(jnp.int32)
        return (j * (tn_proj // unit) + past1 * (skip1 // unit) + past2 * (skip2 // unit)) * unit

    wq3 = mla_w_q_b.reshape(MLA_Q_RANK, mla_heads, dq)
    wq_perm = jnp.concatenate([wq3[:, :, :MLA_NOPE].reshape(MLA_Q_RANK, -1),
                               wq3[:, :, MLA_NOPE:MLA_NOPE + half].reshape(MLA_Q_RANK, -1),
                               wq3[:, :, MLA_NOPE + half:].reshape(MLA_Q_RANK, -1)], axis=1).astype(BF16)
    wkv3 = mla_w_kv_b.reshape(MLA_KV_RANK, mla_heads, MLA_NOPE + HEAD_DIM)
    wkv_perm = jnp.concatenate([wkv3[:, :, :MLA_NOPE].reshape(MLA_KV_RANK, -1),
                                wkv3[:, :, MLA_NOPE:].reshape(MLA_KV_RANK, -1)], axis=1).astype(BF16)

    inv_freq = ROPE_BASE ** (-jnp.arange(half, dtype=F32) / half)
    invf_row = jnp.tile(inv_freq, LANES // half).reshape(1, LANES)
    pad_l = jnp.zeros((MLA_ROPE,), F32)
    pad_r = jnp.zeros((LANES - MLA_ROPE - gdn_heads,), F32)
    alog_row = jnp.concatenate([pad_l, gdn_a_log.astype(F32), pad_r]).reshape(1, LANES)
    dtb_row = jnp.concatenate([pad_l, gdn_dt_bias.astype(F32), pad_r]).reshape(1, LANES)

    xn = rms_norm(x, attn_norm_w, BF16, 512)
    proj = proj_nt(xn, w_in_t, proj_src_row, n_proj, tm=1024, tn=tn_proj, name="proj_main")
    small = proj_small(xn, w_in_t, o_pe, MLA_ROPE, o_a, 2 * gdn_heads, tm=1024)
    aux, cos, sin = gates(small, positions.reshape(S, 1), invf_row, alog_row, dtb_row, 1024)

    t_att = 512
    qt = mla_q(proj, mla_q_norm_w, wq_perm, cos, sin, heads=mla_heads, tm=512, scale=dq ** -0.5 * LOG2E)
    k_full, vt4 = mla_kv(proj, o_kv // MLA_KV_RANK, mla_kv_norm_w, wkv_perm, aux, heads=mla_heads, tm=t_att)
    k4 = k_full.reshape(mla_heads, S // t_att, t_att, dq)
    mla_out = flash_attention(qt, k4, vt4, t=t_att, hp=4, kv_buffers=1)

    hb = 4
    gc_full = chunk_cumsum(aux, GDN_CHUNK, 1024)
    gc = gc_full[:, MLA_ROPE:MLA_ROPE + gdn_heads]
    beta = aux[:, MLA_ROPE + gdn_heads:MLA_ROPE + 2 * gdn_heads]
    gcc = gc.reshape(S, gdn_heads // hb, hb).transpose(1, 0, 2)
    bc = beta.reshape(S, gdn_heads // hb, hb).transpose(1, 0, 2)
    gcr = gc.T.reshape(gdn_heads // hb, hb, S)
    gdn_out = gdn_mixer(proj, c_qkv, c_gate, gdn_conv_w, gcc, bc, gcr, gdn_norm_w,
                        heads=gdn_heads, hb=hb, rows=512, chunk=GDN_CHUNK)

    h1, h1w, h1_ssq = matmul_res_norm([mla_out, gdn_out], w_out, x, xattn_norm_w, tm=1024, tn=512, name="w_out")

    memn = rms_norm(mem, mem_norm_w, BF16, 256)
    xd = D // XATTN_HEADS
    xq = matmul(h1w, xattn_wq, tm=1024, tn=512, out_dtype=BF16, out_scale=xd ** -0.5 * LOG2E,
                lhs_ssq=h1_ssq, name="xattn_q")
    xk = matmul(memn, xattn_wk, tm=256, tn=512, out_dtype=BF16, name="xattn_k")
    xv = matmul(memn, xattn_wv, tm=256, tn=512, out_dtype=BF16, name="xattn_v")
    xo = xattn_core(xq, xk.T, xv, heads=XATTN_HEADS, tq=512)
    h2, h2w, h2_ssq = matmul_res_norm([xo], xattn_wo, h1, mlp_norm_w, tm=1024, tn=512, name="xattn_o")

    up = matmul(h2w, mlp_w_up, tm=1024, tn=512, out_dtype=BF16, relu2=True, lhs_ssq=h2_ssq, name="mlp_up")
    down = matmul_kacc(up, mlp_w_down, tm=2048, tn=1024, tk=2048, row_chunk=256, name="mlp_down")
    return rms_norm(h2, final_norm_w, F32, 256, add=down)


def kernel(x, mem, positions, attn_norm_w, w_in, mla_q_norm_w, mla_w_q_b, mla_kv_norm_w, mla_w_kv_b, gdn_conv_w, gdn_a_log, gdn_dt_bias, gdn_norm_w, w_out, xattn_norm_w, mem_norm_w, xattn_wq, xattn_wk, xattn_wv, xattn_wo, mlp_norm_w, mlp_w_up, mlp_w_down, final_norm_w):
    assert x.shape[0] == 1 and attn_norm_w.shape[0] == 1, "single sequence, single layer"
    out = _block(x[0], mem[0], positions[0], attn_norm_w[0], w_in[0], mla_q_norm_w[0], mla_w_q_b[0],
                 mla_kv_norm_w[0], mla_w_kv_b[0], gdn_conv_w[0], gdn_a_log[0], gdn_dt_bias[0], gdn_norm_w[0],
                 w_out[0], xattn_norm_w[0], mem_norm_w[0], xattn_wq[0], xattn_wk[0], xattn_wv[0], xattn_wo[0],
                 mlp_norm_w[0], mlp_w_up[0], mlp_w_down[0], final_norm_w)
    return out[None]
```

```python
import functools
import math

import jax
import jax.numpy as jnp
from jax import lax
from jax.experimental import pallas as pl
from jax.experimental.pallas import tpu as pltpu

F32 = jnp.float32
BF16 = jnp.bfloat16

HEAD_DIM = 128
MLA_NOPE = 128
MLA_ROPE = 64
MLA_Q_RANK = 1024
MLA_KV_RANK = 512
GDN_CONV = 4
XATTN_HEADS = 4
ROPE_BASE = 10000.0
NORM_EPS = 1e-6
L2_EPS = 1e-6
LOG2E = math.log2(math.e)

V7X_VMEM_BYTES = 64 * 1024 * 1024
VMEM_LIMIT = 58 * 1024 * 1024
LANES = 128
MM_ROW_CHUNK = 2048

GDN_CHUNK = 128


def _cparams(sem):
    return pltpu.CompilerParams(dimension_semantics=sem, vmem_limit_bytes=VMEM_LIMIT)


def _rms_kernel(x_ref, w_ref, o_ref):
    x = x_ref[...].astype(F32)
    r = lax.rsqrt(jnp.mean(x * x, axis=-1, keepdims=True) + NORM_EPS)
    o_ref[...] = (x * r * w_ref[...]).astype(o_ref.dtype)


def _rms_add_kernel(x_ref, y_ref, w_ref, o_ref):
    x = x_ref[...] + y_ref[...]
    r = lax.rsqrt(jnp.mean(x * x, axis=-1, keepdims=True) + NORM_EPS)
    o_ref[...] = (x * r * w_ref[...]).astype(o_ref.dtype)


def rms_norm(x, w, out_dtype, tm, add=None):
    M, D = x.shape
    tm = min(tm, M)
    row = pl.BlockSpec((tm, D), lambda i: (i, 0))
    wspec = pl.BlockSpec((1, D), lambda i: (0, 0))
    if add is None:
        kern, specs, args = _rms_kernel, [row, wspec], (x, w.reshape(1, D))
    else:
        kern, specs, args = _rms_add_kernel, [row, row, wspec], (x, add, w.reshape(1, D))
    return pl.pallas_call(
        kern, grid=(M // tm,), in_specs=specs, out_specs=row,
        out_shape=jax.ShapeDtypeStruct((M, D), out_dtype),
        compiler_params=_cparams(("parallel",)), name="rms_norm")(*args)


def _mm_kernel(*refs, relu2, has_res, out_scale, ssq_dim, row_chunk):
    x_ref, w_ref = refs[0], refs[1]
    res_ref = refs[2] if has_res else None
    ssq_ref = refs[2 + has_res] if ssq_dim else None
    o_ref = refs[2 + has_res + bool(ssq_dim)]
    wb = w_ref[...].astype(BF16)
    for r0 in range(0, x_ref.shape[0], row_chunk):
        rows = slice(r0, r0 + row_chunk)
        r = jnp.dot(x_ref[rows, :], wb, preferred_element_type=F32)
        if ssq_dim:
            r = r * lax.rsqrt(ssq_ref[rows, 0:1] * (1.0 / ssq_dim) + NORM_EPS)
        if relu2:
            r = jnp.square(jnp.maximum(r, 0.0))
        if out_scale is not None:
            r = r * out_scale
        if has_res:
            r = r + res_ref[rows, :]
        o_ref[rows, :] = r.astype(o_ref.dtype)


def matmul(x, w, *, tm, tn, out_dtype, relu2=False, res=None, out_scale=None, lhs_ssq=None, row_chunk=MM_ROW_CHUNK,
           x_buffers=2, name="matmul"):
    M, K = x.shape
    K2, N = w.shape
    assert K == K2
    tm, tn = min(tm, M), min(tn, N)
    row_chunk = min(row_chunk, tm)
    assert M % tm == 0 and N % tn == 0 and tm % row_chunk == 0
    in_specs = [pl.BlockSpec((tm, K), lambda i, j: (i, 0), pipeline_mode=pl.Buffered(x_buffers)),
                pl.BlockSpec((K, tn), lambda i, j: (0, j))]
    args = [x, w]
    if res is not None:
        in_specs.append(pl.BlockSpec((tm, tn), lambda i, j: (i, j)))
        args.append(res)
    if lhs_ssq is not None:
        in_specs.append(pl.BlockSpec((tm, LANES), lambda i, j: (i, 0)))
        args.append(lhs_ssq)
    kern = functools.partial(_mm_kernel, relu2=relu2, has_res=res is not None, out_scale=out_scale,
                             ssq_dim=K if lhs_ssq is not None else 0, row_chunk=row_chunk)
    return pl.pallas_call(
        kern, grid=(M // tm, N // tn), in_specs=in_specs,
        out_specs=pl.BlockSpec((tm, tn), lambda i, j: (i, j)),
        out_shape=jax.ShapeDtypeStruct((M, N), out_dtype),
        compiler_params=_cparams(("parallel", "parallel")), name=name)(*args)


_NT_DIMS = (((1,), (1,)), ((), ()))


def _proj_nt_kernel(x_ref, wt_ref, cast_src_ref, o_ref, cast_dst_ref):
    wb = wt_ref[...].astype(BF16)
    o_ref[...] = lax.dot_general(x_ref[...], wb, _NT_DIMS, preferred_element_type=F32).astype(o_ref.dtype)
    cast_dst_ref[...] = cast_src_ref[...].astype(cast_dst_ref.dtype)


def _cast_blocks(w, n_steps):
    rows = w.shape[0]
    rb = 16
    while rows // rb > n_steps or rows % rb:
        rb += 16
    return rb, rows // rb


def proj_nt(x, wt, src_row_of_block, n_out, cast_w, *, tm, tn, name):
    M, K = x.shape
    ni, nj = M // tm, n_out // tn
    rb, n_blocks = _cast_blocks(cast_w, ni * nj)
    cast_spec = pl.BlockSpec((rb, cast_w.shape[1]), lambda i, j: (jnp.minimum(i * nj + j, n_blocks - 1), 0))
    return pl.pallas_call(
        _proj_nt_kernel, grid=(ni, nj),
        in_specs=[pl.BlockSpec((tm, K), lambda i, j: (i, 0)),
                  pl.BlockSpec((pl.Element(tn), pl.Element(K)), lambda i, j: (src_row_of_block(j), 0)),
                  cast_spec],
        out_specs=[pl.BlockSpec((tm, tn), lambda i, j: (i, j)), cast_spec],
        out_shape=[jax.ShapeDtypeStruct((M, n_out), BF16), jax.ShapeDtypeStruct(cast_w.shape, BF16)],
        compiler_params=_cparams(("arbitrary", "arbitrary")), name=name)(x, wt, cast_w)


def _proj_small_kernel(x_ref, wa_ref, wb_ref, o_ref):
    wa = wa_ref[...].astype(BF16)
    wb = wb_ref[...].astype(BF16)
    pad = jnp.zeros((LANES - wa.shape[0] - wb.shape[0], wa.shape[1]), BF16)
    w = jnp.concatenate([wa, wb, pad], axis=0)
    o_ref[...] = lax.dot_general(x_ref[...], w, _NT_DIMS, preferred_element_type=F32)


def proj_small(x, wt, row_a, n_a, row_b, n_b, *, tm):
    M, K = x.shape
    return pl.pallas_call(
        _proj_small_kernel, grid=(M // tm,),
        in_specs=[pl.BlockSpec((tm, K), lambda i: (i, 0)),
                  pl.BlockSpec((pl.Element(n_a), pl.Element(K)), lambda i: (row_a, 0)),
                  pl.BlockSpec((pl.Element(n_b), pl.Element(K)), lambda i: (row_b, 0))],
        out_specs=pl.BlockSpec((tm, LANES), lambda i: (i, 0)),
        out_shape=jax.ShapeDtypeStruct((M, LANES), F32),
        compiler_params=_cparams(("parallel",)), name="proj_small")(x, wt, wt)


def _mm_res_kernel(*refs, n_x, row_chunk):
    x_refs = refs[:n_x]
    w_ref, res_ref, nw_ref, h_ref, hw_ref, ssq_ref = refs[n_x:]
    j = pl.program_id(1)
    wb = w_ref[...].astype(BF16)
    nw = nw_ref[...]
    parts = []
    for r0 in range(0, res_ref.shape[0], row_chunk):
        rows = slice(r0, r0 + row_chunk)
        k0 = 0
        acc = res_ref[rows, :]
        for x_ref in x_refs:
            kx = x_ref.shape[1]
            acc = acc + jnp.dot(x_ref[rows, :], wb[k0:k0 + kx], preferred_element_type=F32)
            k0 += kx
        h_ref[rows, :] = acc
        hw_ref[rows, :] = (acc * nw).astype(hw_ref.dtype)
        parts.append(jnp.sum(acc * acc, axis=-1, keepdims=True))
    part = jnp.broadcast_to(jnp.concatenate(parts, axis=0), ssq_ref.shape)

    @pl.when(j == 0)
    def _():
        ssq_ref[...] = part

    @pl.when(j > 0)
    def _():
        ssq_ref[...] += part


def matmul_res_norm(xs, w, res, norm_w, *, tm, tn, name, row_chunk=MM_ROW_CHUNK):
    M = xs[0].shape[0]
    N = w.shape[1]
    ktot = sum(x.shape[1] for x in xs)
    tile = pl.BlockSpec((tm, tn), lambda i, j: (i, j))
    return pl.pallas_call(
        functools.partial(_mm_res_kernel, n_x=len(xs), row_chunk=min(row_chunk, tm)), grid=(M // tm, N // tn),
        in_specs=[pl.BlockSpec((tm, x.shape[1]), lambda i, j: (i, 0)) for x in xs]
        + [pl.BlockSpec((ktot, tn), lambda i, j: (0, j)), tile, pl.BlockSpec((1, tn), lambda i, j: (0, j))],
        out_specs=[tile, tile, pl.BlockSpec((tm, LANES), lambda i, j: (i, 0))],
        out_shape=[jax.ShapeDtypeStruct((M, N), F32), jax.ShapeDtypeStruct((M, N), BF16),
                   jax.ShapeDtypeStruct((M, LANES), F32)],
        compiler_params=_cparams(("parallel", "arbitrary")), name=name)(*xs, w, res, norm_w.reshape(1, N))


def _mm_kacc_kernel(x_ref, w_ref, o_ref, *, row_chunk):
    k = pl.program_id(2)
    wb = w_ref[...].astype(BF16)
    tm = x_ref.shape[0]

    @pl.when(k == 0)
    def _():
        for r in range(0, tm, row_chunk):
            rows = slice(r, r + row_chunk)
            o_ref[rows, :] = jnp.dot(x_ref[rows, :], wb, preferred_element_type=F32)

    @pl.when(k > 0)
    def _():
        for r in range(0, tm, row_chunk):
            rows = slice(r, r + row_chunk)
            o_ref[rows, :] += jnp.dot(x_ref[rows, :], wb, preferred_element_type=F32)


def matmul_kacc(x, w, *, tm, tn, tk, row_chunk, name):
    M, K = x.shape
    N = w.shape[1]
    tm, tn, tk = min(tm, M), min(tn, N), min(tk, K)
    assert M % tm == 0 and N % tn == 0 and K % tk == 0 and tm % row_chunk == 0
    return pl.pallas_call(
        functools.partial(_mm_kacc_kernel, row_chunk=row_chunk), grid=(M // tm, N // tn, K // tk),
        in_specs=[pl.BlockSpec((tm, tk), lambda i, j, k: (i, k)),
                  pl.BlockSpec((tk, tn), lambda i, j, k: (k, j))],
        out_specs=pl.BlockSpec((tm, tn), lambda i, j, k: (i, j)),
        out_shape=jax.ShapeDtypeStruct((M, N), F32),
        compiler_params=_cparams(("parallel", "parallel", "arbitrary")), name=name)(x, w)


def _gates_kernel(y_ref, pos_ref, invf_ref, alog_ref, dtb_ref, aux_ref, cos_ref, sin_ref):
    y = y_ref[...]
    ang = pos_ref[...].astype(F32) * invf_ref[...]
    c = jnp.cos(ang)
    s = jnp.sin(ang)
    cos_ref[...] = c
    sin_ref[...] = s
    lane = lax.broadcasted_iota(jnp.int32, y.shape, 1)
    half = MLA_ROPE // 2
    rot = jnp.where(lane < half, -pltpu.roll(y, LANES - half, 1), pltpu.roll(y, half, 1))
    roped = y * c + rot * s
    z = y + dtb_ref[...]
    softplus = jnp.maximum(z, 0.0) + jnp.log1p(jnp.exp(-jnp.abs(z)))
    g = -jnp.exp(alog_ref[...]) * softplus
    beta = jax.nn.sigmoid(y)
    aux_ref[...] = jnp.where(lane < MLA_ROPE, roped, jnp.where(lane < MLA_ROPE + 16, g, beta))


def gates(y, pos_col, invf_row, alog_row, dtb_row, tm):
    M = y.shape[0]
    tm = min(tm, M)
    row = pl.BlockSpec((tm, LANES), lambda i: (i, 0))
    one = pl.BlockSpec((1, LANES), lambda i: (0, 0))
    shp = jax.ShapeDtypeStruct((M, LANES), F32)
    return pl.pallas_call(
        _gates_kernel, grid=(M // tm,),
        in_specs=[row, pl.BlockSpec((tm, 1), lambda i: (i, 0)), one, one, one],
        out_specs=[row, row, row], out_shape=[shp, shp, shp],
        compiler_params=_cparams(("parallel",)), name="gates")(y, pos_col, invf_row, alog_row, dtb_row)


def _mla_q_kernel(cq_ref, nw_ref, wq_ref, cos_ref, sin_ref, o_ref, *, heads, scale):
    x = cq_ref[...].astype(F32)
    r = lax.rsqrt(jnp.mean(x * x, axis=-1, keepdims=True) + NORM_EPS)
    xn = (x * r * nw_ref[...]).astype(BF16)
    q = jnp.dot(xn, wq_ref[...], preferred_element_type=F32) * scale
    half = MLA_ROPE // 2
    n_nope = heads * MLA_NOPE
    n_half = heads * half
    reps = n_half // LANES
    c = jnp.concatenate([cos_ref[...]] * reps, axis=1)
    s = jnp.concatenate([sin_ref[...]] * reps, axis=1)
    p1 = q[:, n_nope:n_nope + n_half]
    p2 = q[:, n_nope + n_half:]
    r1t = (p1 * c - p2 * s).T.astype(BF16)
    r2t = (p2 * c + p1 * s).T.astype(BF16)
    for h in range(heads):
        o_ref[h, 0:MLA_NOPE, :] = q[:, h * MLA_NOPE:(h + 1) * MLA_NOPE].T.astype(BF16)
        o_ref[h, MLA_NOPE:MLA_NOPE + half, :] = r1t[h * half:(h + 1) * half, :]
        o_ref[h, MLA_NOPE + half:MLA_NOPE + 2 * half, :] = r2t[h * half:(h + 1) * half, :]


def mla_q(proj, nw, wq, cos, sin, *, heads, tm, scale):
    S = proj.shape[0]
    R = nw.shape[0]
    dq = MLA_NOPE + MLA_ROPE
    kern = functools.partial(_mla_q_kernel, heads=heads, scale=scale)
    return pl.pallas_call(
        kern, grid=(S // tm,),
        in_specs=[pl.BlockSpec((tm, R), lambda i: (i, 0)),
                  pl.BlockSpec((1, R), lambda i: (0, 0)),
                  pl.BlockSpec(wq.shape, lambda i: (0, 0)),
                  pl.BlockSpec((tm, LANES), lambda i: (i, 0)),
                  pl.BlockSpec((tm, LANES), lambda i: (i, 0))],
        out_specs=pl.BlockSpec((heads, dq, tm), lambda i: (0, 0, i)),
        out_shape=jax.ShapeDtypeStruct((heads, dq, S), BF16),
        compiler_params=_cparams(("parallel",)), name="mla_q")(proj, nw.reshape(1, R), wq, cos, sin)


def _mla_kv_kernel(ckv_ref, nw_ref, wkv_ref, aux_ref, k_ref, vt_ref, *, heads):
    x = ckv_ref[...].astype(F32)
    r = lax.rsqrt(jnp.mean(x * x, axis=-1, keepdims=True) + NORM_EPS)
    xn = (x * r * nw_ref[...]).astype(BF16)
    kv = jnp.dot(xn, wkv_ref[...], preferred_element_type=F32)
    kpe = aux_ref[...][:, 0:MLA_ROPE].astype(BF16)
    n_nope = heads * MLA_NOPE
    for h in range(heads):
        k_ref[h, :, 0:MLA_NOPE] = kv[:, h * MLA_NOPE:(h + 1) * MLA_NOPE].astype(BF16)
        k_ref[h, :, MLA_NOPE:MLA_NOPE + MLA_ROPE] = kpe
        vt_ref[h, 0] = kv[:, n_nope + h * HEAD_DIM:n_nope + (h + 1) * HEAD_DIM].T.astype(BF16)


def mla_kv(proj, col_block, nw, wkv, aux, *, heads, tm):
    S = proj.shape[0]
    R = nw.shape[0]
    dq = MLA_NOPE + MLA_ROPE
    kern = functools.partial(_mla_kv_kernel, heads=heads)
    return pl.pallas_call(
        kern, grid=(S // tm,),
        in_specs=[pl.BlockSpec((tm, R), lambda i: (i, col_block)),
                  pl.BlockSpec((1, R), lambda i: (0, 0)),
                  pl.BlockSpec(wkv.shape, lambda i: (0, 0)),
                  pl.BlockSpec((tm, LANES), lambda i: (i, 0))],
        out_specs=[pl.BlockSpec((heads, tm, dq), lambda i: (0, i, 0)),
                   pl.BlockSpec((heads, 1, HEAD_DIM, tm), lambda i: (0, i, 0, 0))],
        out_shape=[jax.ShapeDtypeStruct((heads, S, dq), BF16),
                   jax.ShapeDtypeStruct((heads, S // tm, HEAD_DIM, tm), BF16)],
        compiler_params=_cparams(("parallel",)), name="mla_kv")(proj, nw.reshape(1, R), wkv, aux)


def _flash_kernel(qt_ref, k_ref, vt_ref, o_ref, acc_ref, m_ref, l_ref, sa_ref, sb_ref, *, t, hp):
    i = pl.program_id(1)
    m_ref[...] = jnp.full(m_ref.shape, -jnp.inf, F32)
    l_ref[...] = jnp.zeros(l_ref.shape, F32)
    acc_ref[...] = jnp.zeros(acc_ref.shape, F32)
    heads = range(hp)

    def scores(j, s_ref):
        for h in heads:
            s_ref[h] = jnp.dot(k_ref[h, j], qt_ref[h], preferred_element_type=F32)

    def softmax_pv(j, s_ref, masked):
        s = [s_ref[h] for h in heads]
        if masked:
            kpos = lax.broadcasted_iota(jnp.int32, (t, t), 0)
            qpos = lax.broadcasted_iota(jnp.int32, (t, t), 1)
            s = [jnp.where(kpos <= qpos, sh, -jnp.inf) for sh in s]
        m_old = [m_ref[h] for h in heads]
        m_new = [jnp.maximum(m_old[h], jnp.max(s[h], axis=0, keepdims=True)) for h in heads]
        alpha = [jnp.exp2(m_old[h] - m_new[h]) for h in heads]
        p = [jnp.exp2(s[h] - m_new[h]) for h in heads]
        for h in heads:
            l_ref[h] = alpha[h] * l_ref[h] + jnp.sum(p[h], axis=0, keepdims=True)
            m_ref[h] = m_new[h]
        pv = [jnp.dot(vt_ref[h, j], p[h].astype(BF16), preferred_element_type=F32) for h in heads]
        for h in heads:
            acc_ref[h] = alpha[h] * acc_ref[h] + pv[h]

    scores(0, sa_ref)

    def body(jj, carry):
        j = 2 * jj
        scores(j + 1, sb_ref)
        softmax_pv(j, sa_ref, False)
        scores(j + 2, sa_ref)
        softmax_pv(j + 1, sb_ref, False)
        return carry

    lax.fori_loop(0, i // 2, body, 0)

    @pl.when(i % 2 == 0)
    def _():
        softmax_pv(i, sa_ref, True)

    @pl.when(i % 2 == 1)
    def _():
        scores(i, sb_ref)
        softmax_pv(i - 1, sa_ref, False)
        softmax_pv(i, sb_ref, True)

    for h in range(hp):
        o_ref[:, h * HEAD_DIM:(h + 1) * HEAD_DIM] = (acc_ref[h] / l_ref[h]).T.astype(o_ref.dtype)


def flash_attention(qt, k4, vt4, *, t, hp, kv_buffers=2):
    H, dq, S = qt.shape
    nk = S // t
    kern = functools.partial(_flash_kernel, t=t, hp=hp)
    return pl.pallas_call(
        kern, grid=(H // hp, nk),
        in_specs=[pl.BlockSpec((hp, dq, t), lambda g, i: (g, 0, i)),
                  pl.BlockSpec((hp, nk, t, dq), lambda g, i: (g, 0, 0, 0), pipeline_mode=pl.Buffered(kv_buffers)),
                  pl.BlockSpec((hp, nk, HEAD_DIM, t), lambda g, i: (g, 0, 0, 0),
                               pipeline_mode=pl.Buffered(kv_buffers))],
        out_specs=pl.BlockSpec((t, hp * HEAD_DIM), lambda g, i: (i, g)),
        out_shape=jax.ShapeDtypeStruct((S, H * HEAD_DIM), BF16),
        scratch_shapes=[pltpu.VMEM((hp, HEAD_DIM, t), F32), pltpu.VMEM((hp, 1, t), F32),
                        pltpu.VMEM((hp, 1, t), F32), pltpu.VMEM((hp, t, t), F32), pltpu.VMEM((hp, t, t), F32)],
        compiler_params=_cparams(("parallel", "arbitrary")), name="mla_flash")(qt, k4, vt4)


def _cumsum_kernel(g_ref, o_ref, *, chunk):
    r = lax.broadcasted_iota(jnp.int32, (chunk, chunk), 0)
    c = lax.broadcasted_iota(jnp.int32, (chunk, chunk), 1)
    trilb = jnp.where(r >= c, 1.0, 0.0).astype(BF16)
    for r0 in range(0, g_ref.shape[0], chunk):
        g = g_ref[r0:r0 + chunk, :]
        g_hi = g.astype(BF16).astype(F32)
        g_mid = (g - g_hi).astype(BF16).astype(F32)
        g_lo = g - g_hi - g_mid
        o_ref[r0:r0 + chunk, :] = (jnp.dot(trilb, g_hi.astype(BF16), preferred_element_type=F32)
                                   + jnp.dot(trilb, g_mid.astype(BF16), preferred_element_type=F32)
                                   + jnp.dot(trilb, g_lo.astype(BF16), preferred_element_type=F32))


def chunk_cumsum(g, chunk, tm):
    M = g.shape[0]
    tm = min(tm, M)
    row = pl.BlockSpec((tm, LANES), lambda i: (i, 0))
    return pl.pallas_call(
        functools.partial(_cumsum_kernel, chunk=chunk), grid=(M // tm,), in_specs=[row], out_specs=row,
        out_shape=jax.ShapeDtypeStruct((M, LANES), F32),
        compiler_params=_cparams(("parallel",)), name="gdn_cumsum")(g)


def _bdot(a, b):
    return jnp.dot(a.astype(BF16), b.astype(BF16), preferred_element_type=F32)


def _conv_silu(x_ref, halo_ref, w_ref, xs_ref, first):
    rows = x_ref.shape[0]
    x = x_ref[...].astype(F32)
    xs_ref[0:8, :] = jnp.where(first, 0.0, halo_ref[...].astype(F32)[8:16, :])
    xs_ref[8:, :] = x
    w = w_ref[...]
    acc = x * w[GDN_CONV - 1:GDN_CONV, :]
    for s in range(1, GDN_CONV):
        acc = acc + xs_ref[8 - s:8 - s + rows, :] * w[GDN_CONV - 1 - s:GDN_CONV - s, :]
    return acc * jax.nn.sigmoid(acc)


def _l2n(y, scale):
    return y * (lax.rsqrt(jnp.sum(y * y, axis=-1, keepdims=True) + L2_EPS) * scale)


def _gdn_kernel(xq_ref, hq_ref, wq_ref, xk_ref, hk_ref, wk_ref, xv_ref, hv_ref, wv_ref, gate_ref,
                gcc_ref, bc_ref, gcr_ref, nw_ref, o_ref, s_ref, sq_ref, sk_ref, sv_ref, *, hb, rows, chunk):
    first = pl.program_id(1) == 0

    @pl.when(first)
    def _():
        s_ref[...] = jnp.zeros(s_ref.shape, F32)

    C = chunk
    ri = lax.broadcasted_iota(jnp.int32, (C, C), 0)
    ci = lax.broadcasted_iota(jnp.int32, (C, C), 1)
    lower = ri >= ci
    strict = ri > ci
    xor = ri ^ ci
    nw = nw_ref[...]
    n_levels = C.bit_length() - 1
    n_chunks = rows // C
    bodies = [(c, h) for c in range(n_chunks) for h in range(hb)]

    def rows_of(c):
        return slice(c * C, (c + 1) * C)

    def cols_of(h):
        return slice(h * HEAD_DIM, (h + 1) * HEAD_DIM)

    yq = _conv_silu(xq_ref, hq_ref, wq_ref, sq_ref, first)
    yk = _conv_silu(xk_ref, hk_ref, wk_ref, sk_ref, first)
    yv = _conv_silu(xv_ref, hv_ref, wv_ref, sv_ref, first)

    amats, rhs, qks, qds, tails, eglasts = {}, {}, {}, {}, {}, {}
    for b in bodies:
        c, h = b
        q = _l2n(yq[rows_of(c), cols_of(h)], HEAD_DIM ** -0.5)
        k = _l2n(yk[rows_of(c), cols_of(h)], 1.0)
        v = yv[rows_of(c), cols_of(h)]
        kt = k.T
        gcc = gcc_ref[rows_of(c), h:h + 1]
        bc = bc_ref[rows_of(c), h:h + 1]
        gcr = gcr_ref[h:h + 1, rows_of(c)]
        glast = gcr[:, C - 1:C]
        decay = jnp.exp(jnp.where(lower, gcc - gcr, -jnp.inf))
        egc = jnp.exp(gcc)
        kb = k * bc
        amats[b] = jnp.where(strict, _bdot(kb, kt) * decay, 0.0)
        rhs[b] = jnp.concatenate([v * bc, kb * egc], axis=1)
        qks[b] = _bdot(q, kt) * decay
        qds[b] = q * egc
        tails[b] = kt * jnp.exp(glast - gcr)
        eglasts[b] = jnp.exp(glast)

    tinv = {b: jnp.where(ri == ci, 1.0, 0.0) - jnp.where(strict & (xor == 1), amats[b], 0.0) for b in bodies}
    for level in range(1, n_levels):
        join = strict & ((xor >> level) == 1)
        mids = {b: _bdot(jnp.where(join, amats[b], 0.0), tinv[b]) for b in bodies}
        for b in bodies:
            tinv[b] = tinv[b] - _bdot(tinv[b], mids[b])
    xs = {b: _bdot(tinv[b], rhs[b]) for b in bodies}

    states = [s_ref[h] for h in range(hb)]
    for c in range(n_chunks):
        ws = [_bdot(jnp.concatenate([xs[(c, h)][:, HEAD_DIM:], qds[(c, h)]], axis=0), states[h]) for h in range(hb)]
        v_new = [xs[(c, h)][:, :HEAD_DIM] - ws[h][:C] for h in range(hb)]
        outs = [ws[h][C:] + _bdot(qks[(c, h)], v_new[h]) for h in range(hb)]
        states = [states[h] * eglasts[(c, h)] + _bdot(tails[(c, h)], v_new[h]) for h in range(hb)]
        for h in range(hb):
            o = outs[h]
            o = o * lax.rsqrt(jnp.mean(o * o, axis=-1, keepdims=True) + NORM_EPS) * nw
            gate = gate_ref[rows_of(c), cols_of(h)].astype(F32)
            o_ref[rows_of(c), cols_of(h)] = (o * (gate * jax.nn.sigmoid(gate))).astype(o_ref.dtype)
    for h in range(hb):
        s_ref[h] = states[h]


def gdn_mixer(proj, qkv_col0, gate_col0, conv_w, gcc, bc, gcr, nw, *, heads, hb, rows, chunk):
    S = proj.shape[0]
    tc = hb * HEAD_DIM
    gw = heads * HEAD_DIM
    halo_blocks = rows // 16
    assert qkv_col0 % tc == 0 and gate_col0 % tc == 0 and gw % tc == 0 and S % rows == 0 and rows % chunk == 0

    def section(col0, w_col0):
        cb, wb = col0 // tc, w_col0 // tc
        return [pl.BlockSpec((rows, tc), lambda g, r: (r, cb + g)),
                pl.BlockSpec((16, tc), lambda g, r: (jnp.maximum(r * halo_blocks - 1, 0), cb + g)),
                pl.BlockSpec((GDN_CONV, tc), lambda g, r: (0, wb + g))]

    in_specs = (section(qkv_col0, 0) + section(qkv_col0 + gw, gw) + section(qkv_col0 + 2 * gw, 2 * gw)
                + [pl.BlockSpec((rows, tc), lambda g, r: (r, gate_col0 // tc + g)),
                   pl.BlockSpec((None, rows, hb), lambda g, r: (g, r, 0)),
                   pl.BlockSpec((None, rows, hb), lambda g, r: (g, r, 0)),
                   pl.BlockSpec((None, hb, rows), lambda g, r: (g, 0, r)),
                   pl.BlockSpec((1, HEAD_DIM), lambda g, r: (0, 0))])
    stage = pltpu.VMEM((rows + 8, tc), F32)
    return pl.pallas_call(
        functools.partial(_gdn_kernel, hb=hb, rows=rows, chunk=chunk), grid=(heads // hb, S // rows),
        in_specs=in_specs,
        out_specs=pl.BlockSpec((rows, tc), lambda g, r: (r, g)),
        out_shape=jax.ShapeDtypeStruct((S, gw), BF16),
        scratch_shapes=[pltpu.VMEM((hb, HEAD_DIM, HEAD_DIM), F32), stage, stage, stage],
        compiler_params=_cparams(("parallel", "arbitrary")), name="gdn_mixer")(
            proj, proj, conv_w, proj, proj, conv_w, proj, proj, conv_w, proj, gcc, bc, gcr, nw.reshape(1, HEAD_DIM))


def _xattn_kernel(q_ref, kt_ref, v_ref, o_ref, *, heads):
    d = q_ref.shape[1] // heads
    cols = [slice(h * d, (h + 1) * d) for h in range(heads)]
    s = [jnp.dot(q_ref[:, c], kt_ref[c, :], preferred_element_type=F32) for c in cols]
    p = [jnp.exp2(sh - jnp.max(sh, axis=-1, keepdims=True)) for sh in s]
    l = [jnp.sum(ph, axis=-1, keepdims=True) for ph in p]
    o = [jnp.dot(ph.astype(BF16), v_ref[:, c], preferred_element_type=F32) for ph, c in zip(p, cols)]
    for c, oh, lh in zip(cols, o, l):
        o_ref[:, c] = (oh / lh).astype(o_ref.dtype)


def xattn_core(q, kt, v, *, heads, tq):
    S, W = q.shape
    M = v.shape[0]
    return pl.pallas_call(
        functools.partial(_xattn_kernel, heads=heads), grid=(S // tq,),
        in_specs=[pl.BlockSpec((tq, W), lambda i: (i, 0)),
                  pl.BlockSpec((W, M), lambda i: (0, 0)),
                  pl.BlockSpec((M, W), lambda i: (0, 0))],
        out_specs=pl.BlockSpec((tq, W), lambda i: (i, 0)),
        out_shape=jax.ShapeDtypeStruct((S, W), BF16),
        compiler_params=_cparams(("parallel",)), name="xattn_core")(q, kt, v)


def _block(x, mem, positions, attn_norm_w, w_in, mla_q_norm_w, mla_w_q_b, mla_kv_norm_w, mla_w_kv_b,
           gdn_conv_w, gdn_a_log, gdn_dt_bias, gdn_norm_w, w_out, xattn_norm_w, mem_norm_w,
           xattn_wq, xattn_wk, xattn_wv, xattn_wo, mlp_norm_w, mlp_w_up, mlp_w_down, final_norm_w):
    S, D = x.shape
    n_mix = w_out.shape[0]
    mla_heads = n_mix // 2 // HEAD_DIM
    gdn_heads = gdn_conv_w.shape[1] // (3 * HEAD_DIM)
    gw = gdn_heads * HEAD_DIM
    dq = MLA_NOPE + MLA_ROPE
    half = MLA_ROPE // 2

    o_kv = MLA_Q_RANK
    o_pe = o_kv + MLA_KV_RANK
    o_qkv = o_pe + MLA_ROPE
    o_a = o_qkv + 3 * gw
    o_b = o_a + gdn_heads
    o_gate = o_b + gdn_heads
    w_in_t = w_in.T
    c_qkv = o_pe
    c_gate = o_pe + 3 * gw
    n_proj = c_gate + (w_in.shape[1] - o_gate)
    tn_proj = 512

    skip1 = o_qkv - c_qkv
    skip2 = (o_gate - c_gate) - skip1
    unit = math.gcd(tn_proj, skip1, skip2)

    def proj_src_row(j):
        v = j * tn_proj
        past1 = (v >= c_qkv).astype(jnp.int32)
        past2 = (v >= c_gate).astype(jnp.int32)
        return (j * (tn_proj // unit) + past1 * (skip1 // unit) + past2 * (skip2 // unit)) * unit

    wq3 = mla_w_q_b.reshape(MLA_Q_RANK, mla_heads, dq)
    wq_perm = jnp.concatenate([wq3[:, :, :MLA_NOPE].reshape(MLA_Q_RANK, -1),
                               wq3[:, :, MLA_NOPE:MLA_NOPE + half].reshape(MLA_Q_RANK, -1),
                               wq3[:, :, MLA_NOPE + half:].reshape(MLA_Q_RANK, -1)], axis=1).astype(BF16)
    wkv3 = mla_w_kv_b.reshape(MLA_KV_RANK, mla_heads, MLA_NOPE + HEAD_DIM)
    wkv_perm = jnp.concatenate([wkv3[:, :, :MLA_NOPE].reshape(MLA_KV_RANK, -1),
                                wkv3[:, :, MLA_NOPE:].reshape(MLA_KV_RANK, -1)], axis=1).astype(BF16)

    inv_freq = ROPE_BASE ** (-jnp.arange(half, dtype=F32) / half)
    invf_row = jnp.tile(inv_freq, LANES // half).reshape(1, LANES)
    pad_l = jnp.zeros((MLA_ROPE,), F32)
    pad_r = jnp.zeros((LANES - MLA_ROPE - gdn_heads,), F32)
    alog_row = jnp.concatenate([pad_l, gdn_a_log.astype(F32), pad_r]).reshape(1, LANES)
    dtb_row = jnp.concatenate([pad_l, gdn_dt_bias.astype(F32), pad_r]).reshape(1, LANES)

    xn = rms_norm(x, attn_norm_w, BF16, 512)
    proj, w_up_bf16 = proj_nt(xn, w_in_t, proj_src_row, n_proj, mlp_w_up, tm=1024, tn=tn_proj, name="proj_main")
    small = proj_small(xn, w_in_t, o_pe, MLA_ROPE, o_a, 2 * gdn_heads, tm=1024)
    aux, cos, sin = gates(small, positions.reshape(S, 1), invf_row, alog_row, dtb_row, 1024)

    t_att = 512
    qt = mla_q(proj, mla_q_norm_w, wq_perm, cos, sin, heads=mla_heads, tm=512, scale=dq ** -0.5 * LOG2E)
    k_full, vt4 = mla_kv(proj, o_kv // MLA_KV_RANK, mla_kv_norm_w, wkv_perm, aux, heads=mla_heads, tm=t_att)
    k4 = k_full.reshape(mla_heads, S // t_att, t_att, dq)
    mla_out = flash_attention(qt, k4, vt4, t=t_att, hp=4, kv_buffers=1)

    hb = 4
    gc_full = chunk_cumsum(aux, GDN_CHUNK, 1024)
    gc = gc_full[:, MLA_ROPE:MLA_ROPE + gdn_heads]
    beta = aux[:, MLA_ROPE + gdn_heads:MLA_ROPE + 2 * gdn_heads]
    gcc = gc.reshape(S, gdn_heads // hb, hb).transpose(1, 0, 2)
    bc = beta.reshape(S, gdn_heads // hb, hb).transpose(1, 0, 2)
    gcr = gc.T.reshape(gdn_heads // hb, hb, S)
    gdn_out = gdn_mixer(proj, c_qkv, c_gate, gdn_conv_w, gcc, bc, gcr, gdn_norm_w,
                        heads=gdn_heads, hb=hb, rows=512, chunk=GDN_CHUNK)

    h1, h1w, h1_ssq = matmul_res_norm([mla_out, gdn_out], w_out, x, xattn_norm_w, tm=1024, tn=512, name="w_out")

    memn = rms_norm(mem, mem_norm_w, BF16, 256)
    xd = D // XATTN_HEADS
    xq = matmul(h1w, xattn_wq, tm=1024, tn=512, out_dtype=BF16, out_scale=xd ** -0.5 * LOG2E,
                lhs_ssq=h1_ssq, name="xattn_q")
    xk = matmul(memn, xattn_wk, tm=256, tn=512, out_dtype=BF16, name="xattn_k")
    xv = matmul(memn, xattn_wv, tm=256, tn=512, out_dtype=BF16, name="xattn_v")
    xo = xattn_core(xq, xk.T, xv, heads=XATTN_HEADS, tq=512)
    h2, h2w, h2_ssq = matmul_res_norm([xo], xattn_wo, h1, mlp_norm_w, tm=1024, tn=512, name="xattn_o")

    up = matmul(h2w, w_up_bf16, tm=1024, tn=1024, out_dtype=BF16, relu2=True, lhs_ssq=h2_ssq, name="mlp_up")
    down = matmul_kacc(up, mlp_w_down, tm=2048, tn=1024, tk=2048, row_chunk=256, name="mlp_down")
    return rms_norm(h2, final_norm_w, F32, 256, add=down)


def kernel(x, mem, positions, attn_norm_w, w_in, mla_q_norm_w, mla_w_q_b, mla_kv_norm_w, mla_w_kv_b, gdn_conv_w, gdn_a_log, gdn_dt_bias, gdn_norm_w, w_out, xattn_norm_w, mem_norm_w, xattn_wq, xattn_wk, xattn_wv, xattn_wo, mlp_norm_w, mlp_w_up, mlp_w_down, final_norm_w):
    assert x.shape[0] == 1 and attn_norm_w.shape[0] == 1, "single sequence, single layer"
    out = _block(x[0], mem[0], positions[0], attn_norm_w[0], w_in[0], mla_q_norm_w[0], mla_w_q_b[0],
                 mla_kv_norm_w[0], mla_w_kv_b[0], gdn_conv_w[0], gdn_a_log[0], gdn_dt_bias[0], gdn_norm_w[0],
                 w_out[0], xattn_norm_w[0], mem_norm_w[0], xattn_wq[0], xattn_wk[0], xattn_wv[0], xattn_wo[0],
                 mlp_norm_w[0], mlp_w_up[0], mlp_w_down[0], final_norm_w)
    return out[None]
```

```python
import functools
import math

import jax
import jax.numpy as jnp
from jax import lax
from jax.experimental import pallas as pl
from jax.experimental.pallas import tpu as pltpu

F32 = jnp.float32
BF16 = jnp.bfloat16

HEAD_DIM = 128
MLA_NOPE = 128
MLA_ROPE = 64
MLA_Q_RANK = 1024
MLA_KV_RANK = 512
GDN_CONV = 4
XATTN_HEADS = 4
ROPE_BASE = 10000.0
NORM_EPS = 1e-6
L2_EPS = 1e-6
LOG2E = math.log2(math.e)

V7X_VMEM_BYTES = 64 * 1024 * 1024
VMEM_LIMIT = 58 * 1024 * 1024
assert VMEM_LIMIT < V7X_VMEM_BYTES
LANES = 128
KACC_ROW_CHUNK = 256

GDN_CHUNK = 128


def _cparams(sem):
    return pltpu.CompilerParams(dimension_semantics=sem, vmem_limit_bytes=VMEM_LIMIT)


def _rms_kernel(x_ref, w_ref, o_ref):
    x = x_ref[...].astype(F32)
    r = lax.rsqrt(jnp.mean(x * x, axis=-1, keepdims=True) + NORM_EPS)
    o_ref[...] = (x * r * w_ref[...]).astype(o_ref.dtype)


def _rms_add_kernel(x_ref, y_ref, w_ref, o_ref):
    x = x_ref[...] + y_ref[...]
    r = lax.rsqrt(jnp.mean(x * x, axis=-1, keepdims=True) + NORM_EPS)
    o_ref[...] = (x * r * w_ref[...]).astype(o_ref.dtype)


def rms_norm(x, w, out_dtype, tm, add=None):
    M, D = x.shape
    tm = min(tm, M)
    row = pl.BlockSpec((tm, D), lambda i: (i, 0))
    wspec = pl.BlockSpec((1, D), lambda i: (0, 0))
    if add is None:
        kern, specs, args = _rms_kernel, [row, wspec], (x, w.reshape(1, D))
    else:
        kern, specs, args = _rms_add_kernel, [row, row, wspec], (x, add, w.reshape(1, D))
    return pl.pallas_call(
        kern, grid=(M // tm,), in_specs=specs, out_specs=row,
        out_shape=jax.ShapeDtypeStruct((M, D), out_dtype),
        compiler_params=_cparams(("parallel",)), name="rms_norm")(*args)


def _mm_kernel(*refs, relu2, has_res, out_scale, ssq_dim):
    x_ref, w_ref = refs[0], refs[1]
    res_ref = refs[2] if has_res else None
    ssq_ref = refs[2 + has_res] if ssq_dim else None
    o_ref = refs[2 + has_res + bool(ssq_dim)]
    r = jnp.dot(x_ref[...], w_ref[...].astype(BF16), preferred_element_type=F32)
    if ssq_dim:
        r = r * lax.rsqrt(ssq_ref[:, 0:1] * (1.0 / ssq_dim) + NORM_EPS)
    if relu2:
        r = jnp.square(jnp.maximum(r, 0.0))
    if out_scale is not None:
        r = r * out_scale
    if has_res:
        r = r + res_ref[...]
    o_ref[...] = r.astype(o_ref.dtype)


def matmul(x, w, *, tm, tn, out_dtype, relu2=False, res=None, out_scale=None, lhs_ssq=None, name="matmul"):
    M, K = x.shape
    K2, N = w.shape
    assert K == K2
    tm, tn = min(tm, M), min(tn, N)
    assert M % tm == 0 and N % tn == 0
    in_specs = [pl.BlockSpec((tm, K), lambda i, j: (i, 0)),
                pl.BlockSpec((K, tn), lambda i, j: (0, j))]
    args = [x, w]
    if res is not None:
        in_specs.append(pl.BlockSpec((tm, tn), lambda i, j: (i, j)))
        args.append(res)
    if lhs_ssq is not None:
        in_specs.append(pl.BlockSpec((tm, LANES), lambda i, j: (i, 0)))
        args.append(lhs_ssq)
    kern = functools.partial(_mm_kernel, relu2=relu2, has_res=res is not None, out_scale=out_scale,
                             ssq_dim=K if lhs_ssq is not None else 0)
    return pl.pallas_call(
        kern, grid=(M // tm, N // tn), in_specs=in_specs,
        out_specs=pl.BlockSpec((tm, tn), lambda i, j: (i, j)),
        out_shape=jax.ShapeDtypeStruct((M, N), out_dtype),
        compiler_params=_cparams(("parallel", "parallel")), name=name)(*args)


_NT_DIMS = (((1,), (1,)), ((), ()))


def _proj_nt_kernel(x_ref, wt_ref, cast_src_ref, o_ref, cast_dst_ref):
    wb = wt_ref[...].astype(BF16)
    o_ref[...] = lax.dot_general(x_ref[...], wb, _NT_DIMS, preferred_element_type=F32).astype(o_ref.dtype)
    cast_dst_ref[...] = cast_src_ref[...].astype(cast_dst_ref.dtype)


def _cast_blocks(w, n_steps):
    rows = w.shape[0]
    rb = 16
    while rows // rb > n_steps or rows % rb:
        rb += 16
    return rb, rows // rb


def proj_nt(x, wt, src_row_of_block, n_out, cast_w, *, tm, tn, name):
    M, K = x.shape
    ni, nj = M // tm, n_out // tn
    rb, n_blocks = _cast_blocks(cast_w, ni * nj)
    cast_spec = pl.BlockSpec((rb, cast_w.shape[1]), lambda i, j: (jnp.minimum(i * nj + j, n_blocks - 1), 0))
    return pl.pallas_call(
        _proj_nt_kernel, grid=(ni, nj),
        in_specs=[pl.BlockSpec((tm, K), lambda i, j: (i, 0)),
                  pl.BlockSpec((pl.Element(tn), pl.Element(K)), lambda i, j: (src_row_of_block(j), 0)),
                  cast_spec],
        out_specs=[pl.BlockSpec((tm, tn), lambda i, j: (i, j)), cast_spec],
        out_shape=[jax.ShapeDtypeStruct((M, n_out), BF16), jax.ShapeDtypeStruct(cast_w.shape, BF16)],
        compiler_params=_cparams(("arbitrary", "arbitrary")), name=name)(x, wt, cast_w)


def _proj_small_kernel(x_ref, wa_ref, wb_ref, o_ref):
    wa = wa_ref[...].astype(BF16)
    wb = wb_ref[...].astype(BF16)
    pad = jnp.zeros((LANES - wa.shape[0] - wb.shape[0], wa.shape[1]), BF16)
    w = jnp.concatenate([wa, wb, pad], axis=0)
    o_ref[...] = lax.dot_general(x_ref[...], w, _NT_DIMS, preferred_element_type=F32)


def proj_small(x, wt, row_a, n_a, row_b, n_b, *, tm):
    M, K = x.shape
    return pl.pallas_call(
        _proj_small_kernel, grid=(M // tm,),
        in_specs=[pl.BlockSpec((tm, K), lambda i: (i, 0)),
                  pl.BlockSpec((pl.Element(n_a), pl.Element(K)), lambda i: (row_a, 0)),
                  pl.BlockSpec((pl.Element(n_b), pl.Element(K)), lambda i: (row_b, 0))],
        out_specs=pl.BlockSpec((tm, LANES), lambda i: (i, 0)),
        out_shape=jax.ShapeDtypeStruct((M, LANES), F32),
        compiler_params=_cparams(("parallel",)), name="proj_small")(x, wt, wt)


def _mm_res_kernel(*refs, n_x):
    x_refs = refs[:n_x]
    w_ref, res_ref, nw_ref, h_ref, hw_ref, ssq_ref = refs[n_x:]
    j = pl.program_id(1)
    wb = w_ref[...].astype(BF16)
    k0 = 0
    acc = res_ref[...]
    for x_ref in x_refs:
        kx = x_ref.shape[1]
        acc = acc + jnp.dot(x_ref[...], wb[k0:k0 + kx], preferred_element_type=F32)
        k0 += kx
    h_ref[...] = acc
    hw_ref[...] = (acc * nw_ref[...]).astype(hw_ref.dtype)
    part = jnp.broadcast_to(jnp.sum(acc * acc, axis=-1, keepdims=True), ssq_ref.shape)

    @pl.when(j == 0)
    def _():
        ssq_ref[...] = part

    @pl.when(j > 0)
    def _():
        ssq_ref[...] += part


def matmul_res_norm(xs, w, res, norm_w, *, tm, tn, name):
    M = xs[0].shape[0]
    N = w.shape[1]
    ktot = sum(x.shape[1] for x in xs)
    tile = pl.BlockSpec((tm, tn), lambda i, j: (i, j))
    return pl.pallas_call(
        functools.partial(_mm_res_kernel, n_x=len(xs)), grid=(M // tm, N // tn),
        in_specs=[pl.BlockSpec((tm, x.shape[1]), lambda i, j: (i, 0)) for x in xs]
        + [pl.BlockSpec((ktot, tn), lambda i, j: (0, j)), tile, pl.BlockSpec((1, tn), lambda i, j: (0, j))],
        out_specs=[tile, tile, pl.BlockSpec((tm, LANES), lambda i, j: (i, 0))],
        out_shape=[jax.ShapeDtypeStruct((M, N), F32), jax.ShapeDtypeStruct((M, N), BF16),
                   jax.ShapeDtypeStruct((M, LANES), F32)],
        compiler_params=_cparams(("parallel", "arbitrary")), name=name)(*xs, w, res, norm_w.reshape(1, N))


def _mm_kacc_kernel(x_ref, w_ref, o_ref, *, row_chunk):
    k = pl.program_id(2)
    wb = w_ref[...].astype(BF16)
    tm = x_ref.shape[0]

    @pl.when(k == 0)
    def _():
        for r in range(0, tm, row_chunk):
            rows = slice(r, r + row_chunk)
            o_ref[rows, :] = jnp.dot(x_ref[rows, :], wb, preferred_element_type=F32)

    @pl.when(k > 0)
    def _():
        for r in range(0, tm, row_chunk):
            rows = slice(r, r + row_chunk)
            o_ref[rows, :] += jnp.dot(x_ref[rows, :], wb, preferred_element_type=F32)


def matmul_kacc(x, w, *, tm, tn, tk, name, row_chunk=KACC_ROW_CHUNK):
    M, K = x.shape
    N = w.shape[1]
    tm, tn, tk = min(tm, M), min(tn, N), min(tk, K)
    assert M % tm == 0 and N % tn == 0 and K % tk == 0 and tm % row_chunk == 0
    return pl.pallas_call(
        functools.partial(_mm_kacc_kernel, row_chunk=row_chunk), grid=(M // tm, N // tn, K // tk),
        in_specs=[pl.BlockSpec((tm, tk), lambda i, j, k: (i, k)),
                  pl.BlockSpec((tk, tn), lambda i, j, k: (k, j))],
        out_specs=pl.BlockSpec((tm, tn), lambda i, j, k: (i, j)),
        out_shape=jax.ShapeDtypeStruct((M, N), F32),
        compiler_params=_cparams(("parallel", "parallel", "arbitrary")), name=name)(x, w)


def _gates_kernel(y_ref, pos_ref, invf_ref, alog_ref, dtb_ref, aux_ref, cos_ref, sin_ref):
    y = y_ref[...]
    ang = pos_ref[...].astype(F32) * invf_ref[...]
    c = jnp.cos(ang)
    s = jnp.sin(ang)
    cos_ref[...] = c
    sin_ref[...] = s
    lane = lax.broadcasted_iota(jnp.int32, y.shape, 1)
    half = MLA_ROPE // 2
    rot = jnp.where(lane < half, -pltpu.roll(y, LANES - half, 1), pltpu.roll(y, half, 1))
    roped = y * c + rot * s
    z = y + dtb_ref[...]
    softplus = jnp.maximum(z, 0.0) + jnp.log1p(jnp.exp(-jnp.abs(z)))
    g = -jnp.exp(alog_ref[...]) * softplus
    beta = jax.nn.sigmoid(y)
    aux_ref[...] = jnp.where(lane < MLA_ROPE, roped, jnp.where(lane < MLA_ROPE + 16, g, beta))


def gates(y, pos_col, invf_row, alog_row, dtb_row, tm):
    M = y.shape[0]
    tm = min(tm, M)
    row = pl.BlockSpec((tm, LANES), lambda i: (i, 0))
    one = pl.BlockSpec((1, LANES), lambda i: (0, 0))
    shp = jax.ShapeDtypeStruct((M, LANES), F32)
    return pl.pallas_call(
        _gates_kernel, grid=(M // tm,),
        in_specs=[row, pl.BlockSpec((tm, 1), lambda i: (i, 0)), one, one, one],
        out_specs=[row, row, row], out_shape=[shp, shp, shp],
        compiler_params=_cparams(("parallel",)), name="gates")(y, pos_col, invf_row, alog_row, dtb_row)


def _mla_q_kernel(cq_ref, nw_ref, wq_ref, cos_ref, sin_ref, o_ref, *, heads, scale):
    x = cq_ref[...].astype(F32)
    r = lax.rsqrt(jnp.mean(x * x, axis=-1, keepdims=True) + NORM_EPS)
    xn = (x * r * nw_ref[...]).astype(BF16)
    q = jnp.dot(xn, wq_ref[...], preferred_element_type=F32) * scale
    half = MLA_ROPE // 2
    n_nope = heads * MLA_NOPE
    n_half = heads * half
    reps = n_half // LANES
    c = jnp.concatenate([cos_ref[...]] * reps, axis=1)
    s = jnp.concatenate([sin_ref[...]] * reps, axis=1)
    p1 = q[:, n_nope:n_nope + n_half]
    p2 = q[:, n_nope + n_half:]
    r1t = (p1 * c - p2 * s).T.astype(BF16)
    r2t = (p2 * c + p1 * s).T.astype(BF16)
    for h in range(heads):
        o_ref[h, 0:MLA_NOPE, :] = q[:, h * MLA_NOPE:(h + 1) * MLA_NOPE].T.astype(BF16)
        o_ref[h, MLA_NOPE:MLA_NOPE + half, :] = r1t[h * half:(h + 1) * half, :]
        o_ref[h, MLA_NOPE + half:MLA_NOPE + 2 * half, :] = r2t[h * half:(h + 1) * half, :]


def mla_q(proj, nw, wq, cos, sin, *, heads, tm, scale):
    S = proj.shape[0]
    R = nw.shape[0]
    dq = MLA_NOPE + MLA_ROPE
    kern = functools.partial(_mla_q_kernel, heads=heads, scale=scale)
    return pl.pallas_call(
        kern, grid=(S // tm,),
        in_specs=[pl.BlockSpec((tm, R), lambda i: (i, 0)),
                  pl.BlockSpec((1, R), lambda i: (0, 0)),
                  pl.BlockSpec(wq.shape, lambda i: (0, 0)),
                  pl.BlockSpec((tm, LANES), lambda i: (i, 0)),
                  pl.BlockSpec((tm, LANES), lambda i: (i, 0))],
        out_specs=pl.BlockSpec((heads, dq, tm), lambda i: (0, 0, i)),
        out_shape=jax.ShapeDtypeStruct((heads, dq, S), BF16),
        compiler_params=_cparams(("parallel",)), name="mla_q")(proj, nw.reshape(1, R), wq, cos, sin)


def _mla_kv_kernel(ckv_ref, nw_ref, wkv_ref, aux_ref, k_ref, vt_ref, *, heads):
    x = ckv_ref[...].astype(F32)
    r = lax.rsqrt(jnp.mean(x * x, axis=-1, keepdims=True) + NORM_EPS)
    xn = (x * r * nw_ref[...]).astype(BF16)
    kv = jnp.dot(xn, wkv_ref[...], preferred_element_type=F32)
    kpe = aux_ref[...][:, 0:MLA_ROPE].astype(BF16)
    n_nope = heads * MLA_NOPE
    for h in range(heads):
        k_ref[h, :, 0:MLA_NOPE] = kv[:, h * MLA_NOPE:(h + 1) * MLA_NOPE].astype(BF16)
        k_ref[h, :, MLA_NOPE:MLA_NOPE + MLA_ROPE] = kpe
        vt_ref[h, 0] = kv[:, n_nope + h * HEAD_DIM:n_nope + (h + 1) * HEAD_DIM].T.astype(BF16)


def mla_kv(proj, col_block, nw, wkv, aux, *, heads, tm):
    S = proj.shape[0]
    R = nw.shape[0]
    dq = MLA_NOPE + MLA_ROPE
    kern = functools.partial(_mla_kv_kernel, heads=heads)
    return pl.pallas_call(
        kern, grid=(S // tm,),
        in_specs=[pl.BlockSpec((tm, R), lambda i: (i, col_block)),
                  pl.BlockSpec((1, R), lambda i: (0, 0)),
                  pl.BlockSpec(wkv.shape, lambda i: (0, 0)),
                  pl.BlockSpec((tm, LANES), lambda i: (i, 0))],
        out_specs=[pl.BlockSpec((heads, tm, dq), lambda i: (0, i, 0)),
                   pl.BlockSpec((heads, 1, HEAD_DIM, tm), lambda i: (0, i, 0, 0))],
        out_shape=[jax.ShapeDtypeStruct((heads, S, dq), BF16),
                   jax.ShapeDtypeStruct((heads, S // tm, HEAD_DIM, tm), BF16)],
        compiler_params=_cparams(("parallel",)), name="mla_kv")(proj, nw.reshape(1, R), wkv, aux)


def _flash_kernel(qt_ref, k_ref, vt_ref, o_ref, acc_ref, m_ref, l_ref, sa_ref, sb_ref, *, t, hp):
    i = pl.program_id(1)
    m_ref[...] = jnp.full(m_ref.shape, -jnp.inf, F32)
    l_ref[...] = jnp.zeros(l_ref.shape, F32)
    acc_ref[...] = jnp.zeros(acc_ref.shape, F32)
    heads = range(hp)

    def scores(j, s_ref):
        for h in heads:
            s_ref[h] = jnp.dot(k_ref[h, j], qt_ref[h], preferred_element_type=F32)

    def softmax_pv(j, s_ref, masked):
        s = [s_ref[h] for h in heads]
        if masked:
            kpos = lax.broadcasted_iota(jnp.int32, (t, t), 0)
            qpos = lax.broadcasted_iota(jnp.int32, (t, t), 1)
            s = [jnp.where(kpos <= qpos, sh, -jnp.inf) for sh in s]
        m_old = [m_ref[h] for h in heads]
        m_new = [jnp.maximum(m_old[h], jnp.max(s[h], axis=0, keepdims=True)) for h in heads]
        alpha = [jnp.exp2(m_old[h] - m_new[h]) for h in heads]
        p = [jnp.exp2(s[h] - m_new[h]) for h in heads]
        for h in heads:
            l_ref[h] = alpha[h] * l_ref[h] + jnp.sum(p[h], axis=0, keepdims=True)
            m_ref[h] = m_new[h]
        pv = [jnp.dot(vt_ref[h, j], p[h].astype(BF16), preferred_element_type=F32) for h in heads]
        for h in heads:
            acc_ref[h] = alpha[h] * acc_ref[h] + pv[h]

    scores(0, sa_ref)

    def body(jj, carry):
        j = 2 * jj
        scores(j + 1, sb_ref)
        softmax_pv(j, sa_ref, False)
        scores(j + 2, sa_ref)
        softmax_pv(j + 1, sb_ref, False)
        return carry

    lax.fori_loop(0, i // 2, body, 0)

    @pl.when(i % 2 == 0)
    def _():
        softmax_pv(i, sa_ref, True)

    @pl.when(i % 2 == 1)
    def _():
        scores(i, sb_ref)
        softmax_pv(i - 1, sa_ref, False)
        softmax_pv(i, sb_ref, True)

    for h in range(hp):
        o_ref[:, h * HEAD_DIM:(h + 1) * HEAD_DIM] = (acc_ref[h] / l_ref[h]).T.astype(o_ref.dtype)


def flash_attention(qt, k4, vt4, *, t, hp):
    H, dq, S = qt.shape
    nk = S // t
    kern = functools.partial(_flash_kernel, t=t, hp=hp)
    whole_head = pl.Buffered(1)
    return pl.pallas_call(
        kern, grid=(H // hp, nk),
        in_specs=[pl.BlockSpec((hp, dq, t), lambda g, i: (g, 0, i)),
                  pl.BlockSpec((hp, nk, t, dq), lambda g, i: (g, 0, 0, 0), pipeline_mode=whole_head),
                  pl.BlockSpec((hp, nk, HEAD_DIM, t), lambda g, i: (g, 0, 0, 0), pipeline_mode=whole_head)],
        out_specs=pl.BlockSpec((t, hp * HEAD_DIM), lambda g, i: (i, g)),
        out_shape=jax.ShapeDtypeStruct((S, H * HEAD_DIM), BF16),
        scratch_shapes=[pltpu.VMEM((hp, HEAD_DIM, t), F32), pltpu.VMEM((hp, 1, t), F32),
                        pltpu.VMEM((hp, 1, t), F32), pltpu.VMEM((hp, t, t), F32), pltpu.VMEM((hp, t, t), F32)],
        compiler_params=_cparams(("parallel", "arbitrary")), name="mla_flash")(qt, k4, vt4)


def _cumsum_kernel(g_ref, o_ref, *, chunk):
    r = lax.broadcasted_iota(jnp.int32, (chunk, chunk), 0)
    c = lax.broadcasted_iota(jnp.int32, (chunk, chunk), 1)
    trilb = jnp.where(r >= c, 1.0, 0.0).astype(BF16)
    for r0 in range(0, g_ref.shape[0], chunk):
        g = g_ref[r0:r0 + chunk, :]
        g_hi = g.astype(BF16).astype(F32)
        g_mid = (g - g_hi).astype(BF16).astype(F32)
        g_lo = g - g_hi - g_mid
        o_ref[r0:r0 + chunk, :] = (jnp.dot(trilb, g_hi.astype(BF16), preferred_element_type=F32)
                                   + jnp.dot(trilb, g_mid.astype(BF16), preferred_element_type=F32)
                                   + jnp.dot(trilb, g_lo.astype(BF16), preferred_element_type=F32))


def chunk_cumsum(g, chunk, tm):
    M = g.shape[0]
    tm = min(tm, M)
    row = pl.BlockSpec((tm, LANES), lambda i: (i, 0))
    return pl.pallas_call(
        functools.partial(_cumsum_kernel, chunk=chunk), grid=(M // tm,), in_specs=[row], out_specs=row,
        out_shape=jax.ShapeDtypeStruct((M, LANES), F32),
        compiler_params=_cparams(("parallel",)), name="gdn_cumsum")(g)


def _bdot(a, b):
    return jnp.dot(a.astype(BF16), b.astype(BF16), preferred_element_type=F32)


def _conv_silu(x_ref, halo_ref, w_ref, xs_ref, first):
    rows = x_ref.shape[0]
    x = x_ref[...].astype(F32)
    xs_ref[0:8, :] = jnp.where(first, 0.0, halo_ref[...].astype(F32)[8:16, :])
    xs_ref[8:, :] = x
    w = w_ref[...]
    acc = x * w[GDN_CONV - 1:GDN_CONV, :]
    for s in range(1, GDN_CONV):
        acc = acc + xs_ref[8 - s:8 - s + rows, :] * w[GDN_CONV - 1 - s:GDN_CONV - s, :]
    return acc * jax.nn.sigmoid(acc)


def _l2n(y, scale):
    return y * (lax.rsqrt(jnp.sum(y * y, axis=-1, keepdims=True) + L2_EPS) * scale)


def _gdn_kernel(xq_ref, hq_ref, wq_ref, xk_ref, hk_ref, wk_ref, xv_ref, hv_ref, wv_ref, gate_ref,
                gcc_ref, bc_ref, gcr_ref, nw_ref, o_ref, s_ref, sq_ref, sk_ref, sv_ref, *, hb, rows, chunk):
    first = pl.program_id(1) == 0

    @pl.when(first)
    def _():
        s_ref[...] = jnp.zeros(s_ref.shape, F32)

    C = chunk
    ri = lax.broadcasted_iota(jnp.int32, (C, C), 0)
    ci = lax.broadcasted_iota(jnp.int32, (C, C), 1)
    lower = ri >= ci
    strict = ri > ci
    xor = ri ^ ci
    nw = nw_ref[...]
    n_levels = C.bit_length() - 1
    n_chunks = rows // C
    bodies = [(c, h) for c in range(n_chunks) for h in range(hb)]

    def rows_of(c):
        return slice(c * C, (c + 1) * C)

    def cols_of(h):
        return slice(h * HEAD_DIM, (h + 1) * HEAD_DIM)

    yq = _conv_silu(xq_ref, hq_ref, wq_ref, sq_ref, first)
    yk = _conv_silu(xk_ref, hk_ref, wk_ref, sk_ref, first)
    yv = _conv_silu(xv_ref, hv_ref, wv_ref, sv_ref, first)

    amats, rhs, qks, qds, tails, eglasts = {}, {}, {}, {}, {}, {}
    for b in bodies:
        c, h = b
        q = _l2n(yq[rows_of(c), cols_of(h)], HEAD_DIM ** -0.5)
        k = _l2n(yk[rows_of(c), cols_of(h)], 1.0)
        v = yv[rows_of(c), cols_of(h)]
        kt = k.T
        gcc = gcc_ref[rows_of(c), h:h + 1]
        bc = bc_ref[rows_of(c), h:h + 1]
        gcr = gcr_ref[h:h + 1, rows_of(c)]
        glast = gcr[:, C - 1:C]
        decay = jnp.exp(jnp.where(lower, gcc - gcr, -jnp.inf))
        egc = jnp.exp(gcc)
        kb = k * bc
        amats[b] = jnp.where(strict, _bdot(kb, kt) * decay, 0.0)
        rhs[b] = jnp.concatenate([v * bc, kb * egc], axis=1)
        qks[b] = _bdot(q, kt) * decay
        qds[b] = q * egc
        tails[b] = kt * jnp.exp(glast - gcr)
        eglasts[b] = jnp.exp(glast)

    tinv = {b: jnp.where(ri == ci, 1.0, 0.0) - jnp.where(strict & (xor == 1), amats[b], 0.0) for b in bodies}
    for level in range(1, n_levels):
        join = strict & ((xor >> level) == 1)
        mids = {b: _bdot(jnp.where(join, amats[b], 0.0), tinv[b]) for b in bodies}
        for b in bodies:
            tinv[b] = tinv[b] - _bdot(tinv[b], mids[b])
    xs = {b: _bdot(tinv[b], rhs[b]) for b in bodies}

    states = [s_ref[h] for h in range(hb)]
    for c in range(n_chunks):
        ws = [_bdot(jnp.concatenate([xs[(c, h)][:, HEAD_DIM:], qds[(c, h)]], axis=0), states[h]) for h in range(hb)]
        v_new = [xs[(c, h)][:, :HEAD_DIM] - ws[h][:C] for h in range(hb)]
        outs = [ws[h][C:] + _bdot(qks[(c, h)], v_new[h]) for h in range(hb)]
        states = [states[h] * eglasts[(c, h)] + _bdot(tails[(c, h)], v_new[h]) for h in range(hb)]
        for h in range(hb):
            o = outs[h]
            o = o * lax.rsqrt(jnp.mean(o * o, axis=-1, keepdims=True) + NORM_EPS) * nw
            gate = gate_ref[rows_of(c), cols_of(h)].astype(F32)
            o_ref[rows_of(c), cols_of(h)] = (o * (gate * jax.nn.sigmoid(gate))).astype(o_ref.dtype)
    for h in range(hb):
        s_ref[h] = states[h]


def gdn_mixer(proj, qkv_col0, gate_col0, conv_w, gcc, bc, gcr, nw, *, heads, hb, rows, chunk):
    S = proj.shape[0]
    tc = hb * HEAD_DIM
    gw = heads * HEAD_DIM
    halo_blocks = rows // 16
    assert qkv_col0 % tc == 0 and gate_col0 % tc == 0 and gw % tc == 0 and S % rows == 0 and rows % chunk == 0

    def section(col0, w_col0):
        cb, wb = col0 // tc, w_col0 // tc
        return [pl.BlockSpec((rows, tc), lambda g, r: (r, cb + g)),
                pl.BlockSpec((16, tc), lambda g, r: (jnp.maximum(r * halo_blocks - 1, 0), cb + g)),
                pl.BlockSpec((GDN_CONV, tc), lambda g, r: (0, wb + g))]

    in_specs = (section(qkv_col0, 0) + section(qkv_col0 + gw, gw) + section(qkv_col0 + 2 * gw, 2 * gw)
                + [pl.BlockSpec((rows, tc), lambda g, r: (r, gate_col0 // tc + g)),
                   pl.BlockSpec((None, rows, hb), lambda g, r: (g, r, 0)),
                   pl.BlockSpec((None, rows, hb), lambda g, r: (g, r, 0)),
                   pl.BlockSpec((None, hb, rows), lambda g, r: (g, 0, r)),
                   pl.BlockSpec((1, HEAD_DIM), lambda g, r: (0, 0))])
    stage = pltpu.VMEM((rows + 8, tc), F32)
    return pl.pallas_call(
        functools.partial(_gdn_kernel, hb=hb, rows=rows, chunk=chunk), grid=(heads // hb, S // rows),
        in_specs=in_specs,
        out_specs=pl.BlockSpec((rows, tc), lambda g, r: (r, g)),
        out_shape=jax.ShapeDtypeStruct((S, gw), BF16),
        scratch_shapes=[pltpu.VMEM((hb, HEAD_DIM, HEAD_DIM), F32), stage, stage, stage],
        compiler_params=_cparams(("parallel", "arbitrary")), name="gdn_mixer")(
            proj, proj, conv_w, proj, proj, conv_w, proj, proj, conv_w, proj, gcc, bc, gcr, nw.reshape(1, HEAD_DIM))


def _xattn_kernel(q_ref, kt_ref, v_ref, o_ref, *, heads):
    d = q_ref.shape[1] // heads
    cols = [slice(h * d, (h + 1) * d) for h in range(heads)]
    s = [jnp.dot(q_ref[:, c], kt_ref[c, :], preferred_element_type=F32) for c in cols]
    p = [jnp.exp2(sh - jnp.max(sh, axis=-1, keepdims=True)) for sh in s]
    l = [jnp.sum(ph, axis=-1, keepdims=True) for ph in p]
    o = [jnp.dot(ph.astype(BF16), v_ref[:, c], preferred_element_type=F32) for ph, c in zip(p, cols)]
    for c, oh, lh in zip(cols, o, l):
        o_ref[:, c] = (oh / lh).astype(o_ref.dtype)


def xattn_core(q, kt, v, *, heads, tq):
    S, W = q.shape
    M = v.shape[0]
    return pl.pallas_call(
        functools.partial(_xattn_kernel, heads=heads), grid=(S // tq,),
        in_specs=[pl.BlockSpec((tq, W), lambda i: (i, 0)),
                  pl.BlockSpec((W, M), lambda i: (0, 0)),
                  pl.BlockSpec((M, W), lambda i: (0, 0))],
        out_specs=pl.BlockSpec((tq, W), lambda i: (i, 0)),
        out_shape=jax.ShapeDtypeStruct((S, W), BF16),
        compiler_params=_cparams(("parallel",)), name="xattn_core")(q, kt, v)


def _block(x, mem, positions, attn_norm_w, w_in, mla_q_norm_w, mla_w_q_b, mla_kv_norm_w, mla_w_kv_b,
           gdn_conv_w, gdn_a_log, gdn_dt_bias, gdn_norm_w, w_out, xattn_norm_w, mem_norm_w,
           xattn_wq, xattn_wk, xattn_wv, xattn_wo, mlp_norm_w, mlp_w_up, mlp_w_down, final_norm_w):
    S, D = x.shape
    n_mix = w_out.shape[0]
    mla_heads = n_mix // 2 // HEAD_DIM
    gdn_heads = gdn_conv_w.shape[1] // (3 * HEAD_DIM)
    gw = gdn_heads * HEAD_DIM
    dq = MLA_NOPE + MLA_ROPE
    half = MLA_ROPE // 2

    o_kv = MLA_Q_RANK
    o_pe = o_kv + MLA_KV_RANK
    o_qkv = o_pe + MLA_ROPE
    o_a = o_qkv + 3 * gw
    o_b = o_a + gdn_heads
    o_gate = o_b + gdn_heads
    w_in_t = w_in.T
    c_qkv = o_pe
    c_gate = o_pe + 3 * gw
    n_proj = c_gate + (w_in.shape[1] - o_gate)
    tn_proj = 512

    skip1 = o_qkv - c_qkv
    skip2 = (o_gate - c_gate) - skip1
    unit = math.gcd(tn_proj, skip1, skip2)

    def proj_src_row(j):
        v = j * tn_proj
        past1 = (v >= c_qkv).astype(jnp.int32)
        past2 = (v >= c_gate).astype(jnp.int32)
        return (j * (tn_proj // unit) + past1 * (skip1 // unit) + past2 * (skip2 // unit)) * unit

    wq3 = mla_w_q_b.reshape(MLA_Q_RANK, mla_heads, dq)
    wq_perm = jnp.concatenate([wq3[:, :, :MLA_NOPE].reshape(MLA_Q_RANK, -1),
                               wq3[:, :, MLA_NOPE:MLA_NOPE + half].reshape(MLA_Q_RANK, -1),
                               wq3[:, :, MLA_NOPE + half:].reshape(MLA_Q_RANK, -1)], axis=1).astype(BF16)
    wkv3 = mla_w_kv_b.reshape(MLA_KV_RANK, mla_heads, MLA_NOPE + HEAD_DIM)
    wkv_perm = jnp.concatenate([wkv3[:, :, :MLA_NOPE].reshape(MLA_KV_RANK, -1),
                                wkv3[:, :, MLA_NOPE:].reshape(MLA_KV_RANK, -1)], axis=1).astype(BF16)

    inv_freq = ROPE_BASE ** (-jnp.arange(half, dtype=F32) / half)
    invf_row = jnp.tile(inv_freq, LANES // half).reshape(1, LANES)
    pad_l = jnp.zeros((MLA_ROPE,), F32)
    pad_r = jnp.zeros((LANES - MLA_ROPE - gdn_heads,), F32)
    alog_row = jnp.concatenate([pad_l, gdn_a_log.astype(F32), pad_r]).reshape(1, LANES)
    dtb_row = jnp.concatenate([pad_l, gdn_dt_bias.astype(F32), pad_r]).reshape(1, LANES)

    xn = rms_norm(x, attn_norm_w, BF16, 512)
    proj, w_up_bf16 = proj_nt(xn, w_in_t, proj_src_row, n_proj, mlp_w_up, tm=1024, tn=tn_proj, name="proj_main")
    small = proj_small(xn, w_in_t, o_pe, MLA_ROPE, o_a, 2 * gdn_heads, tm=1024)
    aux, cos, sin = gates(small, positions.reshape(S, 1), invf_row, alog_row, dtb_row, 1024)

    t_att = 512
    qt = mla_q(proj, mla_q_norm_w, wq_perm, cos, sin, heads=mla_heads, tm=512, scale=dq ** -0.5 * LOG2E)
    k_full, vt4 = mla_kv(proj, o_kv // MLA_KV_RANK, mla_kv_norm_w, wkv_perm, aux, heads=mla_heads, tm=t_att)
    k4 = k_full.reshape(mla_heads, S // t_att, t_att, dq)
    mla_out = flash_attention(qt, k4, vt4, t=t_att, hp=4)

    hb = 4
    gc_full = chunk_cumsum(aux, GDN_CHUNK, 1024)
    gc = gc_full[:, MLA_ROPE:MLA_ROPE + gdn_heads]
    beta = aux[:, MLA_ROPE + gdn_heads:MLA_ROPE + 2 * gdn_heads]
    gcc = gc.reshape(S, gdn_heads // hb, hb).transpose(1, 0, 2)
    bc = beta.reshape(S, gdn_heads // hb, hb).transpose(1, 0, 2)
    gcr = gc.T.reshape(gdn_heads // hb, hb, S)
    gdn_out = gdn_mixer(proj, c_qkv, c_gate, gdn_conv_w, gcc, bc, gcr, gdn_norm_w,
                        heads=gdn_heads, hb=hb, rows=512, chunk=GDN_CHUNK)

    h1, h1w, h1_ssq = matmul_res_norm([mla_out, gdn_out], w_out, x, xattn_norm_w, tm=1024, tn=512, name="w_out")

    memn = rms_norm(mem, mem_norm_w, BF16, 256)
    xd = D // XATTN_HEADS
    xq = matmul(h1w, xattn_wq, tm=1024, tn=512, out_dtype=BF16, out_scale=xd ** -0.5 * LOG2E,
                lhs_ssq=h1_ssq, name="xattn_q")
    xk = matmul(memn, xattn_wk, tm=256, tn=512, out_dtype=BF16, name="xattn_k")
    xv = matmul(memn, xattn_wv, tm=256, tn=512, out_dtype=BF16, name="xattn_v")
    xo = xattn_core(xq, xk.T, xv, heads=XATTN_HEADS, tq=512)
    h2, h2w, h2_ssq = matmul_res_norm([xo], xattn_wo, h1, mlp_norm_w, tm=1024, tn=512, name="xattn_o")

    up = matmul(h2w, w_up_bf16, tm=1024, tn=1024, out_dtype=BF16, relu2=True, lhs_ssq=h2_ssq, name="mlp_up")
    down = matmul_kacc(up, mlp_w_down, tm=2048, tn=1024, tk=2048, name="mlp_down")
    return rms_norm(h2, final_norm_w, F32, 256, add=down)


def kernel(x, mem, positions, attn_norm_w, w_in, mla_q_norm_w, mla_w_q_b, mla_kv_norm_w, mla_w_kv_b, gdn_conv_w, gdn_a_log, gdn_dt_bias, gdn_norm_w, w_out, xattn_norm_w, mem_norm_w, xattn_wq, xattn_wk, xattn_wv, xattn_wo, mlp_norm_w, mlp_w_up, mlp_w_down, final_norm_w):
    assert x.shape[0] == 1 and attn_norm_w.shape[0] == 1, "single sequence, single layer"
    out = _block(x[0], mem[0], positions[0], attn_norm_w[0], w_in[0], mla_q_norm_w[0], mla_w_q_b[0],
                 mla_kv_norm_w[0], mla_w_kv_b[0], gdn_conv_w[0], gdn_a_log[0], gdn_dt_bias[0], gdn_norm_w[0],
                 w_out[0], xattn_norm_w[0], mem_norm_w[0], xattn_wq[0], xattn_wk[0], xattn_wv[0], xattn_wo[0],
                 mlp_norm_w[0], mlp_w_up[0], mlp_w_down[0], final_norm_w)
    return out[None]
```

```python
import functools
import math

import jax
import jax.numpy as jnp
from jax import lax
from jax.experimental import pallas as pl
from jax.experimental.pallas import tpu as pltpu

F32 = jnp.float32
BF16 = jnp.bfloat16

HEAD_DIM = 128
MLA_NOPE = 128
MLA_ROPE = 64
MLA_Q_RANK = 1024
MLA_KV_RANK = 512
GDN_CONV = 4
XATTN_HEADS = 4
ROPE_BASE = 10000.0
NORM_EPS = 1e-6
L2_EPS = 1e-6
LOG2E = math.log2(math.e)

V7X_VMEM_BYTES = 64 * 1024 * 1024
VMEM_LIMIT = 58 * 1024 * 1024
assert VMEM_LIMIT < V7X_VMEM_BYTES
LANES = 128
KACC_ROW_CHUNK = 256

GDN_CHUNK = 128


def _cparams(sem):
    return pltpu.CompilerParams(dimension_semantics=sem, vmem_limit_bytes=VMEM_LIMIT)


def _rms_kernel(x_ref, w_ref, o_ref):
    x = x_ref[...].astype(F32)
    r = lax.rsqrt(jnp.mean(x * x, axis=-1, keepdims=True) + NORM_EPS)
    o_ref[...] = (x * r * w_ref[...]).astype(o_ref.dtype)


def _rms_add_kernel(x_ref, y_ref, w_ref, o_ref):
    x = x_ref[...] + y_ref[...]
    r = lax.rsqrt(jnp.mean(x * x, axis=-1, keepdims=True) + NORM_EPS)
    o_ref[...] = (x * r * w_ref[...]).astype(o_ref.dtype)


def rms_norm(x, w, out_dtype, tm, add=None):
    M, D = x.shape
    tm = min(tm, M)
    row = pl.BlockSpec((tm, D), lambda i: (i, 0))
    wspec = pl.BlockSpec((1, D), lambda i: (0, 0))
    if add is None:
        kern, specs, args = _rms_kernel, [row, wspec], (x, w.reshape(1, D))
    else:
        kern, specs, args = _rms_add_kernel, [row, row, wspec], (x, add, w.reshape(1, D))
    return pl.pallas_call(
        kern, grid=(M // tm,), in_specs=specs, out_specs=row,
        out_shape=jax.ShapeDtypeStruct((M, D), out_dtype),
        compiler_params=_cparams(("parallel",)), name="rms_norm")(*args)


def _mm_kernel(*refs, relu2, has_res, out_scale, ssq_dim):
    x_ref, w_ref = refs[0], refs[1]
    res_ref = refs[2] if has_res else None
    ssq_ref = refs[2 + has_res] if ssq_dim else None
    o_ref = refs[2 + has_res + bool(ssq_dim)]
    r = jnp.dot(x_ref[...], w_ref[...].astype(BF16), preferred_element_type=F32)
    if ssq_dim:
        r = r * lax.rsqrt(ssq_ref[:, 0:1] * (1.0 / ssq_dim) + NORM_EPS)
    if relu2:
        r = jnp.square(jnp.maximum(r, 0.0))
    if out_scale is not None:
        r = r * out_scale
    if has_res:
        r = r + res_ref[...]
    o_ref[...] = r.astype(o_ref.dtype)


def matmul(x, w, *, tm, tn, out_dtype, relu2=False, res=None, out_scale=None, lhs_ssq=None, name="matmul"):
    M, K = x.shape
    K2, N = w.shape
    assert K == K2
    tm, tn = min(tm, M), min(tn, N)
    assert M % tm == 0 and N % tn == 0
    in_specs = [pl.BlockSpec((tm, K), lambda i, j: (i, 0)),
                pl.BlockSpec((K, tn), lambda i, j: (0, j))]
    args = [x, w]
    if res is not None:
        in_specs.append(pl.BlockSpec((tm, tn), lambda i, j: (i, j)))
        args.append(res)
    if lhs_ssq is not None:
        in_specs.append(pl.BlockSpec((tm, LANES), lambda i, j: (i, 0)))
        args.append(lhs_ssq)
    kern = functools.partial(_mm_kernel, relu2=relu2, has_res=res is not None, out_scale=out_scale,
                             ssq_dim=K if lhs_ssq is not None else 0)
    return pl.pallas_call(
        kern, grid=(M // tm, N // tn), in_specs=in_specs,
        out_specs=pl.BlockSpec((tm, tn), lambda i, j: (i, j)),
        out_shape=jax.ShapeDtypeStruct((M, N), out_dtype),
        compiler_params=_cparams(("parallel", "parallel")), name=name)(*args)


_NT_DIMS = (((1,), (1,)), ((), ()))


def _proj_nt_kernel(x_ref, wt_ref, cast_src_ref, o_ref, cast_dst_ref):
    wb = wt_ref[...].astype(BF16)
    o_ref[...] = lax.dot_general(x_ref[...], wb, _NT_DIMS, preferred_element_type=F32).astype(o_ref.dtype)
    cast_dst_ref[...] = cast_src_ref[...].astype(cast_dst_ref.dtype)


def _cast_blocks(w, n_steps):
    rows = w.shape[0]
    rb = 16
    while rows // rb > n_steps or rows % rb:
        rb += 16
    return rb, rows // rb


def proj_nt(x, wt, src_row_of_block, n_out, cast_w, *, tm, tn, name):
    M, K = x.shape
    ni, nj = M // tm, n_out // tn
    rb, n_blocks = _cast_blocks(cast_w, ni * nj)
    cast_spec = pl.BlockSpec((rb, cast_w.shape[1]), lambda i, j: (jnp.minimum(i * nj + j, n_blocks - 1), 0))
    return pl.pallas_call(
        _proj_nt_kernel, grid=(ni, nj),
        in_specs=[pl.BlockSpec((tm, K), lambda i, j: (i, 0)),
                  pl.BlockSpec((pl.Element(tn), pl.Element(K)), lambda i, j: (src_row_of_block(j), 0)),
                  cast_spec],
        out_specs=[pl.BlockSpec((tm, tn), lambda i, j: (i, j)), cast_spec],
        out_shape=[jax.ShapeDtypeStruct((M, n_out), BF16), jax.ShapeDtypeStruct(cast_w.shape, BF16)],
        compiler_params=_cparams(("arbitrary", "arbitrary")), name=name)(x, wt, cast_w)


def _proj_small_kernel(x_ref, wa_ref, wb_ref, o_ref):
    wa = wa_ref[...].astype(BF16)
    wb = wb_ref[...].astype(BF16)
    pad = jnp.zeros((LANES - wa.shape[0] - wb.shape[0], wa.shape[1]), BF16)
    w = jnp.concatenate([wa, wb, pad], axis=0)
    o_ref[...] = lax.dot_general(x_ref[...], w, _NT_DIMS, preferred_element_type=F32)


def proj_small(x, wt, row_a, n_a, row_b, n_b, *, tm):
    M, K = x.shape
    return pl.pallas_call(
        _proj_small_kernel, grid=(M // tm,),
        in_specs=[pl.BlockSpec((tm, K), lambda i: (i, 0)),
                  pl.BlockSpec((pl.Element(n_a), pl.Element(K)), lambda i: (row_a, 0)),
                  pl.BlockSpec((pl.Element(n_b), pl.Element(K)), lambda i: (row_b, 0))],
        out_specs=pl.BlockSpec((tm, LANES), lambda i: (i, 0)),
        out_shape=jax.ShapeDtypeStruct((M, LANES), F32),
        compiler_params=_cparams(("parallel",)), name="proj_small")(x, wt, wt)


def _mm_res_kernel(*refs, n_x):
    x_refs = refs[:n_x]
    w_ref, res_ref, nw_ref, h_ref, hw_ref, ssq_ref = refs[n_x:]
    j = pl.program_id(1)
    wb = w_ref[...].astype(BF16)
    k0 = 0
    acc = res_ref[...]
    for x_ref in x_refs:
        kx = x_ref.shape[1]
        acc = acc + jnp.dot(x_ref[...], wb[k0:k0 + kx], preferred_element_type=F32)
        k0 += kx
    h_ref[...] = acc
    hw_ref[...] = (acc * nw_ref[...]).astype(hw_ref.dtype)
    part = jnp.broadcast_to(jnp.sum(acc * acc, axis=-1, keepdims=True), ssq_ref.shape)

    @pl.when(j == 0)
    def _():
        ssq_ref[...] = part

    @pl.when(j > 0)
    def _():
        ssq_ref[...] += part


def matmul_res_norm(xs, w, res, norm_w, *, tm, tn, name):
    M = xs[0].shape[0]
    N = w.shape[1]
    ktot = sum(x.shape[1] for x in xs)
    tile = pl.BlockSpec((tm, tn), lambda i, j: (i, j))
    return pl.pallas_call(
        functools.partial(_mm_res_kernel, n_x=len(xs)), grid=(M // tm, N // tn),
        in_specs=[pl.BlockSpec((tm, x.shape[1]), lambda i, j: (i, 0)) for x in xs]
        + [pl.BlockSpec((ktot, tn), lambda i, j: (0, j)), tile, pl.BlockSpec((1, tn), lambda i, j: (0, j))],
        out_specs=[tile, tile, pl.BlockSpec((tm, LANES), lambda i, j: (i, 0))],
        out_shape=[jax.ShapeDtypeStruct((M, N), F32), jax.ShapeDtypeStruct((M, N), BF16),
                   jax.ShapeDtypeStruct((M, LANES), F32)],
        compiler_params=_cparams(("parallel", "arbitrary")), name=name)(*xs, w, res, norm_w.reshape(1, N))


def _mm_kacc_kernel(x_ref, w_ref, o_ref, *, row_chunk):
    k = pl.program_id(2)
    wb = w_ref[...].astype(BF16)
    tm = x_ref.shape[0]

    @pl.when(k == 0)
    def _():
        for r in range(0, tm, row_chunk):
            rows = slice(r, r + row_chunk)
            o_ref[rows, :] = jnp.dot(x_ref[rows, :], wb, preferred_element_type=F32)

    @pl.when(k > 0)
    def _():
        for r in range(0, tm, row_chunk):
            rows = slice(r, r + row_chunk)
            o_ref[rows, :] += jnp.dot(x_ref[rows, :], wb, preferred_element_type=F32)


def matmul_kacc(x, w, *, tm, tn, tk, name, row_chunk=KACC_ROW_CHUNK):
    M, K = x.shape
    N = w.shape[1]
    tm, tn, tk = min(tm, M), min(tn, N), min(tk, K)
    assert M % tm == 0 and N % tn == 0 and K % tk == 0 and tm % row_chunk == 0
    return pl.pallas_call(
        functools.partial(_mm_kacc_kernel, row_chunk=row_chunk), grid=(M // tm, N // tn, K // tk),
        in_specs=[pl.BlockSpec((tm, tk), lambda i, j, k: (i, k)),
                  pl.BlockSpec((tk, tn), lambda i, j, k: (k, j))],
        out_specs=pl.BlockSpec((tm, tn), lambda i, j, k: (i, j)),
        out_shape=jax.ShapeDtypeStruct((M, N), F32),
        compiler_params=_cparams(("parallel", "parallel", "arbitrary")), name=name)(x, w)


def _gates_kernel(y_ref, pos_ref, invf_ref, alog_ref, dtb_ref, aux_ref, cos_ref, sin_ref):
    y = y_ref[...]
    ang = pos_ref[...].astype(F32) * invf_ref[...]
    c = jnp.cos(ang)
    s = jnp.sin(ang)
    cos_ref[...] = c
    sin_ref[...] = s
    lane = lax.broadcasted_iota(jnp.int32, y.shape, 1)
    half = MLA_ROPE // 2
    rot = jnp.where(lane < half, -pltpu.roll(y, LANES - half, 1), pltpu.roll(y, half, 1))
    roped = y * c + rot * s
    z = y + dtb_ref[...]
    softplus = jnp.maximum(z, 0.0) + jnp.log1p(jnp.exp(-jnp.abs(z)))
    g = -jnp.exp(alog_ref[...]) * softplus
    beta = jax.nn.sigmoid(y)
    aux_ref[...] = jnp.where(lane < MLA_ROPE, roped, jnp.where(lane < MLA_ROPE + 16, g, beta))


def gates(y, pos_col, invf_row, alog_row, dtb_row, tm):
    M = y.shape[0]
    tm = min(tm, M)
    row = pl.BlockSpec((tm, LANES), lambda i: (i, 0))
    one = pl.BlockSpec((1, LANES), lambda i: (0, 0))
    shp = jax.ShapeDtypeStruct((M, LANES), F32)
    return pl.pallas_call(
        _gates_kernel, grid=(M // tm,),
        in_specs=[row, pl.BlockSpec((tm, 1), lambda i: (i, 0)), one, one, one],
        out_specs=[row, row, row], out_shape=[shp, shp, shp],
        compiler_params=_cparams(("parallel",)), name="gates")(y, pos_col, invf_row, alog_row, dtb_row)


def _mla_q_kernel(cq_ref, nw_ref, wq_ref, cos_ref, sin_ref, o_ref, *, heads, scale):
    x = cq_ref[...].astype(F32)
    r = lax.rsqrt(jnp.mean(x * x, axis=-1, keepdims=True) + NORM_EPS)
    xn = (x * r * nw_ref[...]).astype(BF16)
    q = jnp.dot(xn, wq_ref[...], preferred_element_type=F32) * scale
    half = MLA_ROPE // 2
    n_nope = heads * MLA_NOPE
    n_half = heads * half
    reps = n_half // LANES
    c = jnp.concatenate([cos_ref[...]] * reps, axis=1)
    s = jnp.concatenate([sin_ref[...]] * reps, axis=1)
    p1 = q[:, n_nope:n_nope + n_half]
    p2 = q[:, n_nope + n_half:]
    r1t = (p1 * c - p2 * s).T.astype(BF16)
    r2t = (p2 * c + p1 * s).T.astype(BF16)
    for h in range(heads):
        o_ref[h, 0:MLA_NOPE, :] = q[:, h * MLA_NOPE:(h + 1) * MLA_NOPE].T.astype(BF16)
        o_ref[h, MLA_NOPE:MLA_NOPE + half, :] = r1t[h * half:(h + 1) * half, :]
        o_ref[h, MLA_NOPE + half:MLA_NOPE + 2 * half, :] = r2t[h * half:(h + 1) * half, :]


def mla_q(proj, nw, wq, cos, sin, *, heads, tm, scale):
    S = proj.shape[0]
    R = nw.shape[0]
    dq = MLA_NOPE + MLA_ROPE
    kern = functools.partial(_mla_q_kernel, heads=heads, scale=scale)
    return pl.pallas_call(
        kern, grid=(S // tm,),
        in_specs=[pl.BlockSpec((tm, R), lambda i: (i, 0)),
                  pl.BlockSpec((1, R), lambda i: (0, 0)),
                  pl.BlockSpec(wq.shape, lambda i: (0, 0)),
                  pl.BlockSpec((tm, LANES), lambda i: (i, 0)),
                  pl.BlockSpec((tm, LANES), lambda i: (i, 0))],
        out_specs=pl.BlockSpec((heads, dq, tm), lambda i: (0, 0, i)),
        out_shape=jax.ShapeDtypeStruct((heads, dq, S), BF16),
        compiler_params=_cparams(("parallel",)), name="mla_q")(proj, nw.reshape(1, R), wq, cos, sin)


def _mla_kv_kernel(ckv_ref, nw_ref, wkv_ref, aux_ref, k_ref, vt_ref, *, heads):
    x = ckv_ref[...].astype(F32)
    r = lax.rsqrt(jnp.mean(x * x, axis=-1, keepdims=True) + NORM_EPS)
    xn = (x * r * nw_ref[...]).astype(BF16)
    kv = jnp.dot(xn, wkv_ref[...], preferred_element_type=F32)
    kpe = aux_ref[...][:, 0:MLA_ROPE].astype(BF16)
    n_nope = heads * MLA_NOPE
    for h in range(heads):
        k_ref[h, :, 0:MLA_NOPE] = kv[:, h * MLA_NOPE:(h + 1) * MLA_NOPE].astype(BF16)
        k_ref[h, :, MLA_NOPE:MLA_NOPE + MLA_ROPE] = kpe
        vt_ref[h, 0] = kv[:, n_nope + h * HEAD_DIM:n_nope + (h + 1) * HEAD_DIM].T.astype(BF16)


def mla_kv(proj, col_block, nw, wkv, aux, *, heads, tm):
    S = proj.shape[0]
    R = nw.shape[0]
    dq = MLA_NOPE + MLA_ROPE
    kern = functools.partial(_mla_kv_kernel, heads=heads)
    return pl.pallas_call(
        kern, grid=(S // tm,),
        in_specs=[pl.BlockSpec((tm, R), lambda i: (i, col_block)),
                  pl.BlockSpec((1, R), lambda i: (0, 0)),
                  pl.BlockSpec(wkv.shape, lambda i: (0, 0)),
                  pl.BlockSpec((tm, LANES), lambda i: (i, 0))],
        out_specs=[pl.BlockSpec((heads, tm, dq), lambda i: (0, i, 0)),
                   pl.BlockSpec((heads, 1, HEAD_DIM, tm), lambda i: (0, i, 0, 0))],
        out_shape=[jax.ShapeDtypeStruct((heads, S, dq), BF16),
                   jax.ShapeDtypeStruct((heads, S // tm, HEAD_DIM, tm), BF16)],
        compiler_params=_cparams(("parallel",)), name="mla_kv")(proj, nw.reshape(1, R), wkv, aux)


def _flash_kernel(qt_ref, k_ref, vt_ref, o_ref, acc_ref, m_ref, l_ref, sa_ref, sb_ref, *, t, hp):
    i = pl.program_id(1)
    m_ref[...] = jnp.full(m_ref.shape, -jnp.inf, F32)
    l_ref[...] = jnp.zeros(l_ref.shape, F32)
    acc_ref[...] = jnp.zeros(acc_ref.shape, F32)
    heads = range(hp)

    def scores(j, s_ref):
        for h in heads:
            s_ref[h] = jnp.dot(k_ref[h, j], qt_ref[h], preferred_element_type=F32)

    def softmax_pv(j, s_ref, masked):
        s = [s_ref[h] for h in heads]
        if masked:
            kpos = lax.broadcasted_iota(jnp.int32, (t, t), 0)
            qpos = lax.broadcasted_iota(jnp.int32, (t, t), 1)
            s = [jnp.where(kpos <= qpos, sh, -jnp.inf) for sh in s]
        m_old = [m_ref[h] for h in heads]
        m_new = [jnp.maximum(m_old[h], jnp.max(s[h], axis=0, keepdims=True)) for h in heads]
        alpha = [jnp.exp2(m_old[h] - m_new[h]) for h in heads]
        p = [jnp.exp2(s[h] - m_new[h]) for h in heads]
        for h in heads:
            l_ref[h] = alpha[h] * l_ref[h] + jnp.sum(p[h], axis=0, keepdims=True)
            m_ref[h] = m_new[h]
        pv = [jnp.dot(vt_ref[h, j], p[h].astype(BF16), preferred_element_type=F32) for h in heads]
        for h in heads:
            acc_ref[h] = alpha[h] * acc_ref[h] + pv[h]

    scores(0, sa_ref)

    def body(jj, carry):
        j = 2 * jj
        scores(j + 1, sb_ref)
        softmax_pv(j, sa_ref, False)
        scores(j + 2, sa_ref)
        softmax_pv(j + 1, sb_ref, False)
        return carry

    lax.fori_loop(0, i // 2, body, 0)

    @pl.when(i % 2 == 0)
    def _():
        softmax_pv(i, sa_ref, True)

    @pl.when(i % 2 == 1)
    def _():
        scores(i, sb_ref)
        softmax_pv(i - 1, sa_ref, False)
        softmax_pv(i, sb_ref, True)

    for h in range(hp):
        o_ref[:, h * HEAD_DIM:(h + 1) * HEAD_DIM] = (acc_ref[h] / l_ref[h]).T.astype(o_ref.dtype)


def flash_attention(qt, k4, vt4, *, t, hp):
    H, dq, S = qt.shape
    nk = S // t
    kern = functools.partial(_flash_kernel, t=t, hp=hp)
    whole_head = pl.Buffered(1)
    return pl.pallas_call(
        kern, grid=(H // hp, nk),
        in_specs=[pl.BlockSpec((hp, dq, t), lambda g, i: (g, 0, i)),
                  pl.BlockSpec((hp, nk, t, dq), lambda g, i: (g, 0, 0, 0), pipeline_mode=whole_head),
                  pl.BlockSpec((hp, nk, HEAD_DIM, t), lambda g, i: (g, 0, 0, 0), pipeline_mode=whole_head)],
        out_specs=pl.BlockSpec((t, hp * HEAD_DIM), lambda g, i: (i, g)),
        out_shape=jax.ShapeDtypeStruct((S, H * HEAD_DIM), BF16),
        scratch_shapes=[pltpu.VMEM((hp, HEAD_DIM, t), F32), pltpu.VMEM((hp, 1, t), F32),
                        pltpu.VMEM((hp, 1, t), F32), pltpu.VMEM((hp, t, t), F32), pltpu.VMEM((hp, t, t), F32)],
        compiler_params=_cparams(("parallel", "arbitrary")), name="mla_flash")(qt, k4, vt4)


def _cumsum_kernel(g_ref, o_ref, *, chunk):
    r = lax.broadcasted_iota(jnp.int32, (chunk, chunk), 0)
    c = lax.broadcasted_iota(jnp.int32, (chunk, chunk), 1)
    trilb = jnp.where(r >= c, 1.0, 0.0).astype(BF16)
    for r0 in range(0, g_ref.shape[0], chunk):
        g = g_ref[r0:r0 + chunk, :]
        g_hi = g.astype(BF16).astype(F32)
        g_mid = (g - g_hi).astype(BF16).astype(F32)
        g_lo = g - g_hi - g_mid
        o_ref[r0:r0 + chunk, :] = (jnp.dot(trilb, g_hi.astype(BF16), preferred_element_type=F32)
                                   + jnp.dot(trilb, g_mid.astype(BF16), preferred_element_type=F32)
                                   + jnp.dot(trilb, g_lo.astype(BF16), preferred_element_type=F32))


def chunk_cumsum(g, chunk, tm):
    M = g.shape[0]
    tm = min(tm, M)
    row = pl.BlockSpec((tm, LANES), lambda i: (i, 0))
    return pl.pallas_call(
        functools.partial(_cumsum_kernel, chunk=chunk), grid=(M // tm,), in_specs=[row], out_specs=row,
        out_shape=jax.ShapeDtypeStruct((M, LANES), F32),
        compiler_params=_cparams(("parallel",)), name="gdn_cumsum")(g)


def _bdot(a, b):
    return jnp.dot(a.astype(BF16), b.astype(BF16), preferred_element_type=F32)


def _conv_silu(x_ref, halo_ref, w_ref, xs_ref, first):
    rows = x_ref.shape[0]
    x = x_ref[...].astype(F32)
    xs_ref[0:8, :] = jnp.where(first, 0.0, halo_ref[...].astype(F32)[8:16, :])
    xs_ref[8:, :] = x
    w = w_ref[...]
    acc = x * w[GDN_CONV - 1:GDN_CONV, :]
    for s in range(1, GDN_CONV):
        acc = acc + xs_ref[8 - s:8 - s + rows, :] * w[GDN_CONV - 1 - s:GDN_CONV - s, :]
    return acc * jax.nn.sigmoid(acc)


def _l2n(y, scale):
    return y * (lax.rsqrt(jnp.sum(y * y, axis=-1, keepdims=True) + L2_EPS) * scale)


def _gdn_kernel(xq_ref, hq_ref, wq_ref, xk_ref, hk_ref, wk_ref, xv_ref, hv_ref, wv_ref, gate_ref,
                gcc_ref, bc_ref, gcr_ref, nw_ref, o_ref, s_ref, sq_ref, sk_ref, sv_ref, *, hb, rows, chunk):
    first = pl.program_id(1) == 0

    @pl.when(first)
    def _():
        s_ref[...] = jnp.zeros(s_ref.shape, F32)

    C = chunk
    ri = lax.broadcasted_iota(jnp.int32, (C, C), 0)
    ci = lax.broadcasted_iota(jnp.int32, (C, C), 1)
    lower = ri >= ci
    strict = ri > ci
    xor = ri ^ ci
    nw = nw_ref[...]
    n_levels = C.bit_length() - 1
    n_chunks = rows // C
    bodies = [(c, h) for c in range(n_chunks) for h in range(hb)]

    def rows_of(c):
        return slice(c * C, (c + 1) * C)

    def cols_of(h):
        return slice(h * HEAD_DIM, (h + 1) * HEAD_DIM)

    yq = _conv_silu(xq_ref, hq_ref, wq_ref, sq_ref, first)
    yk = _conv_silu(xk_ref, hk_ref, wk_ref, sk_ref, first)
    yv = _conv_silu(xv_ref, hv_ref, wv_ref, sv_ref, first)

    amats, rhs, qks, qds, tails, eglasts = {}, {}, {}, {}, {}, {}
    for b in bodies:
        c, h = b
        q = _l2n(yq[rows_of(c), cols_of(h)], HEAD_DIM ** -0.5)
        k = _l2n(yk[rows_of(c), cols_of(h)], 1.0)
        v = yv[rows_of(c), cols_of(h)]
        kt = k.T
        gcc = gcc_ref[rows_of(c), h:h + 1]
        bc = bc_ref[rows_of(c), h:h + 1]
        gcr = gcr_ref[h:h + 1, rows_of(c)]
        glast = gcr[:, C - 1:C]
        decay = jnp.exp(jnp.where(lower, gcc - gcr, -jnp.inf))
        egc = jnp.exp(gcc)
        kb = k * bc
        amats[b] = jnp.where(strict, _bdot(kb, kt) * decay, 0.0)
        rhs[b] = jnp.concatenate([v * bc, kb * egc], axis=1)
        qks[b] = _bdot(q, kt) * decay
        qds[b] = q * egc
        tails[b] = kt * jnp.exp(glast - gcr)
        eglasts[b] = jnp.exp(glast)

    tinv = {b: jnp.where(ri == ci, 1.0, 0.0) - jnp.where(strict & (xor == 1), amats[b], 0.0) for b in bodies}
    for level in range(1, n_levels):
        join = strict & ((xor >> level) == 1)
        mids = {b: _bdot(jnp.where(join, amats[b], 0.0), tinv[b]) for b in bodies}
        for b in bodies:
            tinv[b] = tinv[b] - _bdot(tinv[b], mids[b])
    xs = {b: _bdot(tinv[b], rhs[b]) for b in bodies}

    states = [s_ref[h] for h in range(hb)]
    for c in range(n_chunks):
        ws = [_bdot(jnp.concatenate([xs[(c, h)][:, HEAD_DIM:], qds[(c, h)]], axis=0), states[h]) for h in range(hb)]
        v_new = [xs[(c, h)][:, :HEAD_DIM] - ws[h][:C] for h in range(hb)]
        outs = [ws[h][C:] + _bdot(qks[(c, h)], v_new[h]) for h in range(hb)]
        states = [states[h] * eglasts[(c, h)] + _bdot(tails[(c, h)], v_new[h]) for h in range(hb)]
        for h in range(hb):
            o = outs[h]
            o = o * lax.rsqrt(jnp.mean(o * o, axis=-1, keepdims=True) + NORM_EPS) * nw
            gate = gate_ref[rows_of(c), cols_of(h)].astype(F32)
            o_ref[rows_of(c), cols_of(h)] = (o * (gate * jax.nn.sigmoid(gate))).astype(o_ref.dtype)
    for h in range(hb):
        s_ref[h] = states[h]


def gdn_mixer(proj, qkv_col0, gate_col0, conv_w, gcc, bc, gcr, nw, *, heads, hb, rows, chunk):
    S = proj.shape[0]
    tc = hb * HEAD_DIM
    gw = heads * HEAD_DIM
    halo_blocks = rows // 16
    assert qkv_col0 % tc == 0 and gate_col0 % tc == 0 and gw % tc == 0 and S % rows == 0 and rows % chunk == 0

    def section(col0, w_col0):
        cb, wb = col0 // tc, w_col0 // tc
        return [pl.BlockSpec((rows, tc), lambda g, r: (r, cb + g)),
                pl.BlockSpec((16, tc), lambda g, r: (jnp.maximum(r * halo_blocks - 1, 0), cb + g)),
                pl.BlockSpec((GDN_CONV, tc), lambda g, r: (0, wb + g))]

    in_specs = (section(qkv_col0, 0) + section(qkv_col0 + gw, gw) + section(qkv_col0 + 2 * gw, 2 * gw)
                + [pl.BlockSpec((rows, tc), lambda g, r: (r, gate_col0 // tc + g)),
                   pl.BlockSpec((None, rows, hb), lambda g, r: (g, r, 0)),
                   pl.BlockSpec((None, rows, hb), lambda g, r: (g, r, 0)),
                   pl.BlockSpec((None, hb, rows), lambda g, r: (g, 0, r)),
                   pl.BlockSpec((1, HEAD_DIM), lambda g, r: (0, 0))])
    stage = pltpu.VMEM((rows + 8, tc), F32)
    return pl.pallas_call(
        functools.partial(_gdn_kernel, hb=hb, rows=rows, chunk=chunk), grid=(heads // hb, S // rows),
        in_specs=in_specs,
        out_specs=pl.BlockSpec((rows, tc), lambda g, r: (r, g)),
        out_shape=jax.ShapeDtypeStruct((S, gw), BF16),
        scratch_shapes=[pltpu.VMEM((hb, HEAD_DIM, HEAD_DIM), F32), stage, stage, stage],
        compiler_params=_cparams(("parallel", "arbitrary")), name="gdn_mixer")(
            proj, proj, conv_w, proj, proj, conv_w, proj, proj, conv_w, proj, gcc, bc, gcr, nw.reshape(1, HEAD_DIM))


def _xattn_kernel(q_ref, kt_ref, v_ref, o_ref, *, heads):
    d = q_ref.shape[1] // heads
    cols = [slice(h * d, (h + 1) * d) for h in range(heads)]
    s = [jnp.dot(q_ref[:, c], kt_ref[c, :], preferred_element_type=F32) for c in cols]
    p = [jnp.exp2(sh - jnp.max(sh, axis=-1, keepdims=True)) for sh in s]
    l = [jnp.sum(ph, axis=-1, keepdims=True) for ph in p]
    o = [jnp.dot(ph.astype(BF16), v_ref[:, c], preferred_element_type=F32) for ph, c in zip(p, cols)]
    for c, oh, lh in zip(cols, o, l):
        o_ref[:, c] = (oh / lh).astype(o_ref.dtype)


def xattn_core(q, kt, v, *, heads, tq):
    S, W = q.shape
    M = v.shape[0]
    return pl.pallas_call(
        functools.partial(_xattn_kernel, heads=heads), grid=(S // tq,),
        in_specs=[pl.BlockSpec((tq, W), lambda i: (i, 0)),
                  pl.BlockSpec((W, M), lambda i: (0, 0)),
                  pl.BlockSpec((M, W), lambda i: (0, 0))],
        out_specs=pl.BlockSpec((tq, W), lambda i: (i, 0)),
        out_shape=jax.ShapeDtypeStruct((S, W), BF16),
        compiler_params=_cparams(("parallel",)), name="xattn_core")(q, kt, v)


def _block(x, mem, positions, attn_norm_w, w_in, mla_q_norm_w, mla_w_q_b, mla_kv_norm_w, mla_w_kv_b,
           gdn_conv_w, gdn_a_log, gdn_dt_bias, gdn_norm_w, w_out, xattn_norm_w, mem_norm_w,
           xattn_wq, xattn_wk, xattn_wv, xattn_wo, mlp_norm_w, mlp_w_up, mlp_w_down, final_norm_w):
    S, D = x.shape
    n_mix = w_out.shape[0]
    mla_heads = n_mix // 2 // HEAD_DIM
    gdn_heads = gdn_conv_w.shape[1] // (3 * HEAD_DIM)
    gw = gdn_heads * HEAD_DIM
    dq = MLA_NOPE + MLA_ROPE
    half = MLA_ROPE // 2

    o_kv = MLA_Q_RANK
    o_pe = o_kv + MLA_KV_RANK
    o_qkv = o_pe + MLA_ROPE
    o_a = o_qkv + 3 * gw
    o_b = o_a + gdn_heads
    o_gate = o_b + gdn_heads
    w_in_t = w_in.T
    c_qkv = o_pe
    c_gate = o_pe + 3 * gw
    n_proj = c_gate + (w_in.shape[1] - o_gate)
    tn_proj = 512

    skip1 = o_qkv - c_qkv
    skip2 = (o_gate - c_gate) - skip1
    unit = math.gcd(tn_proj, skip1, skip2)

    def proj_src_row(j):
        v = j * tn_proj
        past1 = (v >= c_qkv).astype(jnp.int32)
        past2 = (v >= c_gate).astype(jnp.int32)
        return (j * (tn_proj // unit) + past1 * (skip1 // unit) + past2 * (skip2 // unit)) * unit

    wq3 = mla_w_q_b.reshape(MLA_Q_RANK, mla_heads, dq)
    wq_perm = jnp.concatenate([wq3[:, :, :MLA_NOPE].reshape(MLA_Q_RANK, -1),
                               wq3[:, :, MLA_NOPE:MLA_NOPE + half].reshape(MLA_Q_RANK, -1),
                               wq3[:, :, MLA_NOPE + half:].reshape(MLA_Q_RANK, -1)], axis=1).astype(BF16)
    wkv3 = mla_w_kv_b.reshape(MLA_KV_RANK, mla_heads, MLA_NOPE + HEAD_DIM)
    wkv_perm = jnp.concatenate([wkv3[:, :, :MLA_NOPE].reshape(MLA_KV_RANK, -1),
                                wkv3[:, :, MLA_NOPE:].reshape(MLA_KV_RANK, -1)], axis=1).astype(BF16)

    inv_freq = ROPE_BASE ** (-jnp.arange(half, dtype=F32) / half)
    invf_row = jnp.tile(inv_freq, LANES // half).reshape(1, LANES)
    pad_l = jnp.zeros((MLA_ROPE,), F32)
    pad_r = jnp.zeros((LANES - MLA_ROPE - gdn_heads,), F32)
    alog_row = jnp.concatenate([pad_l, gdn_a_log.astype(F32), pad_r]).reshape(1, LANES)
    dtb_row = jnp.concatenate([pad_l, gdn_dt_bias.astype(F32), pad_r]).reshape(1, LANES)

    xn = rms_norm(x, attn_norm_w, BF16, 512)
    proj, w_up_bf16 = proj_nt(xn, w_in_t, proj_src_row, n_proj, mlp_w_up, tm=1024, tn=tn_proj, name="proj_main")
    small = proj_small(xn, w_in_t, o_pe, MLA_ROPE, o_a, 2 * gdn_heads, tm=1024)
    aux, cos, sin = gates(small, positions.reshape(S, 1), invf_row, alog_row, dtb_row, 1024)

    t_att = 512
    qt = mla_q(proj, mla_q_norm_w, wq_perm, cos, sin, heads=mla_heads, tm=512, scale=dq ** -0.5 * LOG2E)
    k_full, vt4 = mla_kv(proj, o_kv // MLA_KV_RANK, mla_kv_norm_w, wkv_perm, aux, heads=mla_heads, tm=t_att)
    k4 = k_full.reshape(mla_heads, S // t_att, t_att, dq)
    mla_out = flash_attention(qt, k4, vt4, t=t_att, hp=4)

    hb = 4
    gc_full = chunk_cumsum(aux, GDN_CHUNK, 1024)
    gc = gc_full[:, MLA_ROPE:MLA_ROPE + gdn_heads]
    beta = aux[:, MLA_ROPE + gdn_heads:MLA_ROPE + 2 * gdn_heads]
    gcc = gc.reshape(S, gdn_heads // hb, hb).transpose(1, 0, 2)
    bc = beta.reshape(S, gdn_heads // hb, hb).transpose(1, 0, 2)
    gcr = gc.T.reshape(gdn_heads // hb, hb, S)
    gdn_out = gdn_mixer(proj, c_qkv, c_gate, gdn_conv_w, gcc, bc, gcr, gdn_norm_w,
                        heads=gdn_heads, hb=hb, rows=1024, chunk=GDN_CHUNK)

    h1, h1w, h1_ssq = matmul_res_norm([mla_out, gdn_out], w_out, x, xattn_norm_w, tm=1024, tn=512, name="w_out")

    memn = rms_norm(mem, mem_norm_w, BF16, 256)
    xd = D // XATTN_HEADS
    xq = matmul(h1w, xattn_wq, tm=1024, tn=512, out_dtype=BF16, out_scale=xd ** -0.5 * LOG2E,
                lhs_ssq=h1_ssq, name="xattn_q")
    xk = matmul(memn, xattn_wk, tm=256, tn=512, out_dtype=BF16, name="xattn_k")
    xv = matmul(memn, xattn_wv, tm=256, tn=512, out_dtype=BF16, name="xattn_v")
    xo = xattn_core(xq, xk.T, xv, heads=XATTN_HEADS, tq=512)
    h2, h2w, h2_ssq = matmul_res_norm([xo], xattn_wo, h1, mlp_norm_w, tm=1024, tn=512, name="xattn_o")

    up = matmul(h2w, w_up_bf16, tm=1024, tn=1024, out_dtype=BF16, relu2=True, lhs_ssq=h2_ssq, name="mlp_up")
    down = matmul_kacc(up, mlp_w_down, tm=2048, tn=1024, tk=2048, name="mlp_down")
    return rms_norm(h2, final_norm_w, F32, 256, add=down)


def kernel(x, mem, positions, attn_norm_w, w_in, mla_q_norm_w, mla_w_q_b, mla_kv_norm_w, mla_w_kv_b, gdn_conv_w, gdn_a_log, gdn_dt_bias, gdn_norm_w, w_out, xattn_norm_w, mem_norm_w, xattn_wq, xattn_wk, xattn_wv, xattn_wo, mlp_norm_w, mlp_w_up, mlp_w_down, final_norm_w):
    assert x.shape[0] == 1 and attn_norm_w.shape[0] == 1, "single sequence, single layer"
    out = _block(x[0], mem[0], positions[0], attn_norm_w[0], w_in[0], mla_q_norm_w[0], mla_w_q_b[0],
                 mla_kv_norm_w[0], mla_w_kv_b[0], gdn_conv_w[0], gdn_a_log[0], gdn_dt_bias[0], gdn_norm_w[0],
                 w_out[0], xattn_norm_w[0], mem_norm_w[0], xattn_wq[0], xattn_wk[0], xattn_wv[0], xattn_wo[0],
                 mlp_norm_w[0], mlp_w_up[0], mlp_w_down[0], final_norm_w)
    return out[None]
```

```python
import functools
import math

import jax
import jax.numpy as jnp
from jax import lax
from jax.experimental import pallas as pl
from jax.experimental.pallas import tpu as pltpu

F32 = jnp.float32
BF16 = jnp.bfloat16

HEAD_DIM = 128
MLA_NOPE = 128
MLA_ROPE = 64
MLA_Q_RANK = 1024
MLA_KV_RANK = 512
GDN_CONV = 4
XATTN_HEADS = 4
ROPE_BASE = 10000.0
NORM_EPS = 1e-6
L2_EPS = 1e-6
LOG2E = math.log2(math.e)

V7X_VMEM_BYTES = 64 * 1024 * 1024
VMEM_LIMIT = 58 * 1024 * 1024
assert VMEM_LIMIT < V7X_VMEM_BYTES
LANES = 128
KACC_ROW_CHUNK = 256

GDN_CHUNK = 128


def _cparams(sem):
    return pltpu.CompilerParams(dimension_semantics=sem, vmem_limit_bytes=VMEM_LIMIT)


def _rms_kernel(x_ref, w_ref, o_ref):
    x = x_ref[...].astype(F32)
    r = lax.rsqrt(jnp.mean(x * x, axis=-1, keepdims=True) + NORM_EPS)
    o_ref[...] = (x * r * w_ref[...]).astype(o_ref.dtype)


def _rms_add_kernel(x_ref, y_ref, w_ref, o_ref):
    x = x_ref[...] + y_ref[...]
    r = lax.rsqrt(jnp.mean(x * x, axis=-1, keepdims=True) + NORM_EPS)
    o_ref[...] = (x * r * w_ref[...]).astype(o_ref.dtype)


def rms_norm(x, w, out_dtype, tm, add=None):
    M, D = x.shape
    tm = min(tm, M)
    row = pl.BlockSpec((tm, D), lambda i: (i, 0))
    wspec = pl.BlockSpec((1, D), lambda i: (0, 0))
    if add is None:
        kern, specs, args = _rms_kernel, [row, wspec], (x, w.reshape(1, D))
    else:
        kern, specs, args = _rms_add_kernel, [row, row, wspec], (x, add, w.reshape(1, D))
    return pl.pallas_call(
        kern, grid=(M // tm,), in_specs=specs, out_specs=row,
        out_shape=jax.ShapeDtypeStruct((M, D), out_dtype),
        compiler_params=_cparams(("parallel",)), name="rms_norm")(*args)


def _mm_kernel(*refs, relu2, has_res, out_scale, ssq_dim):
    x_ref, w_ref = refs[0], refs[1]
    res_ref = refs[2] if has_res else None
    ssq_ref = refs[2 + has_res] if ssq_dim else None
    o_ref = refs[2 + has_res + bool(ssq_dim)]
    r = jnp.dot(x_ref[...], w_ref[...].astype(BF16), preferred_element_type=F32)
    if ssq_dim:
        r = r * lax.rsqrt(ssq_ref[:, 0:1] * (1.0 / ssq_dim) + NORM_EPS)
    if relu2:
        r = jnp.square(jnp.maximum(r, 0.0))
    if out_scale is not None:
        r = r * out_scale
    if has_res:
        r = r + res_ref[...]
    o_ref[...] = r.astype(o_ref.dtype)


def matmul(x, w, *, tm, tn, out_dtype, relu2=False, res=None, out_scale=None, lhs_ssq=None, name="matmul"):
    M, K = x.shape
    K2, N = w.shape
    assert K == K2
    tm, tn = min(tm, M), min(tn, N)
    assert M % tm == 0 and N % tn == 0
    in_specs = [pl.BlockSpec((tm, K), lambda i, j: (i, 0)),
                pl.BlockSpec((K, tn), lambda i, j: (0, j))]
    args = [x, w]
    if res is not None:
        in_specs.append(pl.BlockSpec((tm, tn), lambda i, j: (i, j)))
        args.append(res)
    if lhs_ssq is not None:
        in_specs.append(pl.BlockSpec((tm, LANES), lambda i, j: (i, 0)))
        args.append(lhs_ssq)
    kern = functools.partial(_mm_kernel, relu2=relu2, has_res=res is not None, out_scale=out_scale,
                             ssq_dim=K if lhs_ssq is not None else 0)
    return pl.pallas_call(
        kern, grid=(M // tm, N // tn), in_specs=in_specs,
        out_specs=pl.BlockSpec((tm, tn), lambda i, j: (i, j)),
        out_shape=jax.ShapeDtypeStruct((M, N), out_dtype),
        compiler_params=_cparams(("parallel", "parallel")), name=name)(*args)


_NT_DIMS = (((1,), (1,)), ((), ()))


def _proj_nt_kernel(x_ref, wt_ref, cast_src_ref, o_ref, cast_dst_ref):
    wb = wt_ref[...].astype(BF16)
    o_ref[...] = lax.dot_general(x_ref[...], wb, _NT_DIMS, preferred_element_type=F32).astype(o_ref.dtype)
    cast_dst_ref[...] = cast_src_ref[...].astype(cast_dst_ref.dtype)


def _cast_blocks(w, n_steps):
    rows = w.shape[0]
    rb = 16
    while rows // rb > n_steps or rows % rb:
        rb += 16
    return rb, rows // rb


def proj_nt(x, wt, src_row_of_block, n_out, cast_w, *, tm, tn, name):
    M, K = x.shape
    ni, nj = M // tm, n_out // tn
    rb, n_blocks = _cast_blocks(cast_w, ni * nj)
    cast_spec = pl.BlockSpec((rb, cast_w.shape[1]), lambda i, j: (jnp.minimum(i * nj + j, n_blocks - 1), 0))
    return pl.pallas_call(
        _proj_nt_kernel, grid=(ni, nj),
        in_specs=[pl.BlockSpec((tm, K), lambda i, j: (i, 0)),
                  pl.BlockSpec((pl.Element(tn), pl.Element(K)), lambda i, j: (src_row_of_block(j), 0)),
                  cast_spec],
        out_specs=[pl.BlockSpec((tm, tn), lambda i, j: (i, j)), cast_spec],
        out_shape=[jax.ShapeDtypeStruct((M, n_out), BF16), jax.ShapeDtypeStruct(cast_w.shape, BF16)],
        compiler_params=_cparams(("arbitrary", "arbitrary")), name=name)(x, wt, cast_w)


def _proj_small_kernel(x_ref, wa_ref, wb_ref, o_ref):
    wa = wa_ref[...].astype(BF16)
    wb = wb_ref[...].astype(BF16)
    pad = jnp.zeros((LANES - wa.shape[0] - wb.shape[0], wa.shape[1]), BF16)
    w = jnp.concatenate([wa, wb, pad], axis=0)
    o_ref[...] = lax.dot_general(x_ref[...], w, _NT_DIMS, preferred_element_type=F32)


def proj_small(x, wt, row_a, n_a, row_b, n_b, *, tm):
    M, K = x.shape
    return pl.pallas_call(
        _proj_small_kernel, grid=(M // tm,),
        in_specs=[pl.BlockSpec((tm, K), lambda i: (i, 0)),
                  pl.BlockSpec((pl.Element(n_a), pl.Element(K)), lambda i: (row_a, 0)),
                  pl.BlockSpec((pl.Element(n_b), pl.Element(K)), lambda i: (row_b, 0))],
        out_specs=pl.BlockSpec((tm, LANES), lambda i: (i, 0)),
        out_shape=jax.ShapeDtypeStruct((M, LANES), F32),
        compiler_params=_cparams(("parallel",)), name="proj_small")(x, wt, wt)


def _mm_res_kernel(*refs, n_x, has_cast):
    x_refs = refs[:n_x]
    if has_cast:
        w_ref, res_ref, nw_ref, cast_src_ref, h_ref, hw_ref, ssq_ref, cast_dst_ref = refs[n_x:]
        cast_dst_ref[...] = cast_src_ref[...].astype(cast_dst_ref.dtype)
    else:
        w_ref, res_ref, nw_ref, h_ref, hw_ref, ssq_ref = refs[n_x:]
    j = pl.program_id(1)
    wb = w_ref[...].astype(BF16)
    k0 = 0
    acc = res_ref[...]
    for x_ref in x_refs:
        kx = x_ref.shape[1]
        acc = acc + jnp.dot(x_ref[...], wb[k0:k0 + kx], preferred_element_type=F32)
        k0 += kx
    h_ref[...] = acc
    hw_ref[...] = (acc * nw_ref[...]).astype(hw_ref.dtype)
    part = jnp.broadcast_to(jnp.sum(acc * acc, axis=-1, keepdims=True), ssq_ref.shape)

    @pl.when(j == 0)
    def _():
        ssq_ref[...] = part

    @pl.when(j > 0)
    def _():
        ssq_ref[...] += part


def matmul_res_norm(xs, w, res, norm_w, *, tm, tn, name, cast_w=None):
    M = xs[0].shape[0]
    N = w.shape[1]
    ktot = sum(x.shape[1] for x in xs)
    ni, nj = M // tm, N // tn
    tile = pl.BlockSpec((tm, tn), lambda i, j: (i, j))
    in_specs = ([pl.BlockSpec((tm, x.shape[1]), lambda i, j: (i, 0)) for x in xs]
                + [pl.BlockSpec((ktot, tn), lambda i, j: (0, j)), tile, pl.BlockSpec((1, tn), lambda i, j: (0, j))])
    out_specs = [tile, tile, pl.BlockSpec((tm, LANES), lambda i, j: (i, 0))]
    out_shape = [jax.ShapeDtypeStruct((M, N), F32), jax.ShapeDtypeStruct((M, N), BF16),
                 jax.ShapeDtypeStruct((M, LANES), F32)]
    args = [*xs, w, res, norm_w.reshape(1, N)]
    if cast_w is not None:
        rb, n_blocks = _cast_blocks(cast_w, ni * nj)
        cast_spec = pl.BlockSpec((rb, cast_w.shape[1]), lambda i, j: (jnp.minimum(i * nj + j, n_blocks - 1), 0))
        in_specs.append(cast_spec)
        out_specs.append(cast_spec)
        out_shape.append(jax.ShapeDtypeStruct(cast_w.shape, BF16))
        args.append(cast_w)
    return pl.pallas_call(
        functools.partial(_mm_res_kernel, n_x=len(xs), has_cast=cast_w is not None), grid=(ni, nj),
        in_specs=in_specs, out_specs=out_specs, out_shape=out_shape,
        compiler_params=_cparams(("arbitrary", "arbitrary")), name=name)(*args)


def _mm_kacc_kernel(x_ref, w_ref, o_ref, *, row_chunk):
    k = pl.program_id(2)
    wb = w_ref[...].astype(BF16)
    tm = x_ref.shape[0]

    @pl.when(k == 0)
    def _():
        for r in range(0, tm, row_chunk):
            rows = slice(r, r + row_chunk)
            o_ref[rows, :] = jnp.dot(x_ref[rows, :], wb, preferred_element_type=F32)

    @pl.when(k > 0)
    def _():
        for r in range(0, tm, row_chunk):
            rows = slice(r, r + row_chunk)
            o_ref[rows, :] += jnp.dot(x_ref[rows, :], wb, preferred_element_type=F32)


def matmul_kacc(x, w, *, tm, tn, tk, name, row_chunk=KACC_ROW_CHUNK):
    M, K = x.shape
    N = w.shape[1]
    tm, tn, tk = min(tm, M), min(tn, N), min(tk, K)
    assert M % tm == 0 and N % tn == 0 and K % tk == 0 and tm % row_chunk == 0
    return pl.pallas_call(
        functools.partial(_mm_kacc_kernel, row_chunk=row_chunk), grid=(M // tm, N // tn, K // tk),
        in_specs=[pl.BlockSpec((tm, tk), lambda i, j, k: (i, k)),
                  pl.BlockSpec((tk, tn), lambda i, j, k: (k, j))],
        out_specs=pl.BlockSpec((tm, tn), lambda i, j, k: (i, j)),
        out_shape=jax.ShapeDtypeStruct((M, N), F32),
        compiler_params=_cparams(("parallel", "parallel", "arbitrary")), name=name)(x, w)


def _gates_kernel(y_ref, pos_ref, invf_ref, alog_ref, dtb_ref, aux_ref, cos_ref, sin_ref):
    y = y_ref[...]
    ang = pos_ref[...].astype(F32) * invf_ref[...]
    c = jnp.cos(ang)
    s = jnp.sin(ang)
    cos_ref[...] = c
    sin_ref[...] = s
    lane = lax.broadcasted_iota(jnp.int32, y.shape, 1)
    half = MLA_ROPE // 2
    rot = jnp.where(lane < half, -pltpu.roll(y, LANES - half, 1), pltpu.roll(y, half, 1))
    roped = y * c + rot * s
    z = y + dtb_ref[...]
    softplus = jnp.maximum(z, 0.0) + jnp.log1p(jnp.exp(-jnp.abs(z)))
    g = -jnp.exp(alog_ref[...]) * softplus
    beta = jax.nn.sigmoid(y)
    aux_ref[...] = jnp.where(lane < MLA_ROPE, roped, jnp.where(lane < MLA_ROPE + 16, g, beta))


def gates(y, pos_col, invf_row, alog_row, dtb_row, tm):
    M = y.shape[0]
    tm = min(tm, M)
    row = pl.BlockSpec((tm, LANES), lambda i: (i, 0))
    one = pl.BlockSpec((1, LANES), lambda i: (0, 0))
    shp = jax.ShapeDtypeStruct((M, LANES), F32)
    return pl.pallas_call(
        _gates_kernel, grid=(M // tm,),
        in_specs=[row, pl.BlockSpec((tm, 1), lambda i: (i, 0)), one, one, one],
        out_specs=[row, row, row], out_shape=[shp, shp, shp],
        compiler_params=_cparams(("parallel",)), name="gates")(y, pos_col, invf_row, alog_row, dtb_row)


def _mla_q_kernel(cq_ref, nw_ref, wq_ref, cos_ref, sin_ref, o_ref, *, heads, scale):
    x = cq_ref[...].astype(F32)
    r = lax.rsqrt(jnp.mean(x * x, axis=-1, keepdims=True) + NORM_EPS)
    xn = (x * r * nw_ref[...]).astype(BF16)
    q = jnp.dot(xn, wq_ref[...], preferred_element_type=F32) * scale
    half = MLA_ROPE // 2
    n_nope = heads * MLA_NOPE
    n_half = heads * half
    reps = n_half // LANES
    c = jnp.concatenate([cos_ref[...]] * reps, axis=1)
    s = jnp.concatenate([sin_ref[...]] * reps, axis=1)
    p1 = q[:, n_nope:n_nope + n_half]
    p2 = q[:, n_nope + n_half:]
    r1t = (p1 * c - p2 * s).T.astype(BF16)
    r2t = (p2 * c + p1 * s).T.astype(BF16)
    for h in range(heads):
        o_ref[h, 0:MLA_NOPE, :] = q[:, h * MLA_NOPE:(h + 1) * MLA_NOPE].T.astype(BF16)
        o_ref[h, MLA_NOPE:MLA_NOPE + half, :] = r1t[h * half:(h + 1) * half, :]
        o_ref[h, MLA_NOPE + half:MLA_NOPE + 2 * half, :] = r2t[h * half:(h + 1) * half, :]


def mla_q(proj, nw, wq, cos, sin, *, heads, tm, scale):
    S = proj.shape[0]
    R = nw.shape[0]
    dq = MLA_NOPE + MLA_ROPE
    kern = functools.partial(_mla_q_kernel, heads=heads, scale=scale)
    return pl.pallas_call(
        kern, grid=(S // tm,),
        in_specs=[pl.BlockSpec((tm, R), lambda i: (i, 0)),
                  pl.BlockSpec((1, R), lambda i: (0, 0)),
                  pl.BlockSpec(wq.shape, lambda i: (0, 0)),
                  pl.BlockSpec((tm, LANES), lambda i: (i, 0)),
                  pl.BlockSpec((tm, LANES), lambda i: (i, 0))],
        out_specs=pl.BlockSpec((heads, dq, tm), lambda i: (0, 0, i)),
        out_shape=jax.ShapeDtypeStruct((heads, dq, S), BF16),
        compiler_params=_cparams(("parallel",)), name="mla_q")(proj, nw.reshape(1, R), wq, cos, sin)


def _mla_kv_kernel(ckv_ref, nw_ref, wkv_ref, aux_ref, k_ref, vt_ref, *, heads):
    x = ckv_ref[...].astype(F32)
    r = lax.rsqrt(jnp.mean(x * x, axis=-1, keepdims=True) + NORM_EPS)
    xn = (x * r * nw_ref[...]).astype(BF16)
    kv = jnp.dot(xn, wkv_ref[...], preferred_element_type=F32)
    kpe = aux_ref[...][:, 0:MLA_ROPE].astype(BF16)
    n_nope = heads * MLA_NOPE
    for h in range(heads):
        k_ref[h, :, 0:MLA_NOPE] = kv[:, h * MLA_NOPE:(h + 1) * MLA_NOPE].astype(BF16)
        k_ref[h, :, MLA_NOPE:MLA_NOPE + MLA_ROPE] = kpe
        vt_ref[h, 0] = kv[:, n_nope + h * HEAD_DIM:n_nope + (h + 1) * HEAD_DIM].T.astype(BF16)


def mla_kv(proj, col_block, nw, wkv, aux, *, heads, tm):
    S = proj.shape[0]
    R = nw.shape[0]
    dq = MLA_NOPE + MLA_ROPE
    kern = functools.partial(_mla_kv_kernel, heads=heads)
    return pl.pallas_call(
        kern, grid=(S // tm,),
        in_specs=[pl.BlockSpec((tm, R), lambda i: (i, col_block)),
                  pl.BlockSpec((1, R), lambda i: (0, 0)),
                  pl.BlockSpec(wkv.shape, lambda i: (0, 0)),
                  pl.BlockSpec((tm, LANES), lambda i: (i, 0))],
        out_specs=[pl.BlockSpec((heads, tm, dq), lambda i: (0, i, 0)),
                   pl.BlockSpec((heads, 1, HEAD_DIM, tm), lambda i: (0, i, 0, 0))],
        out_shape=[jax.ShapeDtypeStruct((heads, S, dq), BF16),
                   jax.ShapeDtypeStruct((heads, S // tm, HEAD_DIM, tm), BF16)],
        compiler_params=_cparams(("parallel",)), name="mla_kv")(proj, nw.reshape(1, R), wkv, aux)


def _flash_kernel(qt_ref, k_ref, vt_ref, o_ref, acc_ref, m_ref, l_ref, sa_ref, sb_ref, *, t, hp):
    i = pl.program_id(1)
    m_ref[...] = jnp.full(m_ref.shape, -jnp.inf, F32)
    l_ref[...] = jnp.zeros(l_ref.shape, F32)
    acc_ref[...] = jnp.zeros(acc_ref.shape, F32)
    heads = range(hp)

    def scores(j, s_ref):
        for h in heads:
            s_ref[h] = jnp.dot(k_ref[h, j], qt_ref[h], preferred_element_type=F32)

    def softmax_pv(j, s_ref, masked):
        s = [s_ref[h] for h in heads]
        if masked:
            kpos = lax.broadcasted_iota(jnp.int32, (t, t), 0)
            qpos = lax.broadcasted_iota(jnp.int32, (t, t), 1)
            s = [jnp.where(kpos <= qpos, sh, -jnp.inf) for sh in s]
        m_old = [m_ref[h] for h in heads]
        m_new = [jnp.maximum(m_old[h], jnp.max(s[h], axis=0, keepdims=True)) for h in heads]
        alpha = [jnp.exp2(m_old[h] - m_new[h]) for h in heads]
        p = [jnp.exp2(s[h] - m_new[h]) for h in heads]
        for h in heads:
            l_ref[h] = alpha[h] * l_ref[h] + jnp.sum(p[h], axis=0, keepdims=True)
            m_ref[h] = m_new[h]
        pv = [jnp.dot(vt_ref[h, j], p[h].astype(BF16), preferred_element_type=F32) for h in heads]
        for h in heads:
            acc_ref[h] = alpha[h] * acc_ref[h] + pv[h]

    scores(0, sa_ref)

    def body(jj, carry):
        j = 2 * jj
        scores(j + 1, sb_ref)
        softmax_pv(j, sa_ref, False)
        scores(j + 2, sa_ref)
        softmax_pv(j + 1, sb_ref, False)
        return carry

    lax.fori_loop(0, i // 2, body, 0)

    @pl.when(i % 2 == 0)
    def _():
        softmax_pv(i, sa_ref, True)

    @pl.when(i % 2 == 1)
    def _():
        scores(i, sb_ref)
        softmax_pv(i - 1, sa_ref, False)
        softmax_pv(i, sb_ref, True)

    for h in range(hp):
        o_ref[:, h * HEAD_DIM:(h + 1) * HEAD_DIM] = (acc_ref[h] / l_ref[h]).T.astype(o_ref.dtype)


def flash_attention(qt, k4, vt4, *, t, hp):
    H, dq, S = qt.shape
    nk = S // t
    kern = functools.partial(_flash_kernel, t=t, hp=hp)
    whole_head = pl.Buffered(1)
    return pl.pallas_call(
        kern, grid=(H // hp, nk),
        in_specs=[pl.BlockSpec((hp, dq, t), lambda g, i: (g, 0, i)),
                  pl.BlockSpec((hp, nk, t, dq), lambda g, i: (g, 0, 0, 0), pipeline_mode=whole_head),
                  pl.BlockSpec((hp, nk, HEAD_DIM, t), lambda g, i: (g, 0, 0, 0), pipeline_mode=whole_head)],
        out_specs=pl.BlockSpec((t, hp * HEAD_DIM), lambda g, i: (i, g)),
        out_shape=jax.ShapeDtypeStruct((S, H * HEAD_DIM), BF16),
        scratch_shapes=[pltpu.VMEM((hp, HEAD_DIM, t), F32), pltpu.VMEM((hp, 1, t), F32),
                        pltpu.VMEM((hp, 1, t), F32), pltpu.VMEM((hp, t, t), F32), pltpu.VMEM((hp, t, t), F32)],
        compiler_params=_cparams(("parallel", "arbitrary")), name="mla_flash")(qt, k4, vt4)


def _cumsum_kernel(g_ref, o_ref, *, chunk):
    r = lax.broadcasted_iota(jnp.int32, (chunk, chunk), 0)
    c = lax.broadcasted_iota(jnp.int32, (chunk, chunk), 1)
    trilb = jnp.where(r >= c, 1.0, 0.0).astype(BF16)
    for r0 in range(0, g_ref.shape[0], chunk):
        g = g_ref[r0:r0 + chunk, :]
        g_hi = g.astype(BF16).astype(F32)
        g_mid = (g - g_hi).astype(BF16).astype(F32)
        g_lo = g - g_hi - g_mid
        o_ref[r0:r0 + chunk, :] = (jnp.dot(trilb, g_hi.astype(BF16), preferred_element_type=F32)
                                   + jnp.dot(trilb, g_mid.astype(BF16), preferred_element_type=F32)
                                   + jnp.dot(trilb, g_lo.astype(BF16), preferred_element_type=F32))


def chunk_cumsum(g, chunk, tm):
    M = g.shape[0]
    tm = min(tm, M)
    row = pl.BlockSpec((tm, LANES), lambda i: (i, 0))
    return pl.pallas_call(
        functools.partial(_cumsum_kernel, chunk=chunk), grid=(M // tm,), in_specs=[row], out_specs=row,
        out_shape=jax.ShapeDtypeStruct((M, LANES), F32),
        compiler_params=_cparams(("parallel",)), name="gdn_cumsum")(g)


def _bdot(a, b):
    return jnp.dot(a.astype(BF16), b.astype(BF16), preferred_element_type=F32)


def _conv_silu(x_ref, halo_ref, w_ref, xs_ref, first):
    rows = x_ref.shape[0]
    x = x_ref[...].astype(F32)
    xs_ref[0:8, :] = jnp.where(first, 0.0, halo_ref[...].astype(F32)[8:16, :])
    xs_ref[8:, :] = x
    w = w_ref[...]
    acc = x * w[GDN_CONV - 1:GDN_CONV, :]
    for s in range(1, GDN_CONV):
        acc = acc + xs_ref[8 - s:8 - s + rows, :] * w[GDN_CONV - 1 - s:GDN_CONV - s, :]
    return acc * jax.nn.sigmoid(acc)


def _l2n(y, scale):
    return y * (lax.rsqrt(jnp.sum(y * y, axis=-1, keepdims=True) + L2_EPS) * scale)


def _gdn_kernel(xq_ref, hq_ref, wq_ref, xk_ref, hk_ref, wk_ref, xv_ref, hv_ref, wv_ref, gate_ref,
                gcc_ref, bc_ref, gcr_ref, nw_ref, o_ref, s_ref, sq_ref, sk_ref, sv_ref, *, hb, rows, chunk):
    first = pl.program_id(1) == 0

    @pl.when(first)
    def _():
        s_ref[...] = jnp.zeros(s_ref.shape, F32)

    C = chunk
    ri = lax.broadcasted_iota(jnp.int32, (C, C), 0)
    ci = lax.broadcasted_iota(jnp.int32, (C, C), 1)
    lower = ri >= ci
    strict = ri > ci
    xor = ri ^ ci
    nw = nw_ref[...]
    n_levels = C.bit_length() - 1
    n_chunks = rows // C
    bodies = [(c, h) for c in range(n_chunks) for h in range(hb)]

    def rows_of(c):
        return slice(c * C, (c + 1) * C)

    def cols_of(h):
        return slice(h * HEAD_DIM, (h + 1) * HEAD_DIM)

    yq = _conv_silu(xq_ref, hq_ref, wq_ref, sq_ref, first)
    yk = _conv_silu(xk_ref, hk_ref, wk_ref, sk_ref, first)
    yv = _conv_silu(xv_ref, hv_ref, wv_ref, sv_ref, first)

    amats, rhs, qks, qds, tails, eglasts = {}, {}, {}, {}, {}, {}
    for b in bodies:
        c, h = b
        q = _l2n(yq[rows_of(c), cols_of(h)], HEAD_DIM ** -0.5)
        k = _l2n(yk[rows_of(c), cols_of(h)], 1.0)
        v = yv[rows_of(c), cols_of(h)]
        kt = k.T
        gcc = gcc_ref[rows_of(c), h:h + 1]
        bc = bc_ref[rows_of(c), h:h + 1]
        gcr = gcr_ref[h:h + 1, rows_of(c)]
        glast = gcr[:, C - 1:C]
        decay = jnp.exp(jnp.where(lower, gcc - gcr, -jnp.inf))
        egc = jnp.exp(gcc)
        kb = k * bc
        amats[b] = jnp.where(strict, _bdot(kb, kt) * decay, 0.0)
        rhs[b] = jnp.concatenate([v * bc, kb * egc], axis=1)
        qks[b] = _bdot(q, kt) * decay
        qds[b] = q * egc
        tails[b] = kt * jnp.exp(glast - gcr)
        eglasts[b] = jnp.exp(glast)

    tinv = {b: jnp.where(ri == ci, 1.0, 0.0) - jnp.where(strict & (xor == 1), amats[b], 0.0) for b in bodies}
    for level in range(1, n_levels):
        join = strict & ((xor >> level) == 1)
        mids = {b: _bdot(jnp.where(join, amats[b], 0.0), tinv[b]) for b in bodies}
        for b in bodies:
            tinv[b] = tinv[b] - _bdot(tinv[b], mids[b])
    xs = {b: _bdot(tinv[b], rhs[b]) for b in bodies}

    states = [s_ref[h] for h in range(hb)]
    for c in range(n_chunks):
        ws = [_bdot(jnp.concatenate([xs[(c, h)][:, HEAD_DIM:], qds[(c, h)]], axis=0), states[h]) for h in range(hb)]
        v_new = [xs[(c, h)][:, :HEAD_DIM] - ws[h][:C] for h in range(hb)]
        outs = [ws[h][C:] + _bdot(qks[(c, h)], v_new[h]) for h in range(hb)]
        states = [states[h] * eglasts[(c, h)] + _bdot(tails[(c, h)], v_new[h]) for h in range(hb)]
        for h in range(hb):
            o = outs[h]
            o = o * lax.rsqrt(jnp.mean(o * o, axis=-1, keepdims=True) + NORM_EPS) * nw
            gate = gate_ref[rows_of(c), cols_of(h)].astype(F32)
            o_ref[rows_of(c), cols_of(h)] = (o * (gate * jax.nn.sigmoid(gate))).astype(o_ref.dtype)
    for h in range(hb):
        s_ref[h] = states[h]


def gdn_mixer(proj, qkv_col0, gate_col0, conv_w, gcc, bc, gcr, nw, *, heads, hb, rows, chunk):
    S = proj.shape[0]
    tc = hb * HEAD_DIM
    gw = heads * HEAD_DIM
    halo_blocks = rows // 16
    assert qkv_col0 % tc == 0 and gate_col0 % tc == 0 and gw % tc == 0 and S % rows == 0 and rows % chunk == 0

    def section(col0, w_col0):
        cb, wb = col0 // tc, w_col0 // tc
        return [pl.BlockSpec((rows, tc), lambda g, r: (r, cb + g)),
                pl.BlockSpec((16, tc), lambda g, r: (jnp.maximum(r * halo_blocks - 1, 0), cb + g)),
                pl.BlockSpec((GDN_CONV, tc), lambda g, r: (0, wb + g))]

    in_specs = (section(qkv_col0, 0) + section(qkv_col0 + gw, gw) + section(qkv_col0 + 2 * gw, 2 * gw)
                + [pl.BlockSpec((rows, tc), lambda g, r: (r, gate_col0 // tc + g)),
                   pl.BlockSpec((None, rows, hb), lambda g, r: (g, r, 0)),
                   pl.BlockSpec((None, rows, hb), lambda g, r: (g, r, 0)),
                   pl.BlockSpec((None, hb, rows), lambda g, r: (g, 0, r)),
                   pl.BlockSpec((1, HEAD_DIM), lambda g, r: (0, 0))])
    stage = pltpu.VMEM((rows + 8, tc), F32)
    return pl.pallas_call(
        functools.partial(_gdn_kernel, hb=hb, rows=rows, chunk=chunk), grid=(heads // hb, S // rows),
        in_specs=in_specs,
        out_specs=pl.BlockSpec((rows, tc), lambda g, r: (r, g)),
        out_shape=jax.ShapeDtypeStruct((S, gw), BF16),
        scratch_shapes=[pltpu.VMEM((hb, HEAD_DIM, HEAD_DIM), F32), stage, stage, stage],
        compiler_params=_cparams(("parallel", "arbitrary")), name="gdn_mixer")(
            proj, proj, conv_w, proj, proj, conv_w, proj, proj, conv_w, proj, gcc, bc, gcr, nw.reshape(1, HEAD_DIM))


def _xattn_kernel(q_ref, kt_ref, v_ref, o_ref, *, heads):
    d = q_ref.shape[1] // heads
    cols = [slice(h * d, (h + 1) * d) for h in range(heads)]
    s = [jnp.dot(q_ref[:, c], kt_ref[c, :], preferred_element_type=F32) for c in cols]
    p = [jnp.exp2(sh - jnp.max(sh, axis=-1, keepdims=True)) for sh in s]
    l = [jnp.sum(ph, axis=-1, keepdims=True) for ph in p]
    o = [jnp.dot(ph.astype(BF16), v_ref[:, c], preferred_element_type=F32) for ph, c in zip(p, cols)]
    for c, oh, lh in zip(cols, o, l):
        o_ref[:, c] = (oh / lh).astype(o_ref.dtype)


def xattn_core(q, kt, v, *, heads, tq):
    S, W = q.shape
    M = v.shape[0]
    return pl.pallas_call(
        functools.partial(_xattn_kernel, heads=heads), grid=(S // tq,),
        in_specs=[pl.BlockSpec((tq, W), lambda i: (i, 0)),
                  pl.BlockSpec((W, M), lambda i: (0, 0)),
                  pl.BlockSpec((M, W), lambda i: (0, 0))],
        out_specs=pl.BlockSpec((tq, W), lambda i: (i, 0)),
        out_shape=jax.ShapeDtypeStruct((S, W), BF16),
        compiler_params=_cparams(("parallel",)), name="xattn_core")(q, kt, v)


def _block(x, mem, positions, attn_norm_w, w_in, mla_q_norm_w, mla_w_q_b, mla_kv_norm_w, mla_w_kv_b,
           gdn_conv_w, gdn_a_log, gdn_dt_bias, gdn_norm_w, w_out, xattn_norm_w, mem_norm_w,
           xattn_wq, xattn_wk, xattn_wv, xattn_wo, mlp_norm_w, mlp_w_up, mlp_w_down, final_norm_w):
    S, D = x.shape
    n_mix = w_out.shape[0]
    mla_heads = n_mix // 2 // HEAD_DIM
    gdn_heads = gdn_conv_w.shape[1] // (3 * HEAD_DIM)
    gw = gdn_heads * HEAD_DIM
    dq = MLA_NOPE + MLA_ROPE
    half = MLA_ROPE // 2

    o_kv = MLA_Q_RANK
    o_pe = o_kv + MLA_KV_RANK
    o_qkv = o_pe + MLA_ROPE
    o_a = o_qkv + 3 * gw
    o_b = o_a + gdn_heads
    o_gate = o_b + gdn_heads
    w_in_t = w_in.T
    c_qkv = o_pe
    c_gate = o_pe + 3 * gw
    n_proj = c_gate + (w_in.shape[1] - o_gate)
    tn_proj = 512

    skip1 = o_qkv - c_qkv
    skip2 = (o_gate - c_gate) - skip1
    unit = math.gcd(tn_proj, skip1, skip2)

    def proj_src_row(j):
        v = j * tn_proj
        past1 = (v >= c_qkv).astype(jnp.int32)
        past2 = (v >= c_gate).astype(jnp.int32)
        return (j * (tn_proj // unit) + past1 * (skip1 // unit) + past2 * (skip2 // unit)) * unit

    wq3 = mla_w_q_b.reshape(MLA_Q_RANK, mla_heads, dq)
    wq_perm = jnp.concatenate([wq3[:, :, :MLA_NOPE].reshape(MLA_Q_RANK, -1),
                               wq3[:, :, MLA_NOPE:MLA_NOPE + half].reshape(MLA_Q_RANK, -1),
                               wq3[:, :, MLA_NOPE + half:].reshape(MLA_Q_RANK, -1)], axis=1).astype(BF16)
    wkv3 = mla_w_kv_b.reshape(MLA_KV_RANK, mla_heads, MLA_NOPE + HEAD_DIM)
    wkv_perm = jnp.concatenate([wkv3[:, :, :MLA_NOPE].reshape(MLA_KV_RANK, -1),
                                wkv3[:, :, MLA_NOPE:].reshape(MLA_KV_RANK, -1)], axis=1).astype(BF16)

    inv_freq = ROPE_BASE ** (-jnp.arange(half, dtype=F32) / half)
    invf_row = jnp.tile(inv_freq, LANES // half).reshape(1, LANES)
    pad_l = jnp.zeros((MLA_ROPE,), F32)
    pad_r = jnp.zeros((LANES - MLA_ROPE - gdn_heads,), F32)
    alog_row = jnp.concatenate([pad_l, gdn_a_log.astype(F32), pad_r]).reshape(1, LANES)
    dtb_row = jnp.concatenate([pad_l, gdn_dt_bias.astype(F32), pad_r]).reshape(1, LANES)

    xn = rms_norm(x, attn_norm_w, BF16, 512)
    proj, w_up_bf16 = proj_nt(xn, w_in_t, proj_src_row, n_proj, mlp_w_up, tm=1024, tn=tn_proj, name="proj_main")
    small = proj_small(xn, w_in_t, o_pe, MLA_ROPE, o_a, 2 * gdn_heads, tm=1024)
    aux, cos, sin = gates(small, positions.reshape(S, 1), invf_row, alog_row, dtb_row, 1024)

    t_att = 512
    qt = mla_q(proj, mla_q_norm_w, wq_perm, cos, sin, heads=mla_heads, tm=512, scale=dq ** -0.5 * LOG2E)
    k_full, vt4 = mla_kv(proj, o_kv // MLA_KV_RANK, mla_kv_norm_w, wkv_perm, aux, heads=mla_heads, tm=t_att)
    k4 = k_full.reshape(mla_heads, S // t_att, t_att, dq)
    mla_out = flash_attention(qt, k4, vt4, t=t_att, hp=4)

    hb = 4
    gc_full = chunk_cumsum(aux, GDN_CHUNK, 1024)
    gc = gc_full[:, MLA_ROPE:MLA_ROPE + gdn_heads]
    beta = aux[:, MLA_ROPE + gdn_heads:MLA_ROPE + 2 * gdn_heads]
    gcc = gc.reshape(S, gdn_heads // hb, hb).transpose(1, 0, 2)
    bc = beta.reshape(S, gdn_heads // hb, hb).transpose(1, 0, 2)
    gcr = gc.T.reshape(gdn_heads // hb, hb, S)
    gdn_out = gdn_mixer(proj, c_qkv, c_gate, gdn_conv_w, gcc, bc, gcr, gdn_norm_w,
                        heads=gdn_heads, hb=hb, rows=512, chunk=GDN_CHUNK)

    h1, h1w, h1_ssq, wq_bf16 = matmul_res_norm([mla_out, gdn_out], w_out, x, xattn_norm_w, tm=1024, tn=512,
                                               name="w_out", cast_w=xattn_wq)

    memn = rms_norm(mem, mem_norm_w, BF16, 256)
    xd = D // XATTN_HEADS
    xq = matmul(h1w, wq_bf16, tm=1024, tn=1024, out_dtype=BF16, out_scale=xd ** -0.5 * LOG2E,
                lhs_ssq=h1_ssq, name="xattn_q")
    xk = matmul(memn, xattn_wk, tm=256, tn=512, out_dtype=BF16, name="xattn_k")
    xv = matmul(memn, xattn_wv, tm=256, tn=512, out_dtype=BF16, name="xattn_v")
    xo = xattn_core(xq, xk.T, xv, heads=XATTN_HEADS, tq=512)
    h2, h2w, h2_ssq = matmul_res_norm([xo], xattn_wo, h1, mlp_norm_w, tm=1024, tn=512, name="xattn_o")

    up = matmul(h2w, w_up_bf16, tm=1024, tn=1024, out_dtype=BF16, relu2=True, lhs_ssq=h2_ssq, name="mlp_up")
    down = matmul_kacc(up, mlp_w_down, tm=2048, tn=1024, tk=2048, name="mlp_down")
    return rms_norm(h2, final_norm_w, F32, 256, add=down)


def kernel(x, mem, positions, attn_norm_w, w_in, mla_q_norm_w, mla_w_q_b, mla_kv_norm_w, mla_w_kv_b, gdn_conv_w, gdn_a_log, gdn_dt_bias, gdn_norm_w, w_out, xattn_norm_w, mem_norm_w, xattn_wq, xattn_wk, xattn_wv, xattn_wo, mlp_norm_w, mlp_w_up, mlp_w_down, final_norm_w):
    assert x.shape[0] == 1 and attn_norm_w.shape[0] == 1, "single sequence, single layer"
    out = _block(x[0], mem[0], positions[0], attn_norm_w[0], w_in[0], mla_q_norm_w[0], mla_w_q_b[0],
                 mla_kv_norm_w[0], mla_w_kv_b[0], gdn_conv_w[0], gdn_a_log[0], gdn_dt_bias[0], gdn_norm_w[0],
                 w_out[0], xattn_norm_w[0], mem_norm_w[0], xattn_wq[0], xattn_wk[0], xattn_wv[0], xattn_wo[0],
                 mlp_norm_w[0], mlp_w_up[0], mlp_w_down[0], final_norm_w)
    return out[None]
```
